```python
import math
import jax, jax.numpy as jnp
from jax import lax
import numpy as np

D_MODEL = 2048
BATCH = 16
SEQ = 256
DEPTH = 4
DEC_BATCH = 4
DEC_SEQ = 2048
PAST_LEN = 256

GRID_W = 64
HEAD_DIM = 128
N_HEADS = D_MODEL // HEAD_DIM
WIN_HEADS = N_HEADS // 4
WIN_KV = WIN_HEADS // 2
AX_HEADS = N_HEADS // 2
AX_KV = AX_HEADS // 4
DIFF_HEADS = N_HEADS - WIN_HEADS - AX_HEADS
DIFF_HALF = HEAD_DIM // 2
IN_HEADS = (WIN_HEADS, WIN_KV, WIN_KV, AX_HEADS, AX_KV, AX_KV, DIFF_HEADS, DIFF_HEADS, DIFF_HEADS)
IN_COLS = HEAD_DIM * (WIN_HEADS + 2 * WIN_KV + AX_HEADS + 2 * AX_KV + 3 * DIFF_HEADS)
WINDOW = 128
BLOCK = 128
ROPE_THETA = 10000.0
PEER_HEADS = 8
PEER_KEYS = 128
PEER_EXPERTS = PEER_KEYS * PEER_KEYS
PEER_QDIM = 256
PEER_TOPK = 16
PEER_CHUNK = 128
EPS = 1e-6
NEG_INF = -1e30

kernel_name = 'hybrid_diffusion_step'


def rmsnorm(x, g):
    xf = x.astype(jnp.float32)
    y = xf * lax.rsqrt(jnp.mean(xf * xf, axis=-1, keepdims=True) + EPS)
    return (y * g.astype(jnp.float32)).astype(x.dtype)


def rope_1d(x, pos):
    half = x.shape[-1] // 2
    freqs = ROPE_THETA ** (-jnp.arange(half, dtype=jnp.float32) / half)
    ang = pos.astype(jnp.float32)[:, None] * freqs[None, :]
    cos = jnp.cos(ang)[:, None, :]
    sin = jnp.sin(ang)[:, None, :]
    x1 = x[..., :half].astype(jnp.float32)
    x2 = x[..., half:].astype(jnp.float32)
    return jnp.concatenate([x1 * cos - x2 * sin, x1 * sin + x2 * cos], axis=-1).astype(x.dtype)


def axial_rope(x, row_pos, col_pos):
    half = x.shape[-1] // 2
    return jnp.concatenate([rope_1d(x[..., :half], row_pos), rope_1d(x[..., half:], col_pos)], axis=-1)


def rope_halves(x, row_pos, col_pos):
    B, T, H, D = x.shape
    return axial_rope(x.reshape(B, T, H * 2, D // 2), row_pos, col_pos).reshape(B, T, H, D)


def adaln(cond, w, b):
    m = jax.nn.silu(cond) @ w + b
    return [t[..., None, :] for t in jnp.split(m, 6, axis=-1)]


def project_heads(h, w_in_l):
    z = h @ w_in_l
    B, T = h.shape[:2]
    parts = []
    off = 0
    for n_h in IN_HEADS:
        parts.append(z[..., off:off + n_h * HEAD_DIM].reshape(B, T, n_h, HEAD_DIM))
        off += n_h * HEAD_DIM
    return parts


def gqa_attend(q, k, v):
    B, Q, H, D = q.shape
    KV = k.shape[2]
    qg = q.reshape(B, Q, KV, H // KV, D)
    s = jnp.einsum('bqkgd,bskd->bkgqs', qg, k).astype(jnp.float32) * (D ** -0.5)
    p = jax.nn.softmax(s, axis=-1).astype(v.dtype)
    o = jnp.einsum('bkgqs,bskd->bqkgd', p, v)
    return o.reshape(B, Q, H * D)


def sink_attend(q, k, v, sink, mask=None):
    N, Q, H, D = q.shape
    KV = k.shape[2]
    G = H // KV
    qg = q.reshape(N, Q, KV, G, D)
    s = jnp.einsum('nqkgd,nskd->nkgqs', qg, k).astype(jnp.float32) * (D ** -0.5)
    if mask is not None:
        s = jnp.where(mask[:, None, None], s, NEG_INF)
    sink_col = jnp.broadcast_to(sink.astype(jnp.float32).reshape(1, KV, G, 1, 1), s.shape[:-1] + (1,))
    p = jax.nn.softmax(jnp.concatenate([s, sink_col], axis=-1), axis=-1)[..., :-1]
    o = jnp.einsum('nkgqs,nskd->nqkgd', p.astype(v.dtype), v)
    return o.reshape(N, Q, H * D)


def diff_attend(q, k, v, lam, lam_init, gain):
    B, Q, H, D = q.shape
    S = k.shape[1]
    qh = q.reshape(B, Q, H, 2, D // 2)
    kh = k.reshape(B, S, H, 2, D // 2)
    s = jnp.einsum('bqhcd,bshcd->bhcqs', qh, kh).astype(jnp.float32) * ((D // 2) ** -0.5)
    p = jax.nn.softmax(s, axis=-1)
    pd = p[:, :, 0] - lam * p[:, :, 1]
    o = jnp.einsum('bhqs,bshd->bqhd', pd.astype(v.dtype), v)
    o = rmsnorm(o, gain) * (1.0 - lam_init)
    return o.reshape(B, Q, H * D)


def diff_lambda(lp, lam_init):
    f = jnp.float32
    return (jnp.exp(jnp.sum(lp['lam_q1'].astype(f) * lp['lam_k1'].astype(f)))
            - jnp.exp(jnp.sum(lp['lam_q2'].astype(f) * lp['lam_k2'].astype(f))) + lam_init)


def sweep_query_blocks(fn, q):
    B, T = q.shape[:2]
    nb = T // BLOCK
    qb = jnp.moveaxis(q.reshape((B, nb, BLOCK) + q.shape[2:]), 1, 0)
    out = lax.map(fn, qb)
    return jnp.moveaxis(out, 0, 1).reshape(B, T, out.shape[-1])


def window_attend_latent(q, k, v, kc, vc, sink):
    B, L, H, D = q.shape
    KV = k.shape[2]
    Sc = kc.shape[1]
    nb = L // BLOCK

    def band(t):
        tb = t.reshape(B, nb, BLOCK, KV, D)
        tp = jnp.pad(tb, ((0, 0), (1, 1), (0, 0), (0, 0), (0, 0)))
        return jnp.concatenate([tp[:, :-2], tp[:, 1:-1], tp[:, 2:]], axis=2)

    def with_ctx(t_band, tc):
        tcb = jnp.broadcast_to(tc[:, None], (B, nb) + tc.shape[1:])
        return jnp.concatenate([t_band, tcb], axis=2).reshape(B * nb, 3 * BLOCK + Sc, KV, D)

    k_all = with_ctx(band(k), kc)
    v_all = with_ctx(band(v), vc)
    q_pos = jnp.arange(L).reshape(nb, BLOCK)
    k_pos = jnp.arange(nb)[:, None] * BLOCK - BLOCK + jnp.arange(3 * BLOCK)[None, :]
    kp = k_pos[:, None, :]
    lat_ok = (kp >= 0) & (kp < L) & (jnp.abs(q_pos[:, :, None] - kp) <= WINDOW)
    mask = jnp.concatenate([lat_ok, jnp.ones((nb, BLOCK, Sc), dtype=bool)], axis=-1)
    mask = jnp.broadcast_to(mask[None], (B,) + mask.shape).reshape(B * nb, BLOCK, 3 * BLOCK + Sc)
    o = sink_attend(q.reshape(B * nb, BLOCK, H, D), k_all, v_all, sink, mask)
    return o.reshape(B, L, H * D)


def peer_ffn(h, wq, sub_keys, u, v):
    B, T, D = h.shape
    n = B * T
    x = h.reshape(n, D)
    q = (x @ wq).reshape(n, PEER_HEADS, 2, PEER_QDIM // 2)
    s = jnp.einsum('nhcd,hckd->nhck', q, sub_keys).astype(jnp.float32)
    sv, si = lax.top_k(s, PEER_TOPK)
    cand_s = (sv[:, :, 0, :, None] + sv[:, :, 1, None, :]).reshape(n, PEER_HEADS, PEER_TOPK * PEER_TOPK)
    cand_i = (si[:, :, 0, :, None] * PEER_KEYS + si[:, :, 1, None, :]).reshape(n, PEER_HEADS, PEER_TOPK * PEER_TOPK)
    top_s, top_j = lax.top_k(cand_s, PEER_TOPK)
    expert = jnp.take_along_axis(cand_i, top_j, axis=-1)
    gates = jax.nn.softmax(top_s, axis=-1)
    nc = n // PEER_CHUNK

    def chunk(args):
        xc, ec, gc = args
        a = jnp.einsum('chkd,cd->chk', u[ec], xc).astype(jnp.float32)
        w = (gc * jax.nn.gelu(a)).astype(xc.dtype)
        return jnp.einsum('chk,chkd->cd', w, v[ec])

    y = lax.map(chunk, (x.reshape(nc, PEER_CHUNK, D),
                        expert.reshape(nc, PEER_CHUNK, PEER_HEADS, PEER_TOPK),
                        gates.reshape(nc, PEER_CHUNK, PEER_HEADS, PEER_TOPK)))
    return y.reshape(B, T, D)


def channel_sublayer(x, shift, scale, gate, lp):
    h = rmsnorm(x, lp['norm2_g']) * (1.0 + scale) + shift
    return x + gate * peer_ffn(h, lp['peer_wq'], lp['peer_keys'], lp['peer_u'], lp['peer_v'])


def context_layer(x, c_ctx, lp, lam_init):
    sh1, sc1, g1, sh2, sc2, g2 = adaln(c_ctx, lp['w_ada'], lp['b_ada'])
    h = rmsnorm(x, lp['norm1_g']) * (1.0 + sc1) + sh1
    qa, ka, va, qb, kb, vb, qc, kc, vc = project_heads(h, lp['w_in'])
    qb = rmsnorm(qb, lp['ax_q_g'])
    kb = rmsnorm(kb, lp['ax_k_g'])
    oa = sink_attend(qa, ka, va, lp['win_sink'])
    ob = gqa_attend(qb, kb, vb)
    oc = diff_attend(qc, kc, vc, diff_lambda(lp, lam_init), lam_init, lp['diff_subln_g'])
    x = x + g1 * (jnp.concatenate([oa, ob, oc], axis=-1) @ lp['w_out'])
    x = channel_sublayer(x, sh2, sc2, g2, lp)
    return x, (ka, va, kb, vb, kc, vc)


def latent_layer(x, c, lp, lam_init, ctx_kv, row_pos, col_pos):
    kca, vca, kcb, vcb, kcc, vcc = ctx_kv
    sh1, sc1, g1, sh2, sc2, g2 = adaln(c, lp['w_ada'], lp['b_ada'])
    h = rmsnorm(x, lp['norm1_g']) * (1.0 + sc1) + sh1
    qa, ka, va, qb, kb, vb, qc, kc, vc = project_heads(h, lp['w_in'])
    qa = axial_rope(qa, row_pos, col_pos)
    ka = axial_rope(ka, row_pos, col_pos)
    qb = axial_rope(rmsnorm(qb, lp['ax_q_g']), row_pos, col_pos)
    kb = axial_rope(rmsnorm(kb, lp['ax_k_g']), row_pos, col_pos)
    qc = rope_halves(qc, row_pos, col_pos)
    kc = rope_halves(kc, row_pos, col_pos)
    oa = window_attend_latent(qa, ka, va, kca, vca, lp['win_sink'])
    kb_all = jnp.concatenate([kb, kcb], axis=1)
    vb_all = jnp.concatenate([vb, vcb], axis=1)
    ob = sweep_query_blocks(lambda qq: gqa_attend(qq, kb_all, vb_all), qb)
    kc_all = jnp.concatenate([kc, kcc], axis=1)
    vc_all = jnp.concatenate([vc, vcc], axis=1)
    lam = diff_lambda(lp, lam_init)
    oc = sweep_query_blocks(lambda qq: diff_attend(qq, kc_all, vc_all, lam, lam_init, lp['diff_subln_g']), qc)
    x = x + g1 * (jnp.concatenate([oa, ob, oc], axis=-1) @ lp['w_out'])
    return channel_sublayer(x, sh2, sc2, g2, lp)


def setup_inputs(seed: int = 0) -> dict:
    key = jax.random.key(seed)
    ks = jax.random.split(key, 32)

    def nrm(k, shape, scale=1.0):
        return jax.random.normal(k, shape, jnp.float32) * scale

    sd = D_MODEL ** -0.5
    return {
        'x_prompt': nrm(ks[0], (BATCH, SEQ, D_MODEL)),
        'x_sample': nrm(ks[1], (DEC_BATCH, DEC_SEQ, D_MODEL)),
        'cache_k_win': nrm(ks[2], (DEC_BATCH, DEPTH, PAST_LEN, WIN_KV, HEAD_DIM)),
        'cache_v_win': nrm(ks[3], (DEC_BATCH, DEPTH, PAST_LEN, WIN_KV, HEAD_DIM)),
        'cache_k_axial': nrm(ks[4], (DEC_BATCH, DEPTH, PAST_LEN, AX_KV, HEAD_DIM)),
        'cache_v_axial': nrm(ks[5], (DEC_BATCH, DEPTH, PAST_LEN, AX_KV, HEAD_DIM)),
        'cache_k_diff': nrm(ks[6], (DEC_BATCH, DEPTH, PAST_LEN, DIFF_HEADS, HEAD_DIM)),
        'cache_v_diff': nrm(ks[7], (DEC_BATCH, DEPTH, PAST_LEN, DIFF_HEADS, HEAD_DIM)),
        'c': nrm(ks[8], (DEC_BATCH, D_MODEL)),
        'c_ctx': nrm(ks[9], (D_MODEL,)),
        'norm1_g': 1.0 + nrm(ks[10], (DEPTH, D_MODEL), 0.01),
        'w_ada': nrm(ks[11], (DEPTH, D_MODEL, 6 * D_MODEL), sd),
        'b_ada': nrm(ks[12], (DEPTH, 6 * D_MODEL), 0.01),
        'w_in': nrm(ks[13], (DEPTH, D_MODEL, IN_COLS), sd),
        'win_sink': nrm(ks[14], (DEPTH, WIN_HEADS)),
        'ax_q_g': 1.0 + nrm(ks[15], (DEPTH, HEAD_DIM), 0.01),
        'ax_k_g': 1.0 + nrm(ks[16], (DEPTH, HEAD_DIM), 0.01),
        'lam_q1': nrm(ks[17], (DEPTH, DIFF_HALF), 0.1),
        'lam_k1': nrm(ks[18], (DEPTH, DIFF_HALF), 0.1),
        'lam_q2': nrm(ks[19], (DEPTH, DIFF_HALF), 0.1),
        'lam_k2': nrm(ks[20], (DEPTH, DIFF_HALF), 0.1),
        'diff_subln_g': 1.0 + nrm(ks[21], (DEPTH, DIFF_HEADS, HEAD_DIM), 0.01),
        'w_out': nrm(ks[22], (DEPTH, D_MODEL, D_MODEL), sd),
        'norm2_g': 1.0 + nrm(ks[23], (DEPTH, D_MODEL), 0.01),
        'peer_wq': nrm(ks[24], (DEPTH, D_MODEL, PEER_HEADS * PEER_QDIM), sd),
        'peer_keys': nrm(ks[25], (DEPTH, PEER_HEADS, 2, PEER_KEYS, PEER_QDIM // 2), (PEER_QDIM // 2) ** -0.5),
        'peer_u': nrm(ks[26], (DEPTH, PEER_EXPERTS, D_MODEL), sd),
        'peer_v': nrm(ks[27], (DEPTH, PEER_EXPERTS, D_MODEL), PEER_HEADS ** -0.5),
        'final_g': 1.0 + nrm(ks[28], (D_MODEL,), 0.01),
    }


def reference(x_prompt, x_sample, cache_k_win, cache_v_win, cache_k_axial, cache_v_axial,
              cache_k_diff, cache_v_diff, c, c_ctx, norm1_g, w_ada, b_ada, w_in, win_sink,
              ax_q_g, ax_k_g, lam_q1, lam_k1, lam_q2, lam_k2, diff_subln_g, w_out, norm2_g,
              peer_wq, peer_keys, peer_u, peer_v, final_g):
    def layer_params(l):
        return {'norm1_g': norm1_g[l], 'w_ada': w_ada[l], 'b_ada': b_ada[l], 'w_in': w_in[l],
                'win_sink': win_sink[l], 'ax_q_g': ax_q_g[l], 'ax_k_g': ax_k_g[l],
                'lam_q1': lam_q1[l], 'lam_k1': lam_k1[l], 'lam_q2': lam_q2[l], 'lam_k2': lam_k2[l],
                'diff_subln_g': diff_subln_g[l], 'w_out': w_out[l], 'norm2_g': norm2_g[l],
                'peer_wq': peer_wq[l], 'peer_keys': peer_keys[l], 'peer_u': peer_u[l], 'peer_v': peer_v[l]}

    n_tok = x_sample.shape[1]
    n_rows = n_tok // GRID_W
    row_pos = jnp.repeat(jnp.arange(n_rows, dtype=jnp.int32), GRID_W)
    col_pos = jnp.tile(jnp.arange(GRID_W, dtype=jnp.int32), n_rows)

    xp = x_prompt
    kw, vw, kx, vx, kd, vd = [], [], [], [], [], []
    for l in range(DEPTH):
        lam_init = 0.8 - 0.6 * math.exp(-0.3 * l)
        xp, (ka, va, kb, vb, kc, vc) = context_layer(xp, c_ctx, layer_params(l), lam_init)
        kw.append(ka); vw.append(va); kx.append(kb); vx.append(vb); kd.append(kc); vd.append(vc)
    y_prompt = rmsnorm(xp, final_g)
    new_k_win = jnp.stack(kw, axis=1)
    new_v_win = jnp.stack(vw, axis=1)
    new_k_axial = jnp.stack(kx, axis=1)
    new_v_axial = jnp.stack(vx, axis=1)
    new_k_diff = jnp.stack(kd, axis=1)
    new_v_diff = jnp.stack(vd, axis=1)

    xs = x_sample
    for l in range(DEPTH):
        lam_init = 0.8 - 0.6 * math.exp(-0.3 * l)
        ctx_kv = (cache_k_win[:, l], cache_v_win[:, l], cache_k_axial[:, l], cache_v_axial[:, l],
                  cache_k_diff[:, l], cache_v_diff[:, l])
        xs = latent_layer(xs, c, layer_params(l), lam_init, ctx_kv, row_pos, col_pos)
    y_sample = rmsnorm(xs, final_g)

    return (y_prompt, y_sample, new_k_win, new_v_win, new_k_axial, new_v_axial, new_k_diff, new_v_diff)
```

```python
import functools
import math

import jax
import jax.numpy as jnp
from jax import lax
from jax.experimental import pallas as pl
from jax.experimental.pallas import tpu as pltpu

F32 = jnp.float32
BF16 = jnp.bfloat16

HEAD_DIM = 128
GRID_W = 64
WINDOW = 128
ROPE_THETA = 10000.0
PEER_HEADS = 8
PEER_KEYS = 128
PEER_TOPK = 16
EPS = 1e-6
NEG_INF = -1e30

QA, KA, VA, QB, KB, VB, QC, KC, VC = 0, 512, 768, 1024, 2048, 2304, 2560, 3072, 3584
IN_COLS = 4096
WIN_HEADS, WIN_KV, AX_HEADS, AX_KV, DIFF_HEADS = 4, 2, 8, 2, 4

VMEM_LIMIT = 56 * 1024 * 1024


def _cparams(sem):
    return pltpu.CompilerParams(dimension_semantics=sem, vmem_limit_bytes=VMEM_LIMIT)


def _dot_nt(a, b):
    return lax.dot_general(a, b, (((1,), (1,)), ((), ())), preferred_element_type=F32)


def _dot(a, b):
    return jnp.dot(a, b, preferred_element_type=F32)


def _ada_kernel(cond_ref, w_ref, b_ref, o_ref):
    c = cond_ref[...]
    s = (c * jax.nn.sigmoid(c)).astype(BF16)
    o_ref[...] = _dot(s, w_ref[...].astype(BF16)) + b_ref[...]


def _ada_call(cond, w_ada, b_ada):
    depth, d, n6 = w_ada.shape
    tn = 1024
    return pl.pallas_call(
        _ada_kernel,
        out_shape=jax.ShapeDtypeStruct((depth, 8, n6), F32),
        grid=(depth, n6 // tn),
        in_specs=[
            pl.BlockSpec((8, d), lambda l, j: (0, 0)),
            pl.BlockSpec((None, d, tn), lambda l, j: (l, 0, j)),
            pl.BlockSpec((None, 1, tn), lambda l, j: (l, 0, j)),
        ],
        out_specs=pl.BlockSpec((None, 8, tn), lambda l, j: (l, 0, j)),
        compiler_params=_cparams(("parallel", "parallel")),
        name="ada",
    )(cond, w_ada, b_ada.reshape(depth, 1, n6))


def _kw_kernel(keys_ref, wq_ref, o_ref):
    kw = lax.dot_general(keys_ref[...], wq_ref[...], (((1,), (1,)), ((), ())),
                         preferred_element_type=F32, precision=lax.Precision.HIGHEST)
    o_ref[...] = kw.astype(BF16)


def _kw_call(peer_keys, peer_wq):
    depth, d, nq = peer_wq.shape
    ngrp = PEER_HEADS * 2
    sub = PEER_KEYS
    return pl.pallas_call(
        _kw_kernel,
        out_shape=jax.ShapeDtypeStruct((depth, ngrp * sub, d), BF16),
        grid=(depth, ngrp),
        in_specs=[
            pl.BlockSpec((None, None, None, sub, sub), lambda l, g: (l, g // 2, g % 2, 0, 0)),
            pl.BlockSpec((None, d, sub), lambda l, g: (l, 0, g)),
        ],
        out_specs=pl.BlockSpec((None, sub, d), lambda l, g: (l, g, 0)),
        compiler_params=_cparams(("parallel", "parallel")),
        name="peer_kw",
    )(peer_keys, peer_wq)


def _rope_tables(n_tok):
    n_rows = n_tok // GRID_W
    row_pos = jnp.repeat(jnp.arange(n_rows, dtype=jnp.int32), GRID_W).astype(F32)
    col_pos = jnp.tile(jnp.arange(GRID_W, dtype=jnp.int32), n_rows).astype(F32)
    lane = jnp.arange(HEAD_DIM)

    def table(pair, use_row):
        freqs = ROPE_THETA ** (-jnp.arange(pair, dtype=F32) / pair)
        f = freqs[lane % pair]
        pos = jnp.where(use_row[None, :], row_pos[:, None], col_pos[:, None])
        ang = pos * f[None, :]
        first = (lane % (2 * pair)) < pair
        return jnp.cos(ang), jnp.where(first[None, :], -jnp.sin(ang), jnp.sin(ang))

    c_ax, s_ax = table(32, lane < 64)
    c_df, s_df = table(16, (lane % 64) < 32)
    return c_ax, s_ax, c_df, s_df


def _rope(y, c, s, pair):
    lane = lax.broadcasted_iota(jnp.int32, y.shape, 1)
    first = (lane % (2 * pair)) < pair
    partner = jnp.where(first, pltpu.roll(y, HEAD_DIM - pair, axis=1), pltpu.roll(y, pair, axis=1))
    return y * c + partner * s


def _head_norm(y, g):
    return y * lax.rsqrt(jnp.mean(y * y, axis=-1, keepdims=True) + EPS) * g


def _in_kernel(*refs, latent):
    if latent:
        (x_ref, sh_ref, sc_ref, g_ref, w_ref, axq_ref, axk_ref, ca_ref, sa_ref, cd_ref, sd_ref,
         _z_in, z_ref, h_ref) = refs
        ka_ref = va_ref = kb_ref = vb_ref = kc_ref = vc_ref = None
    else:
        (x_ref, sh_ref, sc_ref, g_ref, w_ref, axq_ref, axk_ref,
         z_ref, ka_ref, va_ref, kb_ref, vb_ref, kc_ref, vc_ref, h_ref) = refs
    j = pl.program_id(1)

    @pl.when(j == 0)
    def _():
        x = x_ref[...]
        y = x * lax.rsqrt(jnp.mean(x * x, axis=-1, keepdims=True) + EPS) * g_ref[...]
        h_ref[...] = (y * (1.0 + sc_ref[...]) + sh_ref[...]).astype(BF16)

    def rope_ax(y):
        return _rope(y, ca_ref[...], sa_ref[...], 32) if latent else y

    def rope_df(y):
        return _rope(y, cd_ref[...], sd_ref[...], 16) if latent else y

    def head(z, k):
        return z[:, k * HEAD_DIM:(k + 1) * HEAD_DIM]

    def put(k, y):
        z_ref[:, k * HEAD_DIM:(k + 1) * HEAD_DIM] = y.astype(BF16)

    def keep(ref, k, y):
        if not latent:
            ref[:, k * HEAD_DIM:(k + 1) * HEAD_DIM] = y

    @pl.when(j == 0)
    def _():
        z = _dot(h_ref[...], w_ref[...])
        for k in range(4):
            put(k, rope_ax(head(z, k)))
        for k in range(2):
            y = head(z, 4 + k)
            keep(ka_ref, k, y)
            put(4 + k, rope_ax(y))
        for k in range(2):
            y = head(z, 6 + k)
            keep(va_ref, k, y)
            put(6 + k, y)

    @pl.when(j == 1)
    def _():
        z = _dot(h_ref[...], w_ref[...])
        for k in range(8):
            put(k, rope_ax(_head_norm(head(z, k), axq_ref[...])))

    @pl.when(j == 2)
    def _():
        z = _dot(h_ref[...], w_ref[...])
        for k in range(2):
            y = _head_norm(head(z, k), axk_ref[...])
            keep(kb_ref, k, y)
            put(k, rope_ax(y))
        for k in range(2):
            y = head(z, 2 + k)
            keep(vb_ref, k, y)
            put(2 + k, y)
        for k in range(4):
            put(4 + k, rope_df(head(z, 4 + k)))

    @pl.when(j == 3)
    def _():
        z = _dot(h_ref[...], w_ref[...])
        for k in range(4):
            y = head(z, k)
            keep(kc_ref, k, y)
            put(k, rope_df(y))
        for k in range(4):
            y = head(z, 4 + k)
            keep(vc_ref, k, y)
            put(4 + k, y)


def _in_call(l, x, mods, norm1_g, w_in, ax_q_g, ax_k_g, tables, z, *, latent, n_ctx, lat_len, tm):
    n, d = x.shape
    depth = w_in.shape[0]
    nb_ctx = n_ctx // tm
    bpl = lat_len // tm
    if latent:
        nblk = (n - n_ctx) // tm
        row = lambda i: i + nb_ctx
        grp = lambda i: 1 + i // bpl
    else:
        nblk = nb_ctx
        row = lambda i: i
        grp = lambda i: 0
    cb = 1024
    in_specs = [
        pl.BlockSpec((tm, d), lambda i, j: (row(i), 0)),
        pl.BlockSpec((None, None, None, 1, d), lambda i, j: (l, grp(i), 0, 0, 0)),
        pl.BlockSpec((None, None, None, 1, d), lambda i, j: (l, grp(i), 1, 0, 0)),
        pl.BlockSpec((None, 1, d), lambda i, j: (l, 0, 0)),
        pl.BlockSpec((None, d, cb), lambda i, j: (l, 0, j)),
        pl.BlockSpec((None, 1, HEAD_DIM), lambda i, j: (l, 0, 0)),
        pl.BlockSpec((None, 1, HEAD_DIM), lambda i, j: (l, 0, 0)),
    ]
    args = [x, mods, mods, norm1_g.reshape(depth, 1, d), w_in,
            ax_q_g.reshape(depth, 1, HEAD_DIM), ax_k_g.reshape(depth, 1, HEAD_DIM)]
    z_spec = pl.BlockSpec((tm, cb), lambda i, j: (row(i), j))
    z_shape = jax.ShapeDtypeStruct((n, IN_COLS), BF16)
    scratch = [pltpu.VMEM((tm, d), BF16)]
    if latent:
        tspec = pl.BlockSpec((tm, HEAD_DIM), lambda i, j: (i % bpl, 0))
        in_specs += [tspec] * 4 + [pl.BlockSpec(memory_space=pl.ANY)]
        args += list(tables) + [z]
        return pl.pallas_call(
            functools.partial(_in_kernel, latent=True),
            out_shape=z_shape, grid=(nblk, 4), in_specs=in_specs, out_specs=z_spec,
            scratch_shapes=scratch, input_output_aliases={len(args) - 1: 0},
            compiler_params=_cparams(("parallel", "arbitrary")), name="in_proj_latent",
        )(*args)
    kv2 = jax.ShapeDtypeStruct((n_ctx, 2 * HEAD_DIM), F32)
    kv4 = jax.ShapeDtypeStruct((n_ctx, 4 * HEAD_DIM), F32)
    s2 = pl.BlockSpec((tm, 2 * HEAD_DIM), lambda i, j: (i, 0))
    s4 = pl.BlockSpec((tm, 4 * HEAD_DIM), lambda i, j: (i, 0))
    return pl.pallas_call(
        functools.partial(_in_kernel, latent=False),
        out_shape=(z_shape, kv2, kv2, kv2, kv2, kv4, kv4), grid=(nblk, 4), in_specs=in_specs,
        out_specs=(z_spec, s2, s2, s2, s2, s4, s4), scratch_shapes=scratch,
        compiler_params=_cparams(("parallel", "arbitrary")), name="in_proj_context",
    )(*args)


def _gqa_kernel(*refs, groups, tq, window, has_cache, has_sink, l, lat_len):
    refs = list(refs)
    sink_ref = refs.pop(0) if has_sink else None
    q_ref, k_ref, v_ref = refs[:3]
    refs = refs[3:]
    if has_cache:
        ck_ref, cv_ref = refs[:2]
        refs = refs[2:]
    o_ref = refs[-1]
    kv = pl.program_id(1)
    qi = pl.program_id(2)
    scale = HEAD_DIM ** -0.5

    if window:
        span = tq + 2 * WINDOW
        start = jnp.clip(qi * tq - WINDOW, 0, lat_len - span)
        start = pl.multiple_of(start, WINDOW)
        k = k_ref[pl.ds(start, span), :]
        v = v_ref[pl.ds(start, span), :]
        qpos = qi * tq + lax.broadcasted_iota(jnp.int32, (tq, span), 0)
        kpos = start + lax.broadcasted_iota(jnp.int32, (tq, span), 1)
        ok = jnp.abs(qpos - kpos) <= WINDOW
    else:
        k = k_ref[...]
        v = v_ref[...]
    if has_cache:
        ck = ck_ref[...].astype(BF16)
        cv = cv_ref[...].astype(BF16)

    for g in range(groups):
        q = q_ref[:, g * HEAD_DIM:(g + 1) * HEAD_DIM]
        s = _dot_nt(q, k) * scale
        if window:
            s = jnp.where(ok, s, NEG_INF)
        m = jnp.max(s, axis=-1, keepdims=True)
        if has_cache:
            s2 = _dot_nt(q, ck) * scale
            m = jnp.maximum(m, jnp.max(s2, axis=-1, keepdims=True))
        if has_sink:
            sink = sink_ref[l, kv * groups + g]
            m = jnp.maximum(m, sink)
        e = jnp.exp(s - m)
        den = jnp.sum(e, axis=-1, keepdims=True)
        o = _dot(e.astype(BF16), v)
        if has_cache:
            e2 = jnp.exp(s2 - m)
            den = den + jnp.sum(e2, axis=-1, keepdims=True)
            o = o + _dot(e2.astype(BF16), cv)
        if has_sink:
            den = den + jnp.exp(sink - m)
        o_ref[:, g * HEAD_DIM:(g + 1) * HEAD_DIM] = (o / den).astype(BF16)


def _gqa_call(l, z, o, *, qcol, kcol, vcol, ocol, heads, kvh, nbatch, seq, row0, tq,
              window=False, cache=None, sink=None):
    groups = heads // kvh
    gw = groups * HEAD_DIM
    nq = seq // tq
    qrow = lambda b, qi: row0 // tq + b * nq + qi
    krow = lambda b: row0 // seq + b
    in_specs, args = [], []
    if sink is not None:
        in_specs.append(pl.BlockSpec(memory_space=pltpu.SMEM))
        args.append(sink)
    in_specs += [
        pl.BlockSpec((tq, gw), lambda b, kv, qi: (qrow(b, qi), qcol // gw + kv)),
        pl.BlockSpec((seq, HEAD_DIM), lambda b, kv, qi: (krow(b), kcol // HEAD_DIM + kv)),
        pl.BlockSpec((seq, HEAD_DIM), lambda b, kv, qi: (krow(b), vcol // HEAD_DIM + kv)),
    ]
    args += [z, z, z]
    if cache is not None:
        past = cache[0].shape[2]
        cspec = pl.BlockSpec((None, None, past, HEAD_DIM), lambda b, kv, qi: (b, l, 0, kv))
        in_specs += [cspec, cspec]
        args += list(cache)
    in_specs.append(pl.BlockSpec(memory_space=pl.ANY))
    args.append(o)
    kern = functools.partial(_gqa_kernel, groups=groups, tq=tq, window=window,
                             has_cache=cache is not None, has_sink=sink is not None, l=l, lat_len=seq)
    return pl.pallas_call(
        kern, out_shape=jax.ShapeDtypeStruct(o.shape, o.dtype), grid=(nbatch, kvh, nq),
        in_specs=in_specs,
        out_specs=pl.BlockSpec((tq, gw), lambda b, kv, qi: (qrow(b, qi), ocol // gw + kv)),
        input_output_aliases={len(args) - 1: 0},
        compiler_params=_cparams(("parallel", "parallel", "arbitrary")),
        name="attn_window" if sink is not None else "attn_axial",
    )(*args)


def _diff_kernel(*refs, has_cache, lam_init):
    refs = list(refs)
    lq1, lk1, lq2, lk2, gain_ref, q_ref, k_ref, v_ref = refs[:8]
    refs = refs[8:]
    if has_cache:
        ck_ref, cv_ref = refs[:2]
    o_ref = refs[-1]
    h = pl.program_id(1)
    half = HEAD_DIM // 2
    scale = half ** -0.5

    lam = (jnp.exp(jnp.sum(lq1[...] * lk1[...], axis=-1, keepdims=True))
           - jnp.exp(jnp.sum(lq2[...] * lk2[...], axis=-1, keepdims=True)) + lam_init)

    q = q_ref[...]
    k = k_ref[...]
    v = v_ref[...]
    lane = lax.broadcasted_iota(jnp.int32, (1, HEAD_DIM), 1)
    lo = jnp.where(lane < half, 1.0, 0.0).astype(BF16)
    qs = (q * lo, q * (1.0 - lo))
    if has_cache:
        ck = ck_ref[...].astype(BF16)
        cv = cv_ref[...].astype(BF16)

    probs = []
    for c in range(2):
        s = _dot_nt(qs[c], k) * scale
        m = jnp.max(s, axis=-1, keepdims=True)
        if has_cache:
            s2 = _dot_nt(qs[c], ck) * scale
            m = jnp.maximum(m, jnp.max(s2, axis=-1, keepdims=True))
        e = jnp.exp(s - m)
        den = jnp.sum(e, axis=-1, keepdims=True)
        if has_cache:
            e2 = jnp.exp(s2 - m)
            den = den + jnp.sum(e2, axis=-1, keepdims=True)
        else:
            e2 = None
        probs.append((e, e2, 1.0 / den))
    (e_a, e2_a, r_a), (e_b, e2_b, r_b) = probs
    r_b = lam * r_b
    o = _dot((e_a * r_a - e_b * r_b).astype(BF16), v)
    if has_cache:
        o = o + _dot((e2_a * r_a - e2_b * r_b).astype(BF16), cv)
    gain = gain_ref[pl.ds(h, 1), :]
    o = o * lax.rsqrt(jnp.mean(o * o, axis=-1, keepdims=True) + EPS) * gain
    o_ref[...] = (o * (1.0 - lam_init)).astype(BF16)


def _diff_call(l, z, o, lam_params, gain, *, nbatch, seq, row0, tq, lam_init, cache=None):
    nq = seq // tq
    depth = gain.shape[0]
    half = HEAD_DIM // 2
    qrow = lambda b, qi: row0 // tq + b * nq + qi
    krow = lambda b: row0 // seq + b
    lspec = pl.BlockSpec((None, 1, half), lambda b, h, qi: (l, 0, 0))
    in_specs = [lspec] * 4 + [
        pl.BlockSpec((None, DIFF_HEADS, HEAD_DIM), lambda b, h, qi: (l, 0, 0)),
        pl.BlockSpec((tq, HEAD_DIM), lambda b, h, qi: (qrow(b, qi), QC // HEAD_DIM + h)),
        pl.BlockSpec((seq, HEAD_DIM), lambda b, h, qi: (krow(b), KC // HEAD_DIM + h)),
        pl.BlockSpec((seq, HEAD_DIM), lambda b, h, qi: (krow(b), VC // HEAD_DIM + h)),
    ]
    args = [p.reshape(depth, 1, half) for p in lam_params] + [gain, z, z, z]
    if cache is not None:
        past = cache[0].shape[2]
        cspec = pl.BlockSpec((None, None, past, HEAD_DIM), lambda b, h, qi: (b, l, 0, h))
        in_specs += [cspec, cspec]
        args += list(cache)
    in_specs.append(pl.BlockSpec(memory_space=pl.ANY))
    args.append(o)
    ocol0 = (WIN_HEADS + AX_HEADS)
    return pl.pallas_call(
        functools.partial(_diff_kernel, has_cache=cache is not None, lam_init=lam_init),
        out_shape=jax.ShapeDtypeStruct(o.shape, o.dtype), grid=(nbatch, DIFF_HEADS, nq),
        in_specs=in_specs,
        out_specs=pl.BlockSpec((tq, HEAD_DIM), lambda b, h, qi: (qrow(b, qi), ocol0 + h)),
        input_output_aliases={len(args) - 1: 0},
        compiler_params=_cparams(("parallel", "parallel", "arbitrary")),
        name="attn_diff",
    )(*args)


def _out_kernel(o_ref, x_ref, w_ref, g1_ref, sh2_ref, sc2_ref, n2g_ref, x1_ref, h2_ref):
    a = _dot(o_ref[...], w_ref[...])
    x1 = x_ref[...] + g1_ref[...] * a
    x1_ref[...] = x1
    y = x1 * lax.rsqrt(jnp.mean(x1 * x1, axis=-1, keepdims=True) + EPS) * n2g_ref[...]
    h2_ref[...] = (y * (1.0 + sc2_ref[...]) + sh2_ref[...]).astype(BF16)


def _group_of_block(i, nb_ctx, bpl):
    return jnp.where(i < nb_ctx, 0, 1 + (i - nb_ctx) // bpl)


def _out_call(l, o, x, w_out, mods, norm2_g, *, n_ctx, lat_len, tm):
    n, d = x.shape
    depth = w_out.shape[0]
    nb_ctx, bpl = n_ctx // tm, lat_len // tm
    grp = lambda i: _group_of_block(i, nb_ctx, bpl)
    mspec = lambda which: pl.BlockSpec((None, None, None, 1, d), lambda i: (l, grp(i), which, 0, 0))
    return pl.pallas_call(
        _out_kernel,
        out_shape=(jax.ShapeDtypeStruct((n, d), F32), jax.ShapeDtypeStruct((n, d), BF16)),
        grid=(n // tm,),
        in_specs=[
            pl.BlockSpec((tm, d), lambda i: (i, 0)),
            pl.BlockSpec((tm, d), lambda i: (i, 0)),
            pl.BlockSpec((None, d, d), lambda i: (l, 0, 0)),
            mspec(2), mspec(3), mspec(4),
            pl.BlockSpec((None, 1, d), lambda i: (l, 0, 0)),
        ],
        out_specs=(pl.BlockSpec((tm, d), lambda i: (i, 0)), pl.BlockSpec((tm, d), lambda i: (i, 0))),
        compiler_params=_cparams(("parallel",)),
        name="out_proj",
    )(o, x, w_out, mods, mods, mods, norm2_g.reshape(depth, 1, d))


def _score_kernel(kw_ref, h2_ref, o_ref):
    o_ref[...] = _dot_nt(kw_ref[...], h2_ref[...])


def _score_call(l, kw, h2, *, tm):
    n, d = h2.shape
    rows = kw.shape[1]
    return pl.pallas_call(
        _score_kernel,
        out_shape=jax.ShapeDtypeStruct((rows, n), F32),
        grid=(n // tm,),
        in_specs=[pl.BlockSpec((None, rows, d), lambda i: (l, 0, 0)),
                  pl.BlockSpec((tm, d), lambda i: (i, 0))],
        out_specs=pl.BlockSpec((rows, tm), lambda i: (0, i)),
        compiler_params=_cparams(("parallel",)),
        name="peer_scores",
    )(kw, h2)


def _sort_pairs(n):
    pairs = []

    def merge(lo, hi, r):
        step = r * 2
        if step < hi - lo:
            merge(lo, hi, step)
            merge(lo + r, hi, step)
            for i in range(lo + r, hi - r, step):
                pairs.append((i, i + r))
        else:
            pairs.append((lo, lo + r))

    def sort(lo, hi):
        if hi - lo >= 1:
            mid = lo + (hi - lo) // 2
            sort(lo, mid)
            sort(mid + 1, hi)
            merge(lo, hi, 1)

    sort(0, n - 1)
    return pairs


_SORT16 = _sort_pairs(PEER_TOPK)


def _vmax(a, b):
    if a is None:
        return b
    if b is None:
        return a
    return jnp.maximum(a, b)


def _vmin(a, b):
    if a is None or b is None:
        return None
    return jnp.minimum(a, b)


def _bitonic_merge_desc(x):
    x = list(x)
    d = PEER_TOPK // 2
    while d >= 1:
        for i in range(PEER_TOPK):
            if (i & d) == 0:
                a, b = x[i], x[i + d]
                x[i], x[i + d] = _vmax(a, b), _vmin(a, b)
        d //= 2
    return x


def _merge_top16(a, b):
    a = list(a) + [None] * (PEER_TOPK - len(a))
    b = list(b) + [None] * (PEER_TOPK - len(b))
    return _bitonic_merge_desc([_vmax(a[k], b[PEER_TOPK - 1 - k]) for k in range(PEER_TOPK)])


def _router_kernel(s_ref, r1_ref, e1_ref, cnt_ref, e0_ref, top_ref):
    nk = PEER_KEYS
    nv = nk // 8
    tk = PEER_TOPK
    sub = lax.broadcasted_iota(jnp.int32, (8, s_ref.shape[1]), 0)

    for g in range(2 * PEER_HEADS):
        x = [s_ref[g * nk + 8 * r:g * nk + 8 * r + 8, :] for r in range(nv)]
        for (i, j) in _SORT16:
            x[i], x[j] = jnp.maximum(x[i], x[j]), jnp.minimum(x[i], x[j])
        for shift in (4, 2, 1):
            y = [pltpu.roll(v, shift, axis=0) for v in x]
            x = _bitonic_merge_desc([jnp.maximum(x[k], y[tk - 1 - k]) for k in range(tk)])
        for a in range(tk):
            top_ref[g * tk + a] = x[a]

    def packed(c, a):
        out = top_ref[c * tk + a]
        for h in range(1, PEER_HEADS):
            out = jnp.where(sub == h, top_ref[(2 * h + c) * tk + a], out)
        return out

    p0 = [packed(0, a) for a in range(tk)]
    p1 = [packed(1, b) for b in range(tk)]
    rows = [[p0[a] + p1[b] for b in range(tk // (a + 1))] for a in range(8)]
    col0 = [p0[a] + p1[0] for a in range(8, tk)]
    t1 = _merge_top16(rows[0], _merge_top16(rows[1], col0))
    t2 = _merge_top16(_merge_top16(rows[2], rows[3]), _merge_top16(rows[4], rows[5]))
    t3 = _merge_top16(rows[6], rows[7])
    best = _merge_top16(t1, _merge_top16(t2, t3))
    tau_p = best[tk - 1]
    zsum = jnp.ones_like(tau_p)
    for k in range(1, tk):
        zsum = zsum + jnp.exp(best[k] - best[0])
    zinv_p = 1.0 / zsum

    for h in range(PEER_HEADS):
        shape = (8, s_ref.shape[1])
        tau = jnp.broadcast_to(tau_p[h:h + 1, :], shape)
        zinv = jnp.broadcast_to(zinv_p[h:h + 1, :], shape)
        t0 = [top_ref[(2 * h) * tk + a] for a in range(tk)]
        t1h = [top_ref[(2 * h + 1) * tk + b] for b in range(tk)]
        for r in range(nv):
            s0 = s_ref[(2 * h) * nk + 8 * r:(2 * h) * nk + 8 * r + 8, :]
            s1 = s_ref[(2 * h + 1) * nk + 8 * r:(2 * h + 1) * nk + 8 * r + 8, :]
            rank = jnp.full(shape, float(tk), F32)
            for b in range(tk - 1, -1, -1):
                rank = jnp.where(s1 >= t1h[b], float(b), rank)
            cnt = jnp.zeros(shape, F32)
            for b in range(tk):
                cnt = cnt + jnp.where(s0 + t1h[b] >= tau, 1.0, 0.0)
            r1_ref[h, 8 * r:8 * r + 8, :] = rank
            e1_ref[h, 8 * r:8 * r + 8, :] = jnp.exp(s1 - t1h[0])
            cnt_ref[h, 8 * r:8 * r + 8, :] = cnt
            e0_ref[h, 8 * r:8 * r + 8, :] = jnp.exp(s0 - t0[0]) * zinv


def _router_call(s_t):
    rows, n = s_t.shape
    tl = 128
    shp = jax.ShapeDtypeStruct((PEER_HEADS, PEER_KEYS, n), F32)
    spec = pl.BlockSpec((PEER_HEADS, PEER_KEYS, tl), lambda i: (0, 0, i))
    return pl.pallas_call(
        _router_kernel,
        out_shape=(shp, shp, shp, shp),
        grid=(n // tl,),
        in_specs=[pl.BlockSpec((rows, tl), lambda i: (0, i))],
        out_specs=(spec, spec, spec, spec),
        scratch_shapes=[pltpu.VMEM((2 * PEER_HEADS * PEER_TOPK, 8, tl), F32)],
        compiler_params=_cparams(("parallel",)),
        name="peer_router",
    )(s_t)


def _gelu(a):
    return 0.5 * a * (1.0 + jnp.tanh(math.sqrt(2.0 / math.pi) * (a + 0.044715 * (a * a * a))))


def _peer_kernel(*refs, final, ib):
    if final:
        (h2_ref, u_ref, vt_ref, r1_ref, e1_ref, cnt_ref, e0_ref, x1_ref, g2_ref, fg_ref,
         o_ref, acc_ref, p_ref) = refs
    else:
        (h2_ref, u_ref, vt_ref, r1_ref, e1_ref, cnt_ref, e0_ref, x1_ref, g2_ref,
         o_ref, acc_ref, p_ref) = refs
    e = pl.program_id(1)
    t = h2_ref.shape[0]

    @pl.when(e == 0)
    def _():
        acc_ref[...] = jnp.zeros_like(acc_ref)

    a_t = _dot_nt(u_ref[...], h2_ref[...])
    for ii in range(ib):
        w = [jnp.zeros((16, t), BF16) for _ in range(PEER_KEYS // 16)]
        for h in range(PEER_HEADS):
            cnt = jnp.broadcast_to(cnt_ref[h, ii:ii + 1, :], (16, t)).astype(BF16)
            e0 = jnp.broadcast_to(e0_ref[h, ii:ii + 1, :], (16, t)).astype(BF16)
            for jc in range(PEER_KEYS // 16):
                r1 = r1_ref[h, jc * 16:(jc + 1) * 16, :]
                e1 = e1_ref[h, jc * 16:(jc + 1) * 16, :]
                w[jc] = w[jc] + jnp.where(r1 < cnt, e0 * e1, jnp.zeros_like(e1))
        for jc in range(PEER_KEYS // 16):
            r0 = ii * PEER_KEYS + jc * 16
            g = _gelu(a_t[r0:r0 + 16, :]).astype(BF16)
            p_ref[r0:r0 + 16, :] = w[jc] * g
    acc_ref[...] += _dot(vt_ref[...], p_ref[...])

    @pl.when(e == pl.num_programs(1) - 1)
    def _():
        x2 = x1_ref[...] + g2_ref[...] * acc_ref[...].T
        if final:
            x2 = x2 * lax.rsqrt(jnp.mean(x2 * x2, axis=-1, keepdims=True) + EPS) * fg_ref[...]
        o_ref[...] = x2


def _peer_call(l, h2, u_bf, vt_bf, r1, e1, cnt, e0, x1, mods, final_g, *, n_ctx, lat_len, tm, final):
    n, d = x1.shape
    ib = 8
    eb = ib * PEER_KEYS
    n_exp = u_bf.shape[1]
    nb_ctx, bpl = n_ctx // tm, lat_len // tm
    grp = lambda i: _group_of_block(i, nb_ctx, bpl)
    once = dict(pipeline_mode=pl.Buffered(1))
    in_specs = [
        pl.BlockSpec((tm, d), lambda i, e: (i, 0), **once),
        pl.BlockSpec((None, eb, d), lambda i, e: (l, e, 0)),
        pl.BlockSpec((None, d, eb), lambda i, e: (l, 0, e)),
        pl.BlockSpec((PEER_HEADS, PEER_KEYS, tm), lambda i, e: (0, 0, i), **once),
        pl.BlockSpec((PEER_HEADS, PEER_KEYS, tm), lambda i, e: (0, 0, i), **once),
        pl.BlockSpec((PEER_HEADS, ib, tm), lambda i, e: (0, e, i)),
        pl.BlockSpec((PEER_HEADS, ib, tm), lambda i, e: (0, e, i)),
        pl.BlockSpec((tm, d), lambda i, e: (i, 0), **once),
        pl.BlockSpec((None, None, None, 1, d), lambda i, e: (l, grp(i), 5, 0, 0)),
    ]
    args = [h2, u_bf, vt_bf, r1, e1, cnt, e0, x1, mods]
    if final:
        in_specs.append(pl.BlockSpec((1, d), lambda i, e: (0, 0)))
        args.append(final_g.reshape(1, d))
    return pl.pallas_call(
        functools.partial(_peer_kernel, final=final, ib=ib),
        out_shape=jax.ShapeDtypeStruct((n, d), F32),
        grid=(n // tm, n_exp // eb),
        in_specs=in_specs,
        out_specs=pl.BlockSpec((tm, d), lambda i, e: (i, 0)),
        scratch_shapes=[pltpu.VMEM((d, tm), F32), pltpu.VMEM((eb, tm), BF16)],
        compiler_params=_cparams(("parallel", "arbitrary")),
        name="peer_experts",
    )(*args)


def kernel(x_prompt, x_sample, cache_k_win, cache_v_win, cache_k_axial, cache_v_axial, cache_k_diff,
           cache_v_diff, c, c_ctx, norm1_g, w_ada, b_ada, w_in, win_sink, ax_q_g, ax_k_g, lam_q1, lam_k1,
           lam_q2, lam_k2, diff_subln_g, w_out, norm2_g, peer_wq, peer_keys, peer_u, peer_v, final_g):
    batch, seq, d = x_prompt.shape
    dec_batch, lat_len, _ = x_sample.shape
    depth = w_in.shape[0]
    past = cache_k_win.shape[2]
    n_ctx = batch * seq
    n_lat = dec_batch * lat_len
    n = n_ctx + n_lat
    tm = 512
    tq_ctx = seq
    tq_lat = 256

    x = jnp.concatenate([x_prompt.reshape(n_ctx, d), x_sample.reshape(n_lat, d)], axis=0)
    cond = jnp.zeros((8, d), F32).at[0].set(c_ctx).at[1:1 + dec_batch].set(c)
    mods = _ada_call(cond, w_ada, b_ada).reshape(depth, 8, 6, 1, d)
    kw = _kw_call(peer_keys, peer_wq)
    w_in_bf = w_in.astype(BF16)
    w_out_bf = w_out.astype(BF16)
    u_bf = peer_u.astype(BF16)
    vt_bf = jnp.swapaxes(peer_v.astype(BF16), 1, 2)
    tables = _rope_tables(lat_len)

    cache_a = (cache_k_win.reshape(dec_batch, depth, past, -1), cache_v_win.reshape(dec_batch, depth, past, -1))
    cache_b = (cache_k_axial.reshape(dec_batch, depth, past, -1), cache_v_axial.reshape(dec_batch, depth, past, -1))
    cache_c = (cache_k_diff.reshape(dec_batch, depth, past, -1), cache_v_diff.reshape(dec_batch, depth, past, -1))
    lam_params = (lam_q1, lam_k1, lam_q2, lam_k2)

    o = jnp.zeros((n, d), BF16)
    kv_out = [[] for _ in range(6)]
    for l in range(depth):
        lam_init = 0.8 - 0.6 * math.exp(-0.3 * l)
        geo = dict(n_ctx=n_ctx, lat_len=lat_len, tm=tm)
        res = _in_call(l, x, mods, norm1_g, w_in_bf, ax_q_g, ax_k_g, None, None, latent=False, **geo)
        z = res[0]
        for k in range(6):
            kv_out[k].append(res[1 + k])
        z = _in_call(l, x, mods, norm1_g, w_in_bf, ax_q_g, ax_k_g, tables, z, latent=True, **geo)

        ctx = dict(nbatch=batch, seq=seq, row0=0, tq=tq_ctx)
        lat = dict(nbatch=dec_batch, seq=lat_len, row0=n_ctx, tq=tq_lat)
        win = dict(qcol=QA, kcol=KA, vcol=VA, ocol=0, heads=WIN_HEADS, kvh=WIN_KV, sink=win_sink)
        axl = dict(qcol=QB, kcol=KB, vcol=VB, ocol=WIN_HEADS * HEAD_DIM, heads=AX_HEADS, kvh=AX_KV)
        o = _gqa_call(l, z, o, **win, **ctx)
        o = _gqa_call(l, z, o, **axl, **ctx)
        o = _diff_call(l, z, o, lam_params, diff_subln_g, lam_init=lam_init, **ctx)
        o = _gqa_call(l, z, o, **win, **lat, window=True, cache=cache_a)
        o = _gqa_call(l, z, o, **axl, **lat, cache=cache_b)
        o = _diff_call(l, z, o, lam_params, diff_subln_g, lam_init=lam_init, cache=cache_c, **lat)

        x1, h2 = _out_call(l, o, x, w_out_bf, mods, norm2_g, **geo)
        s_t = _score_call(l, kw, h2, tm=tm)
        r1, e1, cnt, e0 = _router_call(s_t)
        x = _peer_call(l, h2, u_bf, vt_bf, r1.astype(BF16), e1.astype(BF16), cnt, e0, x1, mods, final_g,
                       final=(l == depth - 1), **geo)

    y_prompt = x[:n_ctx].reshape(batch, seq, d)
    y_sample = x[n_ctx:].reshape(dec_batch, lat_len, d)
    outs = []
    for k, nh in enumerate((WIN_KV, WIN_KV, AX_KV, AX_KV, DIFF_HEADS, DIFF_HEADS)):
        outs.append(jnp.stack([a.reshape(batch, seq, nh, HEAD_DIM) for a in kv_out[k]], axis=1))
    return (y_prompt, y_sample, *outs)
```

```python
import functools
import math

import jax
import jax.numpy as jnp
from jax import lax
from jax.experimental import pallas as pl
from jax.experimental.pallas import tpu as pltpu

F32 = jnp.float32
BF16 = jnp.bfloat16

HEAD_DIM = 128
GRID_W = 64
WINDOW = 128
ROPE_THETA = 10000.0
PEER_HEADS = 8
PEER_KEYS = 128
PEER_TOPK = 16
EPS = 1e-6
NEG_INF = -1e30
LOG2E = 1.4426950408889634

QA, KA, VA, QB, KB, VB, QC, KC, VC = 0, 512, 768, 1024, 2048, 2304, 2560, 3072, 3584
IN_COLS = 4096
WIN_HEADS, WIN_KV, AX_HEADS, AX_KV, DIFF_HEADS = 4, 2, 8, 2, 4

VMEM_LIMIT = 56 * 1024 * 1024


def _cparams(sem, **kw):
    return pltpu.CompilerParams(dimension_semantics=sem, vmem_limit_bytes=VMEM_LIMIT, **kw)


def _dot_nt(a, b):
    return lax.dot_general(a, b, (((1,), (1,)), ((), ())), preferred_element_type=F32)


def _dot(a, b):
    return jnp.dot(a, b, preferred_element_type=F32)


def _ada_kernel(cond_ref, w_ref, b_ref, o_ref):
    c = cond_ref[...]
    s = (c * jax.nn.sigmoid(c)).astype(BF16)
    o_ref[...] = _dot(s, w_ref[...].astype(BF16)) + b_ref[...]


def _ada_call(cond, w_ada, b_ada):
    depth, d, n6 = w_ada.shape
    tn = 1024
    return pl.pallas_call(
        _ada_kernel,
        out_shape=jax.ShapeDtypeStruct((depth, 8, n6), F32),
        grid=(depth, n6 // tn),
        in_specs=[
            pl.BlockSpec((8, d), lambda l, j: (0, 0)),
            pl.BlockSpec((None, d, tn), lambda l, j: (l, 0, j)),
            pl.BlockSpec((None, 1, tn), lambda l, j: (l, 0, j)),
        ],
        out_specs=pl.BlockSpec((None, 8, tn), lambda l, j: (l, 0, j)),
        compiler_params=_cparams(("parallel", "parallel")),
        name="ada",
    )(cond, w_ada, b_ada.reshape(depth, 1, n6))


def _kw_kernel(keys_ref, wq_ref, o_ref):
    kw = lax.dot_general(keys_ref[...], wq_ref[...], (((1,), (1,)), ((), ())),
                         preferred_element_type=F32, precision=lax.Precision.HIGHEST)
    o_ref[...] = kw.astype(BF16)


def _kw_call(peer_keys, peer_wq):
    depth, d, nq = peer_wq.shape
    ngrp = PEER_HEADS * 2
    sub = PEER_KEYS
    return pl.pallas_call(
        _kw_kernel,
        out_shape=jax.ShapeDtypeStruct((depth, ngrp * sub, d), BF16),
        grid=(depth, ngrp),
        in_specs=[
            pl.BlockSpec((None, None, None, sub, sub), lambda l, g: (l, g // 2, g % 2, 0, 0)),
            pl.BlockSpec((None, d, sub), lambda l, g: (l, 0, g)),
        ],
        out_specs=pl.BlockSpec((None, sub, d), lambda l, g: (l, g, 0)),
        compiler_params=_cparams(("parallel", "parallel")),
        name="peer_kw",
    )(peer_keys, peer_wq)


def _vt_kernel(v_ref, o_ref):
    o_ref[...] = v_ref[...].T.astype(BF16)


def _vt_call(peer_v):
    depth, n_exp, d = peer_v.shape
    te = 512
    return pl.pallas_call(
        _vt_kernel,
        out_shape=jax.ShapeDtypeStruct((depth, d, n_exp), BF16),
        grid=(depth, n_exp // te),
        in_specs=[pl.BlockSpec((None, te, d), lambda l, j: (l, j, 0))],
        out_specs=pl.BlockSpec((None, d, te), lambda l, j: (l, 0, j)),
        compiler_params=_cparams(("parallel", "parallel")),
        name="peer_vt",
    )(peer_v)


def _rope_tables(n_tok):
    n_rows = n_tok // GRID_W
    row_pos = jnp.repeat(jnp.arange(n_rows, dtype=jnp.int32), GRID_W).astype(F32)
    col_pos = jnp.tile(jnp.arange(GRID_W, dtype=jnp.int32), n_rows).astype(F32)
    lane = jnp.arange(HEAD_DIM)

    def table(pair, use_row):
        freqs = ROPE_THETA ** (-jnp.arange(pair, dtype=F32) / pair)
        f = freqs[lane % pair]
        pos = jnp.where(use_row[None, :], row_pos[:, None], col_pos[:, None])
        ang = pos * f[None, :]
        first = (lane % (2 * pair)) < pair
        return jnp.cos(ang), jnp.where(first[None, :], -jnp.sin(ang), jnp.sin(ang))

    c_ax, s_ax = table(32, lane < 64)
    c_df, s_df = table(16, (lane % 64) < 32)
    return c_ax, s_ax, c_df, s_df


def _rope(y, c, s, pair):
    lane = lax.broadcasted_iota(jnp.int32, y.shape, 1)
    first = (lane % (2 * pair)) < pair
    partner = jnp.where(first, pltpu.roll(y, HEAD_DIM - pair, axis=1), pltpu.roll(y, pair, axis=1))
    return y * c + partner * s


def _head_norm(y, g):
    return y * lax.rsqrt(jnp.mean(y * y, axis=-1, keepdims=True) + EPS) * g


def _in_kernel(*refs, latent):
    if latent:
        (x_ref, sh_ref, sc_ref, g_ref, w_ref, axq_ref, axk_ref, ca_ref, sa_ref, cd_ref, sd_ref,
         _z_in, z_ref, h_ref) = refs
        ka_ref = va_ref = kb_ref = vb_ref = kc_ref = vc_ref = None
    else:
        (x_ref, sh_ref, sc_ref, g_ref, w_ref, axq_ref, axk_ref,
         z_ref, ka_ref, va_ref, kb_ref, vb_ref, kc_ref, vc_ref, h_ref) = refs
    j = pl.program_id(1)

    @pl.when(j == 0)
    def _():
        x = x_ref[...]
        y = x * lax.rsqrt(jnp.mean(x * x, axis=-1, keepdims=True) + EPS) * g_ref[...]
        h_ref[...] = (y * (1.0 + sc_ref[...]) + sh_ref[...]).astype(BF16)

    def rope_ax(y):
        return _rope(y, ca_ref[...], sa_ref[...], 32) if latent else y

    def rope_df(y):
        return _rope(y, cd_ref[...], sd_ref[...], 16) if latent else y

    def head(z, k):
        return z[:, k * HEAD_DIM:(k + 1) * HEAD_DIM]

    def put(k, y):
        z_ref[:, k * HEAD_DIM:(k + 1) * HEAD_DIM] = y.astype(BF16)

    def keep(ref, k, y):
        if not latent:
            ref[:, k * HEAD_DIM:(k + 1) * HEAD_DIM] = y

    @pl.when(j == 0)
    def _():
        z = _dot(h_ref[...], w_ref[...])
        for k in range(4):
            put(k, rope_ax(head(z, k)))
        for k in range(2):
            y = head(z, 4 + k)
            keep(ka_ref, k, y)
            put(4 + k, rope_ax(y))
        for k in range(2):
            y = head(z, 6 + k)
            keep(va_ref, k, y)
            put(6 + k, y)

    @pl.when(j == 1)
    def _():
        z = _dot(h_ref[...], w_ref[...])
        for k in range(8):
            put(k, rope_ax(_head_norm(head(z, k), axq_ref[...])))

    @pl.when(j == 2)
    def _():
        z = _dot(h_ref[...], w_ref[...])
        for k in range(2):
            y = _head_norm(head(z, k), axk_ref[...])
            keep(kb_ref, k, y)
            put(k, rope_ax(y))
        for k in range(2):
            y = head(z, 2 + k)
            keep(vb_ref, k, y)
            put(2 + k, y)
        for k in range(4):
            put(4 + k, rope_df(head(z, 4 + k)))

    @pl.when(j == 3)
    def _():
        z = _dot(h_ref[...], w_ref[...])
        for k in range(4):
            y = head(z, k)
            keep(kc_ref, k, y)
            put(k, rope_df(y))
        for k in range(4):
            y = head(z, 4 + k)
            keep(vc_ref, k, y)
            put(4 + k, y)


def _in_call(l, x, mods, norm1_g, w_in, ax_q_g, ax_k_g, tables, z, *, latent, n_ctx, lat_len, tm):
    n, d = x.shape
    depth = w_in.shape[0]
    nb_ctx = n_ctx // tm
    bpl = lat_len // tm
    if latent:
        nblk = (n - n_ctx) // tm
        row = lambda i: i + nb_ctx
        grp = lambda i: 1 + i // bpl
    else:
        nblk = nb_ctx
        row = lambda i: i
        grp = lambda i: 0
    cb = 1024
    in_specs = [
        pl.BlockSpec((tm, d), lambda i, j: (row(i), 0)),
        pl.BlockSpec((None, None, None, 1, d), lambda i, j: (l, grp(i), 0, 0, 0)),
        pl.BlockSpec((None, None, None, 1, d), lambda i, j: (l, grp(i), 1, 0, 0)),
        pl.BlockSpec((None, 1, d), lambda i, j: (l, 0, 0)),
        pl.BlockSpec((None, d, cb), lambda i, j: (l, 0, j)),
        pl.BlockSpec((None, 1, HEAD_DIM), lambda i, j: (l, 0, 0)),
        pl.BlockSpec((None, 1, HEAD_DIM), lambda i, j: (l, 0, 0)),
    ]
    args = [x, mods, mods, norm1_g.reshape(depth, 1, d), w_in,
            ax_q_g.reshape(depth, 1, HEAD_DIM), ax_k_g.reshape(depth, 1, HEAD_DIM)]
    z_spec = pl.BlockSpec((tm, cb), lambda i, j: (row(i), j))
    z_shape = jax.ShapeDtypeStruct((n, IN_COLS), BF16)
    scratch = [pltpu.VMEM((tm, d), BF16)]
    if latent:
        tspec = pl.BlockSpec((tm, HEAD_DIM), lambda i, j: (i % bpl, 0))
        in_specs += [tspec] * 4 + [pl.BlockSpec(memory_space=pl.ANY)]
        args += list(tables) + [z]
        return pl.pallas_call(
            functools.partial(_in_kernel, latent=True),
            out_shape=z_shape, grid=(nblk, 4), in_specs=in_specs, out_specs=z_spec,
            scratch_shapes=scratch, input_output_aliases={len(args) - 1: 0},
            compiler_params=_cparams(("parallel", "arbitrary")), name="in_proj_latent",
        )(*args)
    kv2 = jax.ShapeDtypeStruct((n_ctx, 2 * HEAD_DIM), F32)
    kv4 = jax.ShapeDtypeStruct((n_ctx, 4 * HEAD_DIM), F32)
    s2 = pl.BlockSpec((tm, 2 * HEAD_DIM), lambda i, j: (i, 0))
    s4 = pl.BlockSpec((tm, 4 * HEAD_DIM), lambda i, j: (i, 0))
    return pl.pallas_call(
        functools.partial(_in_kernel, latent=False),
        out_shape=(z_shape, kv2, kv2, kv2, kv2, kv4, kv4), grid=(nblk, 4), in_specs=in_specs,
        out_specs=(z_spec, s2, s2, s2, s2, s4, s4), scratch_shapes=scratch,
        compiler_params=_cparams(("parallel", "arbitrary")), name="in_proj_context",
    )(*args)


def _gqa_kernel(*refs, groups, tq, window, has_cache, has_sink, l, lat_len):
    refs = list(refs)
    sink_ref = refs.pop(0) if has_sink else None
    q_ref, k_ref, v_ref = refs[:3]
    refs = refs[3:]
    if has_cache:
        ck_ref, cv_ref = refs[:2]
        refs = refs[2:]
    o_ref = refs[-1]
    kv = pl.program_id(1)
    qi = pl.program_id(2)
    scale = HEAD_DIM ** -0.5

    if window:
        span = tq + 2 * WINDOW
        start = jnp.clip(qi * tq - WINDOW, 0, lat_len - span)
        start = pl.multiple_of(start, WINDOW)
        k = k_ref[pl.ds(start, span), :]
        v = v_ref[pl.ds(start, span), :]
        qpos = qi * tq + lax.broadcasted_iota(jnp.int32, (tq, span), 0)
        kpos = start + lax.broadcasted_iota(jnp.int32, (tq, span), 1)
        ok = jnp.abs(qpos - kpos) <= WINDOW
    else:
        k = k_ref[...]
        v = v_ref[...]
    if has_cache:
        ck = ck_ref[...].astype(BF16)
        cv = cv_ref[...].astype(BF16)

    c2 = scale * LOG2E
    for g in range(groups):
        q = q_ref[:, g * HEAD_DIM:(g + 1) * HEAD_DIM]
        s = _dot_nt(q, k)
        if window:
            s = jnp.where(ok, s, NEG_INF)
        m = jnp.max(s, axis=-1, keepdims=True)
        if has_cache:
            s2 = _dot_nt(q, ck)
            m = jnp.maximum(m, jnp.max(s2, axis=-1, keepdims=True))
        if has_sink:
            sink = sink_ref[l, kv * groups + g] * (1.0 / scale)
            m = jnp.maximum(m, sink)
        e = jnp.exp2((s - m) * c2)
        den = jnp.sum(e, axis=-1, keepdims=True)
        o = _dot(e.astype(BF16), v)
        if has_cache:
            e2 = jnp.exp2((s2 - m) * c2)
            den = den + jnp.sum(e2, axis=-1, keepdims=True)
            o = o + _dot(e2.astype(BF16), cv)
        if has_sink:
            den = den + jnp.exp2((sink - m) * c2)
        o_ref[:, g * HEAD_DIM:(g + 1) * HEAD_DIM] = (o / den).astype(BF16)


def _gqa_call(l, z, o, *, qcol, kcol, vcol, ocol, heads, kvh, nbatch, seq, row0, tq,
              window=False, cache=None, sink=None):
    groups = heads // kvh
    gw = groups * HEAD_DIM
    nq = seq // tq
    qrow = lambda b, qi: row0 // tq + b * nq + qi
    krow = lambda b: row0 // seq + b
    in_specs, args = [], []
    if sink is not None:
        in_specs.append(pl.BlockSpec(memory_space=pltpu.SMEM))
        args.append(sink)
    in_specs += [
        pl.BlockSpec((tq, gw), lambda b, kv, qi: (qrow(b, qi), qcol // gw + kv)),
        pl.BlockSpec((seq, HEAD_DIM), lambda b, kv, qi: (krow(b), kcol // HEAD_DIM + kv)),
        pl.BlockSpec((seq, HEAD_DIM), lambda b, kv, qi: (krow(b), vcol // HEAD_DIM + kv)),
    ]
    args += [z, z, z]
    if cache is not None:
        past = cache[0].shape[2]
        cspec = pl.BlockSpec((None, None, past, HEAD_DIM), lambda b, kv, qi: (b, l, 0, kv))
        in_specs += [cspec, cspec]
        args += list(cache)
    in_specs.append(pl.BlockSpec(memory_space=pl.ANY))
    args.append(o)
    kern = functools.partial(_gqa_kernel, groups=groups, tq=tq, window=window,
                             has_cache=cache is not None, has_sink=sink is not None, l=l, lat_len=seq)
    return pl.pallas_call(
        kern, out_shape=jax.ShapeDtypeStruct(o.shape, o.dtype), grid=(nbatch, kvh, nq),
        in_specs=in_specs,
        out_specs=pl.BlockSpec((tq, gw), lambda b, kv, qi: (qrow(b, qi), ocol // gw + kv)),
        input_output_aliases={len(args) - 1: 0},
        compiler_params=_cparams(("parallel", "parallel", "arbitrary")),
        name="attn_window" if sink is not None else "attn_axial",
    )(*args)


def _diff_kernel(*refs, has_cache, lam_init):
    refs = list(refs)
    lq1, lk1, lq2, lk2, gain_ref, q_ref, k_ref, v_ref = refs[:8]
    refs = refs[8:]
    if has_cache:
        ck_ref, cv_ref = refs[:2]
    o_ref = refs[-1]
    h = pl.program_id(1)
    half = HEAD_DIM // 2
    scale = half ** -0.5

    lam = (jnp.exp(jnp.sum(lq1[...] * lk1[...], axis=-1, keepdims=True))
           - jnp.exp(jnp.sum(lq2[...] * lk2[...], axis=-1, keepdims=True)) + lam_init)

    q = q_ref[...]
    k = k_ref[...]
    v = v_ref[...]
    lane = lax.broadcasted_iota(jnp.int32, (1, HEAD_DIM), 1)
    lo = jnp.where(lane < half, 1.0, 0.0).astype(BF16)
    qs = (q * lo, q * (1.0 - lo))
    if has_cache:
        ck = ck_ref[...].astype(BF16)
        cv = cv_ref[...].astype(BF16)

    probs = []
    c2 = scale * LOG2E
    for c in range(2):
        s = _dot_nt(qs[c], k)
        m = jnp.max(s, axis=-1, keepdims=True)
        if has_cache:
            s2 = _dot_nt(qs[c], ck)
            m = jnp.maximum(m, jnp.max(s2, axis=-1, keepdims=True))
        e = jnp.exp2((s - m) * c2)
        den = jnp.sum(e, axis=-1, keepdims=True)
        if has_cache:
            e2 = jnp.exp2((s2 - m) * c2)
            den = den + jnp.sum(e2, axis=-1, keepdims=True)
        else:
            e2 = None
        probs.append((e, e2, 1.0 / den))
    (e_a, e2_a, r_a), (e_b, e2_b, r_b) = probs
    r_b = lam * r_b
    o = _dot((e_a * r_a - e_b * r_b).astype(BF16), v)
    if has_cache:
        o = o + _dot((e2_a * r_a - e2_b * r_b).astype(BF16), cv)
    gain = gain_ref[pl.ds(h, 1), :]
    o = o * lax.rsqrt(jnp.mean(o * o, axis=-1, keepdims=True) + EPS) * gain
    o_ref[...] = (o * (1.0 - lam_init)).astype(BF16)


def _diff_call(l, z, o, lam_params, gain, *, nbatch, seq, row0, tq, lam_init, cache=None):
    nq = seq // tq
    depth = gain.shape[0]
    half = HEAD_DIM // 2
    qrow = lambda b, qi: row0 // tq + b * nq + qi
    krow = lambda b: row0 // seq + b
    lspec = pl.BlockSpec((None, 1, half), lambda b, h, qi: (l, 0, 0))
    in_specs = [lspec] * 4 + [
        pl.BlockSpec((None, DIFF_HEADS, HEAD_DIM), lambda b, h, qi: (l, 0, 0)),
        pl.BlockSpec((tq, HEAD_DIM), lambda b, h, qi: (qrow(b, qi), QC // HEAD_DIM + h)),
        pl.BlockSpec((seq, HEAD_DIM), lambda b, h, qi: (krow(b), KC // HEAD_DIM + h)),
        pl.BlockSpec((seq, HEAD_DIM), lambda b, h, qi: (krow(b), VC // HEAD_DIM + h)),
    ]
    args = [p.reshape(depth, 1, half) for p in lam_params] + [gain, z, z, z]
    if cache is not None:
        past = cache[0].shape[2]
        cspec = pl.BlockSpec((None, None, past, HEAD_DIM), lambda b, h, qi: (b, l, 0, h))
        in_specs += [cspec, cspec]
        args += list(cache)
    in_specs.append(pl.BlockSpec(memory_space=pl.ANY))
    args.append(o)
    ocol0 = (WIN_HEADS + AX_HEADS)
    return pl.pallas_call(
        functools.partial(_diff_kernel, has_cache=cache is not None, lam_init=lam_init),
        out_shape=jax.ShapeDtypeStruct(o.shape, o.dtype), grid=(nbatch, DIFF_HEADS, nq),
        in_specs=in_specs,
        out_specs=pl.BlockSpec((tq, HEAD_DIM), lambda b, h, qi: (qrow(b, qi), ocol0 + h)),
        input_output_aliases={len(args) - 1: 0},
        compiler_params=_cparams(("parallel", "parallel", "arbitrary")),
        name="attn_diff",
    )(*args)


def _out_kernel(o_ref, x_ref, w_ref, g1_ref, sh2_ref, sc2_ref, n2g_ref, x1_ref, h2_ref):
    a = _dot(o_ref[...], w_ref[...])
    x1 = x_ref[...] + g1_ref[...] * a
    x1_ref[...] = x1
    y = x1 * lax.rsqrt(jnp.mean(x1 * x1, axis=-1, keepdims=True) + EPS) * n2g_ref[...]
    h2_ref[...] = (y * (1.0 + sc2_ref[...]) + sh2_ref[...]).astype(BF16)


def _group_of_block(i, nb_ctx, bpl):
    return jnp.where(i < nb_ctx, 0, 1 + (i - nb_ctx) // bpl)


def _out_call(l, o, x, w_out, mods, norm2_g, *, n_ctx, lat_len, tm):
    n, d = x.shape
    depth = w_out.shape[0]
    nb_ctx, bpl = n_ctx // tm, lat_len // tm
    grp = lambda i: _group_of_block(i, nb_ctx, bpl)
    mspec = lambda which: pl.BlockSpec((None, None, None, 1, d), lambda i: (l, grp(i), which, 0, 0))
    return pl.pallas_call(
        _out_kernel,
        out_shape=(jax.ShapeDtypeStruct((n, d), F32), jax.ShapeDtypeStruct((n, d), BF16)),
        grid=(n // tm,),
        in_specs=[
            pl.BlockSpec((tm, d), lambda i: (i, 0)),
            pl.BlockSpec((tm, d), lambda i: (i, 0)),
            pl.BlockSpec((None, d, d), lambda i: (l, 0, 0)),
            mspec(2), mspec(3), mspec(4),
            pl.BlockSpec((None, 1, d), lambda i: (l, 0, 0)),
        ],
        out_specs=(pl.BlockSpec((tm, d), lambda i: (i, 0)), pl.BlockSpec((tm, d), lambda i: (i, 0))),
        compiler_params=_cparams(("parallel",)),
        name="out_proj",
    )(o, x, w_out, mods, mods, mods, norm2_g.reshape(depth, 1, d))


def _score_kernel(kw_ref, h2_ref, o_ref):
    o_ref[...] = _dot_nt(kw_ref[...], h2_ref[...])


def _score_call(l, kw, h2, *, tm):
    n, d = h2.shape
    rows = kw.shape[1]
    return pl.pallas_call(
        _score_kernel,
        out_shape=jax.ShapeDtypeStruct((rows, n), F32),
        grid=(n // tm,),
        in_specs=[pl.BlockSpec((None, rows, d), lambda i: (l, 0, 0)),
                  pl.BlockSpec((tm, d), lambda i: (i, 0))],
        out_specs=pl.BlockSpec((rows, tm), lambda i: (0, i)),
        compiler_params=_cparams(("parallel",)),
        name="peer_scores",
    )(kw, h2)


def _sort_pairs(n):
    pairs = []

    def merge(lo, hi, r):
        step = r * 2
        if step < hi - lo:
            merge(lo, hi, step)
            merge(lo + r, hi, step)
            for i in range(lo + r, hi - r, step):
                pairs.append((i, i + r))
        else:
            pairs.append((lo, lo + r))

    def sort(lo, hi):
        if hi - lo >= 1:
            mid = lo + (hi - lo) // 2
            sort(lo, mid)
            sort(mid + 1, hi)
            merge(lo, hi, 1)

    sort(0, n - 1)
    return pairs


_SORT16 = _sort_pairs(PEER_TOPK)


def _vmax(a, b):
    if a is None:
        return b
    if b is None:
        return a
    return jnp.maximum(a, b)


def _vmin(a, b):
    if a is None or b is None:
        return None
    return jnp.minimum(a, b)


def _bitonic_merge_desc(x):
    x = list(x)
    d = PEER_TOPK // 2
    while d >= 1:
        for i in range(PEER_TOPK):
            if (i & d) == 0:
                a, b = x[i], x[i + d]
                x[i], x[i + d] = _vmax(a, b), _vmin(a, b)
        d //= 2
    return x


def _merge_top16(a, b):
    a = list(a) + [None] * (PEER_TOPK - len(a))
    b = list(b) + [None] * (PEER_TOPK - len(b))
    return _bitonic_merge_desc([_vmax(a[k], b[PEER_TOPK - 1 - k]) for k in range(PEER_TOPK)])


def _router_kernel(s_ref, r1_ref, e1_ref, cnt_ref, e0_ref, top_ref):
    nk = PEER_KEYS
    nv = nk // 8
    tk = PEER_TOPK
    sub = lax.broadcasted_iota(jnp.int32, (8, s_ref.shape[1]), 0)

    for g in range(2 * PEER_HEADS):
        x = [s_ref[g * nk + 8 * r:g * nk + 8 * r + 8, :] for r in range(nv)]
        for (i, j) in _SORT16:
            x[i], x[j] = jnp.maximum(x[i], x[j]), jnp.minimum(x[i], x[j])
        for shift in (4, 2, 1):
            y = [pltpu.roll(v, shift, axis=0) for v in x]
            x = _bitonic_merge_desc([jnp.maximum(x[k], y[tk - 1 - k]) for k in range(tk)])
        for a in range(tk):
            top_ref[g * tk + a] = x[a]

    def packed(c, a):
        out = top_ref[c * tk + a]
        for h in range(1, PEER_HEADS):
            out = jnp.where(sub == h, top_ref[(2 * h + c) * tk + a], out)
        return out

    p0 = [packed(0, a) for a in range(tk)]
    p1 = [packed(1, b) for b in range(tk)]
    rows = [[p0[a] + p1[b] for b in range(tk // (a + 1))] for a in range(8)]
    col0 = [p0[a] + p1[0] for a in range(8, tk)]
    t1 = _merge_top16(rows[0], _merge_top16(rows[1], col0))
    t2 = _merge_top16(_merge_top16(rows[2], rows[3]), _merge_top16(rows[4], rows[5]))
    t3 = _merge_top16(rows[6], rows[7])
    best = _merge_top16(t1, _merge_top16(t2, t3))
    tau_p = best[tk - 1]
    zsum = jnp.ones_like(tau_p)
    for k in range(1, tk):
        zsum = zsum + jnp.exp(best[k] - best[0])
    zinv_p = 1.0 / zsum

    cnt_p = []
    for a in range(tk):
        acc = jnp.zeros_like(tau_p)
        for b in range(tk):
            acc = acc + jnp.where(p0[a] + p1[b] >= tau_p, 1.0, 0.0)
        cnt_p.append(acc)

    def pair_words(v):
        bits = pltpu.bitcast(v, jnp.uint32)
        return bits | lax.shift_right_logical(bits, jnp.uint32(16))

    for h in range(PEER_HEADS):
        shape = (8, s_ref.shape[1])
        zinv = jnp.broadcast_to(zinv_p[h:h + 1, :], shape)
        cnt_a = [jnp.broadcast_to(cnt_p[a][h:h + 1, :], shape) for a in range(tk)]
        t0 = [top_ref[(2 * h) * tk + a] for a in range(tk)]
        t1h = [top_ref[(2 * h + 1) * tk + b] for b in range(tk)]
        for rr in range(nv // 2):
            ranks, gates = [], []
            for r in (2 * rr, 2 * rr + 1):
                s0 = s_ref[(2 * h) * nk + 8 * r:(2 * h) * nk + 8 * r + 8, :]
                s1 = s_ref[(2 * h + 1) * nk + 8 * r:(2 * h + 1) * nk + 8 * r + 8, :]
                rank = jnp.full(shape, float(tk), F32)
                cnt = jnp.zeros(shape, F32)
                for b in range(tk - 1, -1, -1):
                    rank = jnp.where(s1 >= t1h[b], float(b), rank)
                    cnt = jnp.where(s0 >= t0[b], cnt_a[b], cnt)
                ranks.append(rank)
                gates.append(jnp.exp(s1 - t1h[0]))
                e0 = (jnp.exp(s0 - t0[0]) * zinv).astype(BF16).astype(F32)
                cnt_ref[h, 8 * r:8 * r + 8, :] = pair_words(cnt)
                e0_ref[h, 8 * r:8 * r + 8, :] = pair_words(e0)
            r1_ref[h, 16 * rr:16 * rr + 16, :] = jnp.concatenate(ranks, axis=0).astype(BF16)
            e1_ref[h, 16 * rr:16 * rr + 16, :] = jnp.concatenate(gates, axis=0).astype(BF16)


def _router_call(s_t):
    rows, n = s_t.shape
    tl = 128
    half = jax.ShapeDtypeStruct((PEER_HEADS, PEER_KEYS, n), BF16)
    words = jax.ShapeDtypeStruct((PEER_HEADS, PEER_KEYS, n), jnp.uint32)
    spec = pl.BlockSpec((PEER_HEADS, PEER_KEYS, tl), lambda i: (0, 0, i))
    return pl.pallas_call(
        _router_kernel,
        out_shape=(half, half, words, words),
        grid=(n // tl,),
        in_specs=[pl.BlockSpec((rows, tl), lambda i: (0, i))],
        out_specs=(spec, spec, spec, spec),
        scratch_shapes=[pltpu.VMEM((2 * PEER_HEADS * PEER_TOPK, 8, tl), F32)],
        compiler_params=_cparams(("parallel",)),
        name="peer_router",
    )(s_t)


def _gelu(a):
    return 0.5 * a * (1.0 + jnp.tanh(math.sqrt(2.0 / math.pi) * (a + 0.044715 * (a * a * a))))


def _peer_kernel(*refs, final, ib):
    if final:
        (h2_ref, u_ref, vt_ref, r1_ref, e1_ref, cnt_ref, e0_ref, x1_ref, g2_ref, fg_ref,
         o_ref, acc_ref, p_ref, cs_ref, es_ref) = refs
    else:
        (h2_ref, u_ref, vt_ref, r1_ref, e1_ref, cnt_ref, e0_ref, x1_ref, g2_ref,
         o_ref, acc_ref, p_ref, cs_ref, es_ref) = refs
    e = pl.program_id(1)
    t = h2_ref.shape[0]
    nchunk = PEER_KEYS // 16

    @pl.when(e == 0)
    def _():
        acc_ref[...] = jnp.zeros_like(acc_ref)

    a_t = _dot_nt(u_ref[...], h2_ref[...])
    for ii in range(ib):
        for h in range(PEER_HEADS):
            slot = (ii % 2) * PEER_HEADS + h
            cs_ref[slot] = pltpu.bitcast(jnp.broadcast_to(cnt_ref[h, ii:ii + 1, :], (8, t)), BF16)
            es_ref[slot] = pltpu.bitcast(jnp.broadcast_to(e0_ref[h, ii:ii + 1, :], (8, t)), BF16)
        w = [None] * nchunk
        for h in range(PEER_HEADS):
            cnt = cs_ref[(ii % 2) * PEER_HEADS + h]
            e0 = es_ref[(ii % 2) * PEER_HEADS + h]
            for jc in range(nchunk):
                r1 = r1_ref[h, jc * 16:(jc + 1) * 16, :]
                e1 = e1_ref[h, jc * 16:(jc + 1) * 16, :]
                gate = jnp.where(r1 < cnt, e0 * e1, jnp.zeros_like(e1))
                w[jc] = gate if w[jc] is None else w[jc] + gate
        for jc in range(nchunk):
            r0 = ii * PEER_KEYS + jc * 16
            p_ref[r0:r0 + 16, :] = w[jc] * _gelu(a_t[r0:r0 + 16, :]).astype(BF16)
    acc_ref[...] += _dot(vt_ref[...], p_ref[...])

    @pl.when(e == pl.num_programs(1) - 1)
    def _():
        x2 = x1_ref[...] + g2_ref[...] * acc_ref[...].T
        if final:
            x2 = x2 * lax.rsqrt(jnp.mean(x2 * x2, axis=-1, keepdims=True) + EPS) * fg_ref[...]
        o_ref[...] = x2


def _peer_call(l, h2, u_bf, vt_bf, r1, e1, cnt, e0, x1, mods, final_g, *, n_ctx, lat_len, tm, final):
    n, d = x1.shape
    ib = 8
    eb = ib * PEER_KEYS
    n_exp = u_bf.shape[1]
    nb_ctx, bpl = n_ctx // tm, lat_len // tm
    grp = lambda i: _group_of_block(i, nb_ctx, bpl)
    once = dict(pipeline_mode=pl.Buffered(1))
    in_specs = [
        pl.BlockSpec((tm, d), lambda i, e: (i, 0), **once),
        pl.BlockSpec((None, eb, d), lambda i, e: (l, e, 0)),
        pl.BlockSpec((None, d, eb), lambda i, e: (l, 0, e)),
        pl.BlockSpec((PEER_HEADS, PEER_KEYS, tm), lambda i, e: (0, 0, i), **once),
        pl.BlockSpec((PEER_HEADS, PEER_KEYS, tm), lambda i, e: (0, 0, i), **once),
        pl.BlockSpec((PEER_HEADS, ib, tm), lambda i, e: (0, e, i)),
        pl.BlockSpec((PEER_HEADS, ib, tm), lambda i, e: (0, e, i)),
        pl.BlockSpec((tm, d), lambda i, e: (i, 0), **once),
        pl.BlockSpec((None, None, None, 1, d), lambda i, e: (l, grp(i), 5, 0, 0)),
    ]
    args = [h2, u_bf, vt_bf, r1, e1, cnt, e0, x1, mods]
    if final:
        in_specs.append(pl.BlockSpec((1, d), lambda i, e: (0, 0)))
        args.append(final_g.reshape(1, d))
    nstage = 2 * PEER_HEADS
    return pl.pallas_call(
        functools.partial(_peer_kernel, final=final, ib=ib),
        out_shape=jax.ShapeDtypeStruct((n, d), F32),
        grid=(n // tm, n_exp // eb),
        in_specs=in_specs,
        out_specs=pl.BlockSpec((tm, d), lambda i, e: (i, 0)),
        scratch_shapes=[pltpu.VMEM((d, tm), F32), pltpu.VMEM((eb, tm), BF16),
                        pltpu.VMEM((nstage, 16, tm), BF16), pltpu.VMEM((nstage, 16, tm), BF16)],
        compiler_params=_cparams(("parallel", "arbitrary")),
        name="peer_experts",
    )(*args)


def kernel(x_prompt, x_sample, cache_k_win, cache_v_win, cache_k_axial, cache_v_axial, cache_k_diff,
           cache_v_diff, c, c_ctx, norm1_g, w_ada, b_ada, w_in, win_sink, ax_q_g, ax_k_g, lam_q1, lam_k1,
           lam_q2, lam_k2, diff_subln_g, w_out, norm2_g, peer_wq, peer_keys, peer_u, peer_v, final_g):
    batch, seq, d = x_prompt.shape
    dec_batch, lat_len, _ = x_sample.shape
    depth = w_in.shape[0]
    past = cache_k_win.shape[2]
    n_ctx = batch * seq
    n_lat = dec_batch * lat_len
    n = n_ctx + n_lat
    tm = 512
    tq_ctx = seq
    tq_lat = 256

    x = jnp.concatenate([x_prompt.reshape(n_ctx, d), x_sample.reshape(n_lat, d)], axis=0)
    cond = jnp.zeros((8, d), F32).at[0].set(c_ctx).at[1:1 + dec_batch].set(c)
    mods = _ada_call(cond, w_ada, b_ada).reshape(depth, 8, 6, 1, d)
    kw = _kw_call(peer_keys, peer_wq)
    w_in_bf = w_in.astype(BF16)
    w_out_bf = w_out.astype(BF16)
    u_bf = peer_u.astype(BF16)
    vt_bf = _vt_call(peer_v)
    tables = _rope_tables(lat_len)

    cache_a = (cache_k_win.reshape(dec_batch, depth, past, -1), cache_v_win.reshape(dec_batch, depth, past, -1))
    cache_b = (cache_k_axial.reshape(dec_batch, depth, past, -1), cache_v_axial.reshape(dec_batch, depth, past, -1))
    cache_c = (cache_k_diff.reshape(dec_batch, depth, past, -1), cache_v_diff.reshape(dec_batch, depth, past, -1))
    lam_params = (lam_q1, lam_k1, lam_q2, lam_k2)

    o = jnp.zeros((n, d), BF16)
    kv_out = [[] for _ in range(6)]
    for l in range(depth):
        lam_init = 0.8 - 0.6 * math.exp(-0.3 * l)
        geo = dict(n_ctx=n_ctx, lat_len=lat_len, tm=tm)
        res = _in_call(l, x, mods, norm1_g, w_in_bf, ax_q_g, ax_k_g, None, None, latent=False, **geo)
        z = res[0]
        for k in range(6):
            kv_out[k].append(res[1 + k])
        z = _in_call(l, x, mods, norm1_g, w_in_bf, ax_q_g, ax_k_g, tables, z, latent=True, **geo)

        ctx = dict(nbatch=batch, seq=seq, row0=0, tq=tq_ctx)
        lat = dict(nbatch=dec_batch, seq=lat_len, row0=n_ctx, tq=tq_lat)
        win = dict(qcol=QA, kcol=KA, vcol=VA, ocol=0, heads=WIN_HEADS, kvh=WIN_KV, sink=win_sink)
        axl = dict(qcol=QB, kcol=KB, vcol=VB, ocol=WIN_HEADS * HEAD_DIM, heads=AX_HEADS, kvh=AX_KV)
        o = _gqa_call(l, z, o, **win, **ctx)
        o = _gqa_call(l, z, o, **axl, **ctx)
        o = _diff_call(l, z, o, lam_params, diff_subln_g, lam_init=lam_init, **ctx)
        o = _gqa_call(l, z, o, **win, **lat, window=True, cache=cache_a)
        o = _gqa_call(l, z, o, **axl, **lat, cache=cache_b)
        o = _diff_call(l, z, o, lam_params, diff_subln_g, lam_init=lam_init, cache=cache_c, **lat)

        x1, h2 = _out_call(l, o, x, w_out_bf, mods, norm2_g, **geo)
        s_t = _score_call(l, kw, h2, tm=tm)
        r1, e1, cnt, e0 = _router_call(s_t)
        x = _peer_call(l, h2, u_bf, vt_bf, r1, e1, cnt, e0, x1, mods, final_g,
                       final=(l == depth - 1), **geo)

    y_prompt = x[:n_ctx].reshape(batch, seq, d)
    y_sample = x[n_ctx:].reshape(dec_batch, lat_len, d)
    outs = []
    for k, nh in enumerate((WIN_KV, WIN_KV, AX_KV, AX_KV, DIFF_HEADS, DIFF_HEADS)):
        outs.append(jnp.stack([a.reshape(batch, seq, nh, HEAD_DIM) for a in kv_out[k]], axis=1))
    return (y_prompt, y_sample, *outs)
```

```python
import functools
import math

import jax
import jax.numpy as jnp
from jax import lax
from jax.experimental import pallas as pl
from jax.experimental.pallas import tpu as pltpu

F32 = jnp.float32
BF16 = jnp.bfloat16

HEAD_DIM = 128
GRID_W = 64
WINDOW = 128
ROPE_THETA = 10000.0
PEER_HEADS = 8
PEER_KEYS = 128
PEER_TOPK = 16
EPS = 1e-6
NEG_INF = -1e30
LOG2E = 1.4426950408889634

QA, KA, VA, QB, KB, VB, QC, KC, VC = 0, 512, 768, 1024, 2048, 2304, 2560, 3072, 3584
IN_COLS = 4096
WIN_HEADS, WIN_KV, AX_HEADS, AX_KV, DIFF_HEADS = 4, 2, 8, 2, 4

VMEM_LIMIT = 56 * 1024 * 1024


def _cparams(sem, **kw):
    return pltpu.CompilerParams(dimension_semantics=sem, vmem_limit_bytes=VMEM_LIMIT, **kw)


def _dot_nt(a, b):
    return lax.dot_general(a, b, (((1,), (1,)), ((), ())), preferred_element_type=F32)


def _dot(a, b):
    return jnp.dot(a, b, preferred_element_type=F32)


def _ada_kernel(cond_ref, w_ref, b_ref, o_ref):
    c = cond_ref[...]
    s = (c * jax.nn.sigmoid(c)).astype(BF16)
    o_ref[...] = _dot(s, w_ref[...].astype(BF16)) + b_ref[...]


def _ada_call(cond, w_ada, b_ada):
    depth, d, n6 = w_ada.shape
    tn = 1024
    return pl.pallas_call(
        _ada_kernel,
        out_shape=jax.ShapeDtypeStruct((depth, 8, n6), F32),
        grid=(depth, n6 // tn),
        in_specs=[
            pl.BlockSpec((8, d), lambda l, j: (0, 0)),
            pl.BlockSpec((None, d, tn), lambda l, j: (l, 0, j)),
            pl.BlockSpec((None, 1, tn), lambda l, j: (l, 0, j)),
        ],
        out_specs=pl.BlockSpec((None, 8, tn), lambda l, j: (l, 0, j)),
        compiler_params=_cparams(("parallel", "parallel")),
        name="ada",
    )(cond, w_ada, b_ada.reshape(depth, 1, n6))


def _kw_kernel(keys_ref, wq_ref, o_ref):
    kw = lax.dot_general(keys_ref[...], wq_ref[...], (((1,), (1,)), ((), ())),
                         preferred_element_type=F32, precision=lax.Precision.HIGHEST)
    o_ref[...] = kw.astype(BF16)


def _kw_call(peer_keys, peer_wq):
    depth, d, nq = peer_wq.shape
    ngrp = PEER_HEADS * 2
    sub = PEER_KEYS
    return pl.pallas_call(
        _kw_kernel,
        out_shape=jax.ShapeDtypeStruct((depth, ngrp * sub, d), BF16),
        grid=(depth, ngrp),
        in_specs=[
            pl.BlockSpec((None, None, None, sub, sub), lambda l, g: (l, g // 2, g % 2, 0, 0)),
            pl.BlockSpec((None, d, sub), lambda l, g: (l, 0, g)),
        ],
        out_specs=pl.BlockSpec((None, sub, d), lambda l, g: (l, g, 0)),
        compiler_params=_cparams(("parallel", "parallel")),
        name="peer_kw",
    )(peer_keys, peer_wq)


def _vt_kernel(v_ref, o_ref):
    o_ref[...] = v_ref[...].T.astype(BF16)


def _vt_call(peer_v):
    depth, n_exp, d = peer_v.shape
    te = 512
    return pl.pallas_call(
        _vt_kernel,
        out_shape=jax.ShapeDtypeStruct((depth, d, n_exp), BF16),
        grid=(depth, n_exp // te),
        in_specs=[pl.BlockSpec((None, te, d), lambda l, j: (l, j, 0))],
        out_specs=pl.BlockSpec((None, d, te), lambda l, j: (l, 0, j)),
        compiler_params=_cparams(("parallel", "parallel")),
        name="peer_vt",
    )(peer_v)


def _rope_tables(n_tok):
    n_rows = n_tok // GRID_W
    row_pos = jnp.repeat(jnp.arange(n_rows, dtype=jnp.int32), GRID_W).astype(F32)
    col_pos = jnp.tile(jnp.arange(GRID_W, dtype=jnp.int32), n_rows).astype(F32)
    lane = jnp.arange(HEAD_DIM)

    def table(pair, use_row):
        freqs = ROPE_THETA ** (-jnp.arange(pair, dtype=F32) / pair)
        f = freqs[lane % pair]
        pos = jnp.where(use_row[None, :], row_pos[:, None], col_pos[:, None])
        ang = pos * f[None, :]
        first = (lane % (2 * pair)) < pair
        return jnp.cos(ang), jnp.where(first[None, :], -jnp.sin(ang), jnp.sin(ang))

    c_ax, s_ax = table(32, lane < 64)
    c_df, s_df = table(16, (lane % 64) < 32)
    return c_ax, s_ax, c_df, s_df


def _rope(y, c, s, pair):
    lane = lax.broadcasted_iota(jnp.int32, y.shape, 1)
    first = (lane % (2 * pair)) < pair
    partner = jnp.where(first, pltpu.roll(y, HEAD_DIM - pair, axis=1), pltpu.roll(y, pair, axis=1))
    return y * c + partner * s


def _head_norm(y, g):
    return y * lax.rsqrt(jnp.mean(y * y, axis=-1, keepdims=True) + EPS) * g


def _in_kernel(*refs, latent):
    if latent:
        (x_ref, sh_ref, sc_ref, g_ref, w_ref, axq_ref, axk_ref, ca_ref, sa_ref, cd_ref, sd_ref,
         _z_in, z_ref, h_ref) = refs
        ka_ref = va_ref = kb_ref = vb_ref = kc_ref = vc_ref = None
    else:
        (x_ref, sh_ref, sc_ref, g_ref, w_ref, axq_ref, axk_ref,
         z_ref, ka_ref, va_ref, kb_ref, vb_ref, kc_ref, vc_ref, h_ref) = refs
    j = pl.program_id(1)

    @pl.when(j == 0)
    def _():
        x = x_ref[...]
        y = x * lax.rsqrt(jnp.mean(x * x, axis=-1, keepdims=True) + EPS) * g_ref[...]
        h_ref[...] = (y * (1.0 + sc_ref[...]) + sh_ref[...]).astype(BF16)

    def rope_ax(y):
        return _rope(y, ca_ref[...], sa_ref[...], 32) if latent else y

    def rope_df(y):
        return _rope(y, cd_ref[...], sd_ref[...], 16) if latent else y

    def head(z, k):
        return z[:, k * HEAD_DIM:(k + 1) * HEAD_DIM]

    def put(k, y):
        z_ref[:, k * HEAD_DIM:(k + 1) * HEAD_DIM] = y.astype(BF16)

    def keep(ref, k, y):
        if not latent:
            ref[:, k * HEAD_DIM:(k + 1) * HEAD_DIM] = y

    @pl.when(j == 0)
    def _():
        z = _dot(h_ref[...], w_ref[...])
        for k in range(4):
            put(k, rope_ax(head(z, k)))
        for k in range(2):
            y = head(z, 4 + k)
            keep(ka_ref, k, y)
            put(4 + k, rope_ax(y))
        for k in range(2):
            y = head(z, 6 + k)
            keep(va_ref, k, y)
            put(6 + k, y)

    @pl.when(j == 1)
    def _():
        z = _dot(h_ref[...], w_ref[...])
        for k in range(8):
            put(k, rope_ax(_head_norm(head(z, k), axq_ref[...])))

    @pl.when(j == 2)
    def _():
        z = _dot(h_ref[...], w_ref[...])
        for k in range(2):
            y = _head_norm(head(z, k), axk_ref[...])
            keep(kb_ref, k, y)
            put(k, rope_ax(y))
        for k in range(2):
            y = head(z, 2 + k)
            keep(vb_ref, k, y)
            put(2 + k, y)
        for k in range(4):
            put(4 + k, rope_df(head(z, 4 + k)))

    @pl.when(j == 3)
    def _():
        z = _dot(h_ref[...], w_ref[...])
        for k in range(4):
            y = head(z, k)
            keep(kc_ref, k, y)
            put(k, rope_df(y))
        for k in range(4):
            y = head(z, 4 + k)
            keep(vc_ref, k, y)
            put(4 + k, y)


def _in_call(l, x, mods, norm1_g, w_in, ax_q_g, ax_k_g, tables, z, *, latent, n_ctx, lat_len, tm):
    n, d = x.shape
    depth = w_in.shape[0]
    nb_ctx = n_ctx // tm
    bpl = lat_len // tm
    if latent:
        nblk = (n - n_ctx) // tm
        row = lambda i: i + nb_ctx
        grp = lambda i: 1 + i // bpl
    else:
        nblk = nb_ctx
        row = lambda i: i
        grp = lambda i: 0
    cb = 1024
    in_specs = [
        pl.BlockSpec((tm, d), lambda i, j: (row(i), 0)),
        pl.BlockSpec((None, None, None, 1, d), lambda i, j: (l, grp(i), 0, 0, 0)),
        pl.BlockSpec((None, None, None, 1, d), lambda i, j: (l, grp(i), 1, 0, 0)),
        pl.BlockSpec((None, 1, d), lambda i, j: (l, 0, 0)),
        pl.BlockSpec((None, d, cb), lambda i, j: (l, 0, j)),
        pl.BlockSpec((None, 1, HEAD_DIM), lambda i, j: (l, 0, 0)),
        pl.BlockSpec((None, 1, HEAD_DIM), lambda i, j: (l, 0, 0)),
    ]
    args = [x, mods, mods, norm1_g.reshape(depth, 1, d), w_in,
            ax_q_g.reshape(depth, 1, HEAD_DIM), ax_k_g.reshape(depth, 1, HEAD_DIM)]
    z_spec = pl.BlockSpec((tm, cb), lambda i, j: (row(i), j))
    z_shape = jax.ShapeDtypeStruct((n, IN_COLS), BF16)
    scratch = [pltpu.VMEM((tm, d), BF16)]
    if latent:
        tspec = pl.BlockSpec((tm, HEAD_DIM), lambda i, j: (i % bpl, 0))
        in_specs += [tspec] * 4 + [pl.BlockSpec(memory_space=pl.ANY)]
        args += list(tables) + [z]
        return pl.pallas_call(
            functools.partial(_in_kernel, latent=True),
            out_shape=z_shape, grid=(nblk, 4), in_specs=in_specs, out_specs=z_spec,
            scratch_shapes=scratch, input_output_aliases={len(args) - 1: 0},
            compiler_params=_cparams(("parallel", "arbitrary")), name="in_proj_latent",
        )(*args)
    kv2 = jax.ShapeDtypeStruct((n_ctx, 2 * HEAD_DIM), F32)
    kv4 = jax.ShapeDtypeStruct((n_ctx, 4 * HEAD_DIM), F32)
    s2 = pl.BlockSpec((tm, 2 * HEAD_DIM), lambda i, j: (i, 0))
    s4 = pl.BlockSpec((tm, 4 * HEAD_DIM), lambda i, j: (i, 0))
    return pl.pallas_call(
        functools.partial(_in_kernel, latent=False),
        out_shape=(z_shape, kv2, kv2, kv2, kv2, kv4, kv4), grid=(nblk, 4), in_specs=in_specs,
        out_specs=(z_spec, s2, s2, s2, s2, s4, s4), scratch_shapes=scratch,
        compiler_params=_cparams(("parallel", "arbitrary")), name="in_proj_context",
    )(*args)


def _gqa_kernel(*refs, groups, tq, window, has_cache, has_sink, l, lat_len):
    refs = list(refs)
    sink_ref = refs.pop(0) if has_sink else None
    q_ref, k_ref, v_ref = refs[:3]
    refs = refs[3:]
    if has_cache:
        ck_ref, cv_ref = refs[:2]
        refs = refs[2:]
    o_ref = refs[-1]
    kv = pl.program_id(1)
    qi = pl.program_id(2)
    scale = HEAD_DIM ** -0.5

    if window:
        span = tq + 2 * WINDOW
        start = jnp.clip(qi * tq - WINDOW, 0, lat_len - span)
        start = pl.multiple_of(start, WINDOW)
        k = k_ref[pl.ds(start, span), :]
        v = v_ref[pl.ds(start, span), :]
        qpos = qi * tq + lax.broadcasted_iota(jnp.int32, (tq, span), 0)
        kpos = start + lax.broadcasted_iota(jnp.int32, (tq, span), 1)
        ok = jnp.abs(qpos - kpos) <= WINDOW
    else:
        k = k_ref[...]
        v = v_ref[...]
    if has_cache:
        ck = ck_ref[...].astype(BF16)
        cv = cv_ref[...].astype(BF16)

    c2 = scale * LOG2E
    for g in range(groups):
        q = q_ref[:, g * HEAD_DIM:(g + 1) * HEAD_DIM]
        s = _dot_nt(q, k)
        if window:
            s = jnp.where(ok, s, NEG_INF)
        m = jnp.max(s, axis=-1, keepdims=True)
        if has_cache:
            s2 = _dot_nt(q, ck)
            m = jnp.maximum(m, jnp.max(s2, axis=-1, keepdims=True))
        if has_sink:
            sink = sink_ref[l, kv * groups + g] * (1.0 / scale)
            m = jnp.maximum(m, sink)
        e = jnp.exp2((s - m) * c2)
        den = jnp.sum(e, axis=-1, keepdims=True)
        o = _dot(e.astype(BF16), v)
        if has_cache:
            e2 = jnp.exp2((s2 - m) * c2)
            den = den + jnp.sum(e2, axis=-1, keepdims=True)
            o = o + _dot(e2.astype(BF16), cv)
        if has_sink:
            den = den + jnp.exp2((sink - m) * c2)
        o_ref[:, g * HEAD_DIM:(g + 1) * HEAD_DIM] = (o / den).astype(BF16)


def _gqa_call(l, z, o, *, qcol, kcol, vcol, ocol, heads, kvh, nbatch, seq, row0, tq,
              window=False, cache=None, sink=None):
    groups = heads // kvh
    gw = groups * HEAD_DIM
    nq = seq // tq
    qrow = lambda b, qi: row0 // tq + b * nq + qi
    krow = lambda b: row0 // seq + b
    in_specs, args = [], []
    if sink is not None:
        in_specs.append(pl.BlockSpec(memory_space=pltpu.SMEM))
        args.append(sink)
    in_specs += [
        pl.BlockSpec((tq, gw), lambda b, kv, qi: (qrow(b, qi), qcol // gw + kv)),
        pl.BlockSpec((seq, HEAD_DIM), lambda b, kv, qi: (krow(b), kcol // HEAD_DIM + kv)),
        pl.BlockSpec((seq, HEAD_DIM), lambda b, kv, qi: (krow(b), vcol // HEAD_DIM + kv)),
    ]
    args += [z, z, z]
    if cache is not None:
        past = cache[0].shape[2]
        cspec = pl.BlockSpec((None, None, past, HEAD_DIM), lambda b, kv, qi: (b, l, 0, kv))
        in_specs += [cspec, cspec]
        args += list(cache)
    in_specs.append(pl.BlockSpec(memory_space=pl.ANY))
    args.append(o)
    kern = functools.partial(_gqa_kernel, groups=groups, tq=tq, window=window,
                             has_cache=cache is not None, has_sink=sink is not None, l=l, lat_len=seq)
    return pl.pallas_call(
        kern, out_shape=jax.ShapeDtypeStruct(o.shape, o.dtype), grid=(nbatch, kvh, nq),
        in_specs=in_specs,
        out_specs=pl.BlockSpec((tq, gw), lambda b, kv, qi: (qrow(b, qi), ocol // gw + kv)),
        input_output_aliases={len(args) - 1: 0},
        compiler_params=_cparams(("parallel", "parallel", "arbitrary")),
        name="attn_window" if sink is not None else "attn_axial",
    )(*args)


def _diff_kernel(*refs, has_cache, lam_init):
    refs = list(refs)
    lq1, lk1, lq2, lk2, gain_ref, q_ref, k_ref, v_ref = refs[:8]
    refs = refs[8:]
    if has_cache:
        ck_ref, cv_ref = refs[:2]
    o_ref = refs[-1]
    h = pl.program_id(1)
    half = HEAD_DIM // 2
    scale = half ** -0.5

    lam = (jnp.exp(jnp.sum(lq1[...] * lk1[...], axis=-1, keepdims=True))
           - jnp.exp(jnp.sum(lq2[...] * lk2[...], axis=-1, keepdims=True)) + lam_init)

    q = q_ref[...]
    k = k_ref[...]
    v = v_ref[...]
    lane = lax.broadcasted_iota(jnp.int32, (1, HEAD_DIM), 1)
    lo = jnp.where(lane < half, 1.0, 0.0).astype(BF16)
    qs = (q * lo, q * (1.0 - lo))
    if has_cache:
        ck = ck_ref[...].astype(BF16)
        cv = cv_ref[...].astype(BF16)

    probs = []
    c2 = scale * LOG2E
    for c in range(2):
        s = _dot_nt(qs[c], k)
        m = jnp.max(s, axis=-1, keepdims=True)
        if has_cache:
            s2 = _dot_nt(qs[c], ck)
            m = jnp.maximum(m, jnp.max(s2, axis=-1, keepdims=True))
        e = jnp.exp2((s - m) * c2)
        den = jnp.sum(e, axis=-1, keepdims=True)
        if has_cache:
            e2 = jnp.exp2((s2 - m) * c2)
            den = den + jnp.sum(e2, axis=-1, keepdims=True)
        else:
            e2 = None
        probs.append((e, e2, 1.0 / den))
    (e_a, e2_a, r_a), (e_b, e2_b, r_b) = probs
    r_b = lam * r_b
    o = _dot((e_a * r_a - e_b * r_b).astype(BF16), v)
    if has_cache:
        o = o + _dot((e2_a * r_a - e2_b * r_b).astype(BF16), cv)
    gain = gain_ref[pl.ds(h, 1), :]
    o = o * lax.rsqrt(jnp.mean(o * o, axis=-1, keepdims=True) + EPS) * gain
    o_ref[...] = (o * (1.0 - lam_init)).astype(BF16)


def _diff_call(l, z, o, lam_params, gain, *, nbatch, seq, row0, tq, lam_init, cache=None):
    nq = seq // tq
    depth = gain.shape[0]
    half = HEAD_DIM // 2
    qrow = lambda b, qi: row0 // tq + b * nq + qi
    krow = lambda b: row0 // seq + b
    lspec = pl.BlockSpec((None, 1, half), lambda b, h, qi: (l, 0, 0))
    in_specs = [lspec] * 4 + [
        pl.BlockSpec((None, DIFF_HEADS, HEAD_DIM), lambda b, h, qi: (l, 0, 0)),
        pl.BlockSpec((tq, HEAD_DIM), lambda b, h, qi: (qrow(b, qi), QC // HEAD_DIM + h)),
        pl.BlockSpec((seq, HEAD_DIM), lambda b, h, qi: (krow(b), KC // HEAD_DIM + h)),
        pl.BlockSpec((seq, HEAD_DIM), lambda b, h, qi: (krow(b), VC // HEAD_DIM + h)),
    ]
    args = [p.reshape(depth, 1, half) for p in lam_params] + [gain, z, z, z]
    if cache is not None:
        past = cache[0].shape[2]
        cspec = pl.BlockSpec((None, None, past, HEAD_DIM), lambda b, h, qi: (b, l, 0, h))
        in_specs += [cspec, cspec]
        args += list(cache)
    in_specs.append(pl.BlockSpec(memory_space=pl.ANY))
    args.append(o)
    ocol0 = (WIN_HEADS + AX_HEADS)
    return pl.pallas_call(
        functools.partial(_diff_kernel, has_cache=cache is not None, lam_init=lam_init),
        out_shape=jax.ShapeDtypeStruct(o.shape, o.dtype), grid=(nbatch, DIFF_HEADS, nq),
        in_specs=in_specs,
        out_specs=pl.BlockSpec((tq, HEAD_DIM), lambda b, h, qi: (qrow(b, qi), ocol0 + h)),
        input_output_aliases={len(args) - 1: 0},
        compiler_params=_cparams(("parallel", "parallel", "arbitrary")),
        name="attn_diff",
    )(*args)


def _out_kernel(o_ref, x_ref, w_ref, g1_ref, sh2_ref, sc2_ref, n2g_ref, x1_ref, h2_ref):
    a = _dot(o_ref[...], w_ref[...])
    x1 = x_ref[...] + g1_ref[...] * a
    x1_ref[...] = x1
    y = x1 * lax.rsqrt(jnp.mean(x1 * x1, axis=-1, keepdims=True) + EPS) * n2g_ref[...]
    h2_ref[...] = (y * (1.0 + sc2_ref[...]) + sh2_ref[...]).astype(BF16)


def _group_of_block(i, nb_ctx, bpl):
    return jnp.where(i < nb_ctx, 0, 1 + (i - nb_ctx) // bpl)


def _out_call(l, o, x, w_out, mods, norm2_g, *, n_ctx, lat_len, tm):
    n, d = x.shape
    depth = w_out.shape[0]
    nb_ctx, bpl = n_ctx // tm, lat_len // tm
    grp = lambda i: _group_of_block(i, nb_ctx, bpl)
    mspec = lambda which: pl.BlockSpec((None, None, None, 1, d), lambda i: (l, grp(i), which, 0, 0))
    return pl.pallas_call(
        _out_kernel,
        out_shape=(jax.ShapeDtypeStruct((n, d), F32), jax.ShapeDtypeStruct((n, d), BF16)),
        grid=(n // tm,),
        in_specs=[
            pl.BlockSpec((tm, d), lambda i: (i, 0)),
            pl.BlockSpec((tm, d), lambda i: (i, 0)),
            pl.BlockSpec((None, d, d), lambda i: (l, 0, 0)),
            mspec(2), mspec(3), mspec(4),
            pl.BlockSpec((None, 1, d), lambda i: (l, 0, 0)),
        ],
        out_specs=(pl.BlockSpec((tm, d), lambda i: (i, 0)), pl.BlockSpec((tm, d), lambda i: (i, 0))),
        compiler_params=_cparams(("parallel",)),
        name="out_proj",
    )(o, x, w_out, mods, mods, mods, norm2_g.reshape(depth, 1, d))


def _score_kernel(kw_ref, h2_ref, o_ref):
    o_ref[...] = _dot_nt(kw_ref[...], h2_ref[...])


def _score_call(l, kw, h2, *, tm):
    n, d = h2.shape
    rows = kw.shape[1]
    return pl.pallas_call(
        _score_kernel,
        out_shape=jax.ShapeDtypeStruct((rows, n), F32),
        grid=(n // tm,),
        in_specs=[pl.BlockSpec((None, rows, d), lambda i: (l, 0, 0)),
                  pl.BlockSpec((tm, d), lambda i: (i, 0))],
        out_specs=pl.BlockSpec((rows, tm), lambda i: (0, i)),
        compiler_params=_cparams(("parallel",)),
        name="peer_scores",
    )(kw, h2)


def _sort_pairs(n):
    pairs = []

    def merge(lo, hi, r):
        step = r * 2
        if step < hi - lo:
            merge(lo, hi, step)
            merge(lo + r, hi, step)
            for i in range(lo + r, hi - r, step):
                pairs.append((i, i + r))
        else:
            pairs.append((lo, lo + r))

    def sort(lo, hi):
        if hi - lo >= 1:
            mid = lo + (hi - lo) // 2
            sort(lo, mid)
            sort(mid + 1, hi)
            merge(lo, hi, 1)

    sort(0, n - 1)
    return pairs


_SORT16 = _sort_pairs(PEER_TOPK)


def _vmax(a, b):
    if a is None:
        return b
    if b is None:
        return a
    return jnp.maximum(a, b)


def _vmin(a, b):
    if a is None or b is None:
        return None
    return jnp.minimum(a, b)


def _bitonic_merge_desc(x):
    x = list(x)
    d = PEER_TOPK // 2
    while d >= 1:
        for i in range(PEER_TOPK):
            if (i & d) == 0:
                a, b = x[i], x[i + d]
                x[i], x[i + d] = _vmax(a, b), _vmin(a, b)
        d //= 2
    return x


def _merge_top16(a, b):
    a = list(a) + [None] * (PEER_TOPK - len(a))
    b = list(b) + [None] * (PEER_TOPK - len(b))
    return _bitonic_merge_desc([_vmax(a[k], b[PEER_TOPK - 1 - k]) for k in range(PEER_TOPK)])


def _router_kernel(s_ref, r1_ref, e1_ref, cnt_ref, e0_ref, top_ref):
    nk = PEER_KEYS
    nv = nk // 8
    tk = PEER_TOPK
    sub = lax.broadcasted_iota(jnp.int32, (8, s_ref.shape[1]), 0)

    for g in range(2 * PEER_HEADS):
        x = [s_ref[g * nk + 8 * r:g * nk + 8 * r + 8, :] for r in range(nv)]
        for (i, j) in _SORT16:
            x[i], x[j] = jnp.maximum(x[i], x[j]), jnp.minimum(x[i], x[j])
        for shift in (4, 2, 1):
            y = [pltpu.roll(v, shift, axis=0) for v in x]
            x = _bitonic_merge_desc([jnp.maximum(x[k], y[tk - 1 - k]) for k in range(tk)])
        for a in range(tk):
            top_ref[g * tk + a] = x[a]

    def packed(c, a):
        out = top_ref[c * tk + a]
        for h in range(1, PEER_HEADS):
            out = jnp.where(sub == h, top_ref[(2 * h + c) * tk + a], out)
        return out

    p0 = [packed(0, a) for a in range(tk)]
    p1 = [packed(1, b) for b in range(tk)]
    rows = [[p0[a] + p1[b] for b in range(tk // (a + 1))] for a in range(8)]
    col0 = [p0[a] + p1[0] for a in range(8, tk)]
    t1 = _merge_top16(rows[0], _merge_top16(rows[1], col0))
    t2 = _merge_top16(_merge_top16(rows[2], rows[3]), _merge_top16(rows[4], rows[5]))
    t3 = _merge_top16(rows[6], rows[7])
    best = _merge_top16(t1, _merge_top16(t2, t3))
    tau_p = best[tk - 1]
    zsum = jnp.ones_like(tau_p)
    for k in range(1, tk):
        zsum = zsum + jnp.exp(best[k] - best[0])
    zinv_p = 1.0 / zsum

    cnt_p = []
    for a in range(tk):
        acc = jnp.zeros_like(tau_p)
        for b in range(tk):
            acc = acc + jnp.where(p0[a] + p1[b] >= tau_p, 1.0, 0.0)
        cnt_p.append(acc)

    def pair_words(v):
        bits = pltpu.bitcast(v, jnp.uint32)
        return bits | lax.shift_right_logical(bits, jnp.uint32(16))

    for h in range(PEER_HEADS):
        shape = (8, s_ref.shape[1])
        zinv = jnp.broadcast_to(zinv_p[h:h + 1, :], shape)
        cnt_a = [jnp.broadcast_to(cnt_p[a][h:h + 1, :], shape) for a in range(tk)]
        t0 = [top_ref[(2 * h) * tk + a] for a in range(tk)]
        t1h = [top_ref[(2 * h + 1) * tk + b] for b in range(tk)]
        for rr in range(nv // 2):
            ranks, gates = [], []
            for r in (2 * rr, 2 * rr + 1):
                s0 = s_ref[(2 * h) * nk + 8 * r:(2 * h) * nk + 8 * r + 8, :]
                s1 = s_ref[(2 * h + 1) * nk + 8 * r:(2 * h + 1) * nk + 8 * r + 8, :]
                rank = jnp.full(shape, float(tk), F32)
                cnt = jnp.zeros(shape, F32)
                for b in range(tk - 1, -1, -1):
                    rank = jnp.where(s1 >= t1h[b], float(b), rank)
                    cnt = jnp.where(s0 >= t0[b], cnt_a[b], cnt)
                ranks.append(rank)
                gates.append(jnp.exp(s1 - t1h[0]))
                e0 = (jnp.exp(s0 - t0[0]) * zinv).astype(BF16).astype(F32)
                cnt_ref[h, 8 * r:8 * r + 8, :] = pair_words(cnt)
                e0_ref[h, 8 * r:8 * r + 8, :] = pair_words(e0)
            r1_ref[h, 16 * rr:16 * rr + 16, :] = jnp.concatenate(ranks, axis=0).astype(BF16)
            e1_ref[h, 16 * rr:16 * rr + 16, :] = jnp.concatenate(gates, axis=0).astype(BF16)


def _router_call(s_t):
    rows, n = s_t.shape
    tl = 128
    half = jax.ShapeDtypeStruct((PEER_HEADS, PEER_KEYS, n), BF16)
    words = jax.ShapeDtypeStruct((PEER_HEADS, PEER_KEYS, n), jnp.uint32)
    spec = pl.BlockSpec((PEER_HEADS, PEER_KEYS, tl), lambda i: (0, 0, i))
    return pl.pallas_call(
        _router_kernel,
        out_shape=(half, half, words, words),
        grid=(n // tl,),
        in_specs=[pl.BlockSpec((rows, tl), lambda i: (0, i))],
        out_specs=(spec, spec, spec, spec),
        scratch_shapes=[pltpu.VMEM((2 * PEER_HEADS * PEER_TOPK, 8, tl), F32)],
        compiler_params=_cparams(("parallel",)),
        name="peer_router",
    )(s_t)


def _gelu(a):
    c0 = math.sqrt(2.0 / math.pi)
    half = 0.5 * a
    return half + half * jnp.tanh(a * (c0 + (c0 * 0.044715) * (a * a)))


PEER_TIE_STRIDE = 6


def _peer_kernel(*refs, final, ib):
    if final:
        (h2_ref, u_ref, vt_ref, r1_ref, e1_ref, cnt_ref, e0_ref, x1_ref, g2_ref, fg_ref,
         o_ref, acc_ref, w_ref, cs_ref, es_ref) = refs
    else:
        (h2_ref, u_ref, vt_ref, r1_ref, e1_ref, cnt_ref, e0_ref, x1_ref, g2_ref,
         o_ref, acc_ref, w_ref, cs_ref, es_ref) = refs
    e = pl.program_id(1)
    t = h2_ref.shape[0]
    nchunk = PEER_KEYS // 16

    @pl.when(e == 0)
    def _():
        acc_ref[...] = jnp.zeros_like(acc_ref)

    lhs = []
    for ii in range(ib):
        for h in range(PEER_HEADS):
            slot = (ii % 2) * PEER_HEADS + h
            cs_ref[slot] = pltpu.bitcast(jnp.broadcast_to(cnt_ref[h, ii:ii + 1, :], (8, t)), BF16)
            es_ref[slot] = pltpu.bitcast(jnp.broadcast_to(e0_ref[h, ii:ii + 1, :], (8, t)), BF16)
        w = [None] * nchunk
        for h in range(PEER_HEADS):
            cnt = cs_ref[(ii % 2) * PEER_HEADS + h]
            e0 = es_ref[(ii % 2) * PEER_HEADS + h]
            for jc in range(nchunk):
                r1 = r1_ref[h, jc * 16:(jc + 1) * 16, :]
                e1 = e1_ref[h, jc * 16:(jc + 1) * 16, :]
                gate = jnp.where(r1 < cnt, e0 * e1, jnp.zeros_like(e1))
                w[jc] = gate if w[jc] is None else w[jc] + gate
        for jc in range(nchunk):
            r0 = ii * PEER_KEYS + jc * 16
            w_ref[r0:r0 + 16, :] = w[jc]
            bits = pltpu.bitcast(w[jc][:, 0:128], jnp.uint32)
            for lt in range(1, t // 128):
                bits = bits | pltpu.bitcast(w[jc][:, lt * 128:(lt + 1) * 128], jnp.uint32)
            zero = lax.shift_right_logical(lax.shift_right_logical(bits, jnp.uint32(16)), jnp.uint32(16))
            zero = pltpu.bitcast(zero, BF16)
            slot = (ii * nchunk + jc) * PEER_TIE_STRIDE
            rows = (slot % 64) * 16
            c0 = (slot // 64) * 256
            piece = u_ref[rows:rows + 16, c0:c0 + 256]
            lhs.append((rows, c0, piece + jnp.concatenate([zero, zero], axis=1)))
    tied = {(rows, c0): v for rows, c0, v in lhs}
    row_groups = []
    for rg in range(ib * PEER_KEYS // 16):
        rows = rg * 16
        cols = []
        for c in range(u_ref.shape[1] // 256):
            key = (rows, c * 256)
            cols.append(tied[key] if key in tied else u_ref[rows:rows + 16, c * 256:(c + 1) * 256])
        row_groups.append(jnp.concatenate(cols, axis=1))
    a_t = _dot_nt(jnp.concatenate(row_groups, axis=0), h2_ref[...])
    p = [w_ref[ch * 16:(ch + 1) * 16, :] * _gelu(a_t[ch * 16:(ch + 1) * 16, :]).astype(BF16)
         for ch in range(ib * nchunk)]
    acc_ref[...] += _dot(vt_ref[...], jnp.concatenate(p, axis=0))

    @pl.when(e == pl.num_programs(1) - 1)
    def _():
        x2 = x1_ref[...] + g2_ref[...] * acc_ref[...].T
        if final:
            x2 = x2 * lax.rsqrt(jnp.mean(x2 * x2, axis=-1, keepdims=True) + EPS) * fg_ref[...]
        o_ref[...] = x2


def _peer_call(l, h2, u_bf, vt_bf, r1, e1, cnt, e0, x1, mods, final_g, *, n_ctx, lat_len, tm, final):
    n, d = x1.shape
    ib = 8
    eb = ib * PEER_KEYS
    n_exp = u_bf.shape[1]
    nb_ctx, bpl = n_ctx // tm, lat_len // tm
    grp = lambda i: _group_of_block(i, nb_ctx, bpl)
    once = dict(pipeline_mode=pl.Buffered(1))
    in_specs = [
        pl.BlockSpec((tm, d), lambda i, e: (i, 0), **once),
        pl.BlockSpec((None, eb, d), lambda i, e: (l, e, 0)),
        pl.BlockSpec((None, d, eb), lambda i, e: (l, 0, e)),
        pl.BlockSpec((PEER_HEADS, PEER_KEYS, tm), lambda i, e: (0, 0, i), **once),
        pl.BlockSpec((PEER_HEADS, PEER_KEYS, tm), lambda i, e: (0, 0, i), **once),
        pl.BlockSpec((PEER_HEADS, ib, tm), lambda i, e: (0, e, i)),
        pl.BlockSpec((PEER_HEADS, ib, tm), lambda i, e: (0, e, i)),
        pl.BlockSpec((tm, d), lambda i, e: (i, 0), **once),
        pl.BlockSpec((None, None, None, 1, d), lambda i, e: (l, grp(i), 5, 0, 0)),
    ]
    args = [h2, u_bf, vt_bf, r1, e1, cnt, e0, x1, mods]
    if final:
        in_specs.append(pl.BlockSpec((1, d), lambda i, e: (0, 0)))
        args.append(final_g.reshape(1, d))
    nstage = 2 * PEER_HEADS
    return pl.pallas_call(
        functools.partial(_peer_kernel, final=final, ib=ib),
        out_shape=jax.ShapeDtypeStruct((n, d), F32),
        grid=(n // tm, n_exp // eb),
        in_specs=in_specs,
        out_specs=pl.BlockSpec((tm, d), lambda i, e: (i, 0)),
        scratch_shapes=[pltpu.VMEM((d, tm), F32), pltpu.VMEM((eb, tm), BF16),
                        pltpu.VMEM((nstage, 16, tm), BF16), pltpu.VMEM((nstage, 16, tm), BF16)],
        compiler_params=_cparams(("parallel", "arbitrary")),
        name="peer_experts",
    )(*args)


def kernel(x_prompt, x_sample, cache_k_win, cache_v_win, cache_k_axial, cache_v_axial, cache_k_diff,
           cache_v_diff, c, c_ctx, norm1_g, w_ada, b_ada, w_in, win_sink, ax_q_g, ax_k_g, lam_q1, lam_k1,
           lam_q2, lam_k2, diff_subln_g, w_out, norm2_g, peer_wq, peer_keys, peer_u, peer_v, final_g):
    batch, seq, d = x_prompt.shape
    dec_batch, lat_len, _ = x_sample.shape
    depth = w_in.shape[0]
    past = cache_k_win.shape[2]
    n_ctx = batch * seq
    n_lat = dec_batch * lat_len
    n = n_ctx + n_lat
    tm = 512
    tq_ctx = seq
    tq_lat = 256

    x = jnp.concatenate([x_prompt.reshape(n_ctx, d), x_sample.reshape(n_lat, d)], axis=0)
    cond = jnp.zeros((8, d), F32).at[0].set(c_ctx).at[1:1 + dec_batch].set(c)
    mods = _ada_call(cond, w_ada, b_ada).reshape(depth, 8, 6, 1, d)
    kw = _kw_call(peer_keys, peer_wq)
    w_in_bf = w_in.astype(BF16)
    w_out_bf = w_out.astype(BF16)
    u_bf = peer_u.astype(BF16)
    vt_bf = _vt_call(peer_v)
    tables = _rope_tables(lat_len)

    cache_a = (cache_k_win.reshape(dec_batch, depth, past, -1), cache_v_win.reshape(dec_batch, depth, past, -1))
    cache_b = (cache_k_axial.reshape(dec_batch, depth, past, -1), cache_v_axial.reshape(dec_batch, depth, past, -1))
    cache_c = (cache_k_diff.reshape(dec_batch, depth, past, -1), cache_v_diff.reshape(dec_batch, depth, past, -1))
    lam_params = (lam_q1, lam_k1, lam_q2, lam_k2)

    o = jnp.zeros((n, d), BF16)
    kv_out = [[] for _ in range(6)]
    for l in range(depth):
        lam_init = 0.8 - 0.6 * math.exp(-0.3 * l)
        geo = dict(n_ctx=n_ctx, lat_len=lat_len, tm=tm)
        res = _in_call(l, x, mods, norm1_g, w_in_bf, ax_q_g, ax_k_g, None, None, latent=False, **geo)
        z = res[0]
        for k in range(6):
            kv_out[k].append(res[1 + k])
        z = _in_call(l, x, mods, norm1_g, w_in_bf, ax_q_g, ax_k_g, tables, z, latent=True, **geo)

        ctx = dict(nbatch=batch, seq=seq, row0=0, tq=tq_ctx)
        lat = dict(nbatch=dec_batch, seq=lat_len, row0=n_ctx, tq=tq_lat)
        win = dict(qcol=QA, kcol=KA, vcol=VA, ocol=0, heads=WIN_HEADS, kvh=WIN_KV, sink=win_sink)
        axl = dict(qcol=QB, kcol=KB, vcol=VB, ocol=WIN_HEADS * HEAD_DIM, heads=AX_HEADS, kvh=AX_KV)
        o = _gqa_call(l, z, o, **win, **ctx)
        o = _gqa_call(l, z, o, **axl, **ctx)
        o = _diff_call(l, z, o, lam_params, diff_subln_g, lam_init=lam_init, **ctx)
        o = _gqa_call(l, z, o, **win, **lat, window=True, cache=cache_a)
        o = _gqa_call(l, z, o, **axl, **lat, cache=cache_b)
        o = _diff_call(l, z, o, lam_params, diff_subln_g, lam_init=lam_init, cache=cache_c, **lat)

        x1, h2 = _out_call(l, o, x, w_out_bf, mods, norm2_g, **geo)
        s_t = _score_call(l, kw, h2, tm=tm)
        r1, e1, cnt, e0 = _router_call(s_t)
        x = _peer_call(l, h2, u_bf, vt_bf, r1, e1, cnt, e0, x1, mods, final_g,
                       final=(l == depth - 1), **geo)

    y_prompt = x[:n_ctx].reshape(batch, seq, d)
    y_sample = x[n_ctx:].reshape(dec_batch, lat_len, d)
    outs = []
    for k, nh in enumerate((WIN_KV, WIN_KV, AX_KV, AX_KV, DIFF_HEADS, DIFF_HEADS)):
        outs.append(jnp.stack([a.reshape(batch, seq, nh, HEAD_DIM) for a in kv_out[k]], axis=1))
    return (y_prompt, y_sample, *outs)
```

```python
import functools
import math

import jax
import jax.numpy as jnp
from jax import lax
from jax.experimental import pallas as pl
from jax.experimental.pallas import tpu as pltpu

F32 = jnp.float32
BF16 = jnp.bfloat16

HEAD_DIM = 128
GRID_W = 64
WINDOW = 128
ROPE_THETA = 10000.0
PEER_HEADS = 8
PEER_KEYS = 128
PEER_TOPK = 16
EPS = 1e-6
NEG_INF = -1e30
LOG2E = 1.4426950408889634

QA, KA, VA, QB, KB, VB, QC, KC, VC = 0, 512, 768, 1024, 2048, 2304, 2560, 3072, 3584
IN_COLS = 4096
WIN_HEADS, WIN_KV, AX_HEADS, AX_KV, DIFF_HEADS = 4, 2, 8, 2, 4

VMEM_LIMIT = 56 * 1024 * 1024


def _cparams(sem, **kw):
    return pltpu.CompilerParams(dimension_semantics=sem, vmem_limit_bytes=VMEM_LIMIT, **kw)


def _dot_nt(a, b):
    return lax.dot_general(a, b, (((1,), (1,)), ((), ())), preferred_element_type=F32)


def _dot(a, b):
    return jnp.dot(a, b, preferred_element_type=F32)


def _ada_kernel(cond_ref, w_ref, b_ref, o_ref):
    c = cond_ref[...]
    s = (c * jax.nn.sigmoid(c)).astype(BF16)
    o_ref[...] = _dot(s, w_ref[...].astype(BF16)) + b_ref[...]


def _ada_call(cond, w_ada, b_ada):
    depth, d, n6 = w_ada.shape
    tn = 1024
    return pl.pallas_call(
        _ada_kernel,
        out_shape=jax.ShapeDtypeStruct((depth, 8, n6), F32),
        grid=(depth, n6 // tn),
        in_specs=[
            pl.BlockSpec((8, d), lambda l, j: (0, 0)),
            pl.BlockSpec((None, d, tn), lambda l, j: (l, 0, j)),
            pl.BlockSpec((None, 1, tn), lambda l, j: (l, 0, j)),
        ],
        out_specs=pl.BlockSpec((None, 8, tn), lambda l, j: (l, 0, j)),
        compiler_params=_cparams(("parallel", "parallel")),
        name="ada",
    )(cond, w_ada, b_ada.reshape(depth, 1, n6))


def _kw_kernel(keys_ref, wq_ref, o_ref):
    kw = lax.dot_general(keys_ref[...], wq_ref[...], (((1,), (1,)), ((), ())),
                         preferred_element_type=F32, precision=lax.Precision.HIGHEST)
    o_ref[...] = kw.astype(BF16)


def _kw_call(peer_keys, peer_wq):
    depth, d, nq = peer_wq.shape
    ngrp = PEER_HEADS * 2
    sub = PEER_KEYS
    return pl.pallas_call(
        _kw_kernel,
        out_shape=jax.ShapeDtypeStruct((depth, ngrp * sub, d), BF16),
        grid=(depth, ngrp),
        in_specs=[
            pl.BlockSpec((None, None, None, sub, sub), lambda l, g: (l, g // 2, g % 2, 0, 0)),
            pl.BlockSpec((None, d, sub), lambda l, g: (l, 0, g)),
        ],
        out_specs=pl.BlockSpec((None, sub, d), lambda l, g: (l, g, 0)),
        compiler_params=_cparams(("parallel", "parallel")),
        name="peer_kw",
    )(peer_keys, peer_wq)


def _vt_kernel(v_ref, o_ref):
    o_ref[...] = v_ref[...].T.astype(BF16)


def _vt_call(peer_v):
    depth, n_exp, d = peer_v.shape
    te = 512
    return pl.pallas_call(
        _vt_kernel,
        out_shape=jax.ShapeDtypeStruct((depth, d, n_exp), BF16),
        grid=(depth, n_exp // te),
        in_specs=[pl.BlockSpec((None, te, d), lambda l, j: (l, j, 0))],
        out_specs=pl.BlockSpec((None, d, te), lambda l, j: (l, 0, j)),
        compiler_params=_cparams(("parallel", "parallel")),
        name="peer_vt",
    )(peer_v)


def _rope_tables(n_tok):
    n_rows = n_tok // GRID_W
    row_pos = jnp.repeat(jnp.arange(n_rows, dtype=jnp.int32), GRID_W).astype(F32)
    col_pos = jnp.tile(jnp.arange(GRID_W, dtype=jnp.int32), n_rows).astype(F32)
    lane = jnp.arange(HEAD_DIM)

    def table(pair, use_row):
        freqs = ROPE_THETA ** (-jnp.arange(pair, dtype=F32) / pair)
        f = freqs[lane % pair]
        pos = jnp.where(use_row[None, :], row_pos[:, None], col_pos[:, None])
        ang = pos * f[None, :]
        first = (lane % (2 * pair)) < pair
        return jnp.cos(ang), jnp.where(first[None, :], -jnp.sin(ang), jnp.sin(ang))

    c_ax, s_ax = table(32, lane < 64)
    c_df, s_df = table(16, (lane % 64) < 32)
    return c_ax, s_ax, c_df, s_df


def _rope(y, c, s, pair):
    lane = lax.broadcasted_iota(jnp.int32, y.shape, 1)
    first = (lane % (2 * pair)) < pair
    partner = jnp.where(first, pltpu.roll(y, HEAD_DIM - pair, axis=1), pltpu.roll(y, pair, axis=1))
    return y * c + partner * s


def _head_norm(y, g):
    return y * lax.rsqrt(jnp.mean(y * y, axis=-1, keepdims=True) + EPS) * g


def _in_kernel(*refs, latent):
    if latent:
        (x_ref, sh_ref, sc_ref, g_ref, w_ref, axq_ref, axk_ref, ca_ref, sa_ref, cd_ref, sd_ref,
         _z_in, z_ref, h_ref) = refs
        ka_ref = va_ref = kb_ref = vb_ref = kc_ref = vc_ref = None
    else:
        (x_ref, sh_ref, sc_ref, g_ref, w_ref, axq_ref, axk_ref, _z_in,
         z_ref, ka_ref, va_ref, kb_ref, vb_ref, kc_ref, vc_ref, h_ref) = refs
    j = pl.program_id(1)

    @pl.when(j == 0)
    def _():
        x = x_ref[...]
        y = x * lax.rsqrt(jnp.mean(x * x, axis=-1, keepdims=True) + EPS) * g_ref[...]
        h_ref[...] = (y * (1.0 + sc_ref[...]) + sh_ref[...]).astype(BF16)

    def rope_ax(y):
        return _rope(y, ca_ref[...], sa_ref[...], 32) if latent else y

    def rope_df(y):
        return _rope(y, cd_ref[...], sd_ref[...], 16) if latent else y

    def head(z, k):
        return z[:, k * HEAD_DIM:(k + 1) * HEAD_DIM]

    def put(k, y):
        z_ref[:, k * HEAD_DIM:(k + 1) * HEAD_DIM] = y.astype(BF16)

    def keep(ref, k, y):
        if not latent:
            ref[:, k * HEAD_DIM:(k + 1) * HEAD_DIM] = y

    @pl.when(j == 0)
    def _():
        z = _dot(h_ref[...], w_ref[...])
        for k in range(4):
            put(k, rope_ax(head(z, k)))
        for k in range(2):
            y = head(z, 4 + k)
            keep(ka_ref, k, y)
            put(4 + k, rope_ax(y))
        for k in range(2):
            y = head(z, 6 + k)
            keep(va_ref, k, y)
            put(6 + k, y)

    @pl.when(j == 1)
    def _():
        z = _dot(h_ref[...], w_ref[...])
        for k in range(8):
            put(k, rope_ax(_head_norm(head(z, k), axq_ref[...])))

    @pl.when(j == 2)
    def _():
        z = _dot(h_ref[...], w_ref[...])
        for k in range(2):
            y = _head_norm(head(z, k), axk_ref[...])
            keep(kb_ref, k, y)
            put(k, rope_ax(y))
        for k in range(2):
            y = head(z, 2 + k)
            keep(vb_ref, k, y)
            put(2 + k, y)
        for k in range(4):
            put(4 + k, rope_df(head(z, 4 + k)))

    @pl.when(j == 3)
    def _():
        z = _dot(h_ref[...], w_ref[...])
        for k in range(4):
            y = head(z, k)
            keep(kc_ref, k, y)
            put(k, rope_df(y))
        for k in range(4):
            y = head(z, 4 + k)
            keep(vc_ref, k, y)
            put(4 + k, y)


def _in_call(l, x, mods, norm1_g, w_in, ax_q_g, ax_k_g, tables, z, *, latent, n_ctx, lat_len, tm):
    n, d = x.shape
    depth = w_in.shape[0]
    nb_ctx = n_ctx // tm
    bpl = lat_len // tm
    if latent:
        nblk = (n - n_ctx) // tm
        row = lambda i: i + nb_ctx
        grp = lambda i: 1 + i // bpl
    else:
        nblk = nb_ctx
        row = lambda i: i
        grp = lambda i: 0
    cb = 1024
    in_specs = [
        pl.BlockSpec((tm, d), lambda i, j: (row(i), 0)),
        pl.BlockSpec((None, None, None, 1, d), lambda i, j: (l, grp(i), 0, 0, 0)),
        pl.BlockSpec((None, None, None, 1, d), lambda i, j: (l, grp(i), 1, 0, 0)),
        pl.BlockSpec((None, 1, d), lambda i, j: (l, 0, 0)),
        pl.BlockSpec((None, d, cb), lambda i, j: (l, 0, j)),
        pl.BlockSpec((None, 1, HEAD_DIM), lambda i, j: (l, 0, 0)),
        pl.BlockSpec((None, 1, HEAD_DIM), lambda i, j: (l, 0, 0)),
    ]
    args = [x, mods, mods, norm1_g.reshape(depth, 1, d), w_in,
            ax_q_g.reshape(depth, 1, HEAD_DIM), ax_k_g.reshape(depth, 1, HEAD_DIM)]
    z_spec = pl.BlockSpec((tm, cb), lambda i, j: (row(i), j))
    z_shape = jax.ShapeDtypeStruct((n, IN_COLS), BF16)
    scratch = [pltpu.VMEM((tm, d), BF16)]
    if latent:
        tspec = pl.BlockSpec((tm, HEAD_DIM), lambda i, j: (i % bpl, 0))
        in_specs += [tspec] * 4 + [pl.BlockSpec(memory_space=pl.ANY)]
        args += list(tables) + [z]
        return pl.pallas_call(
            functools.partial(_in_kernel, latent=True),
            out_shape=z_shape, grid=(nblk, 4), in_specs=in_specs, out_specs=z_spec,
            scratch_shapes=scratch, input_output_aliases={len(args) - 1: 0},
            compiler_params=_cparams(("parallel", "arbitrary")), name="in_proj_latent",
        )(*args)
    kv2 = jax.ShapeDtypeStruct((n_ctx, 2 * HEAD_DIM), F32)
    kv4 = jax.ShapeDtypeStruct((n_ctx, 4 * HEAD_DIM), F32)
    s2 = pl.BlockSpec((tm, 2 * HEAD_DIM), lambda i, j: (i, 0))
    s4 = pl.BlockSpec((tm, 4 * HEAD_DIM), lambda i, j: (i, 0))
    in_specs.append(pl.BlockSpec(memory_space=pl.ANY))
    args.append(z)
    return pl.pallas_call(
        functools.partial(_in_kernel, latent=False),
        out_shape=(z_shape, kv2, kv2, kv2, kv2, kv4, kv4), grid=(nblk, 4), in_specs=in_specs,
        out_specs=(z_spec, s2, s2, s2, s2, s4, s4), scratch_shapes=scratch,
        input_output_aliases={len(args) - 1: 0},
        compiler_params=_cparams(("parallel", "arbitrary")), name="in_proj_context",
    )(*args)


def _gqa_kernel(*refs, groups, tq, window, has_cache, has_sink, l, lat_len):
    refs = list(refs)
    sink_ref = refs.pop(0) if has_sink else None
    q_ref, k_ref, v_ref = refs[:3]
    refs = refs[3:]
    if has_cache:
        ck_ref, cv_ref = refs[:2]
        refs = refs[2:]
    o_ref = refs[-1]
    kv = pl.program_id(1)
    qi = pl.program_id(2)
    scale = HEAD_DIM ** -0.5

    if window:
        span = tq + 2 * WINDOW
        start = jnp.clip(qi * tq - WINDOW, 0, lat_len - span)
        start = pl.multiple_of(start, WINDOW)
        k = k_ref[pl.ds(start, span), :]
        v = v_ref[pl.ds(start, span), :]
        qpos = qi * tq + lax.broadcasted_iota(jnp.int32, (tq, span), 0)
        kpos = start + lax.broadcasted_iota(jnp.int32, (tq, span), 1)
        ok = jnp.abs(qpos - kpos) <= WINDOW
    else:
        k = k_ref[...]
        v = v_ref[...]
    if has_cache:
        ck = ck_ref[...].astype(BF16)
        cv = cv_ref[...].astype(BF16)

    c2 = scale * LOG2E
    for g in range(groups):
        q = q_ref[:, g * HEAD_DIM:(g + 1) * HEAD_DIM]
        s = _dot_nt(q, k)
        if window:
            s = jnp.where(ok, s, NEG_INF)
        m = jnp.max(s, axis=-1, keepdims=True)
        if has_cache:
            s2 = _dot_nt(q, ck)
            m = jnp.maximum(m, jnp.max(s2, axis=-1, keepdims=True))
        if has_sink:
            sink = sink_ref[l, kv * groups + g] * (1.0 / scale)
            m = jnp.maximum(m, sink)
        e = jnp.exp2((s - m) * c2)
        den = jnp.sum(e, axis=-1, keepdims=True)
        o = _dot(e.astype(BF16), v)
        if has_cache:
            e2 = jnp.exp2((s2 - m) * c2)
            den = den + jnp.sum(e2, axis=-1, keepdims=True)
            o = o + _dot(e2.astype(BF16), cv)
        if has_sink:
            den = den + jnp.exp2((sink - m) * c2)
        o_ref[:, g * HEAD_DIM:(g + 1) * HEAD_DIM] = (o / den).astype(BF16)


def _gqa_call(l, z, o, *, qcol, kcol, vcol, ocol, heads, kvh, nbatch, seq, row0, tq,
              window=False, cache=None, sink=None):
    groups = heads // kvh
    gw = groups * HEAD_DIM
    nq = seq // tq
    qrow = lambda b, qi: row0 // tq + b * nq + qi
    krow = lambda b: row0 // seq + b
    in_specs, args = [], []
    if sink is not None:
        in_specs.append(pl.BlockSpec(memory_space=pltpu.SMEM))
        args.append(sink)
    in_specs += [
        pl.BlockSpec((tq, gw), lambda b, kv, qi: (qrow(b, qi), qcol // gw + kv)),
        pl.BlockSpec((seq, HEAD_DIM), lambda b, kv, qi: (krow(b), kcol // HEAD_DIM + kv)),
        pl.BlockSpec((seq, HEAD_DIM), lambda b, kv, qi: (krow(b), vcol // HEAD_DIM + kv)),
    ]
    args += [z, z, z]
    if cache is not None:
        past = cache[0].shape[2]
        cspec = pl.BlockSpec((None, None, past, HEAD_DIM), lambda b, kv, qi: (b, l, 0, kv))
        in_specs += [cspec, cspec]
        args += list(cache)
    in_specs.append(pl.BlockSpec(memory_space=pl.ANY))
    args.append(o)
    kern = functools.partial(_gqa_kernel, groups=groups, tq=tq, window=window,
                             has_cache=cache is not None, has_sink=sink is not None, l=l, lat_len=seq)
    return pl.pallas_call(
        kern, out_shape=jax.ShapeDtypeStruct(o.shape, o.dtype), grid=(nbatch, kvh, nq),
        in_specs=in_specs,
        out_specs=pl.BlockSpec((tq, gw), lambda b, kv, qi: (qrow(b, qi), ocol // gw + kv)),
        input_output_aliases={len(args) - 1: 0},
        compiler_params=_cparams(("parallel", "parallel", "arbitrary")),
        name="attn_window" if sink is not None else "attn_axial",
    )(*args)


def _diff_kernel(*refs, has_cache, lam_init, chain_rows):
    refs = list(refs)
    lq1, lk1, lq2, lk2, gain_ref, q_ref, k_ref, v_ref = refs[:8]
    refs = refs[8:]
    if has_cache:
        ck_ref, cv_ref = refs[:2]
    o_ref = refs[-1]
    h = pl.program_id(1)
    half = HEAD_DIM // 2
    scale = half ** -0.5

    lam = (jnp.exp(jnp.sum(lq1[...] * lk1[...], axis=-1, keepdims=True))
           - jnp.exp(jnp.sum(lq2[...] * lk2[...], axis=-1, keepdims=True)) + lam_init)

    k = k_ref[...]
    v = v_ref[...]
    lane = lax.broadcasted_iota(jnp.int32, (1, HEAD_DIM), 1)
    lo = jnp.where(lane < half, 1.0, 0.0).astype(BF16)
    if has_cache:
        ck = ck_ref[...].astype(BF16)
        cv = cv_ref[...].astype(BF16)
    gain = gain_ref[pl.ds(h, 1), :]
    c2 = scale * LOG2E

    for r0 in range(0, q_ref.shape[0], chain_rows):
        q = q_ref[r0:r0 + chain_rows, :]
        qs = (q * lo, q * (1.0 - lo))
        probs = []
        for c in range(2):
            s = _dot_nt(qs[c], k)
            m = jnp.max(s, axis=-1, keepdims=True)
            if has_cache:
                s2 = _dot_nt(qs[c], ck)
                m = jnp.maximum(m, jnp.max(s2, axis=-1, keepdims=True))
            e = jnp.exp2((s - m) * c2)
            den = jnp.sum(e, axis=-1, keepdims=True)
            if has_cache:
                e2 = jnp.exp2((s2 - m) * c2)
                den = den + jnp.sum(e2, axis=-1, keepdims=True)
            else:
                e2 = None
            probs.append((e, e2, 1.0 / den))
        (e_a, e2_a, r_a), (e_b, e2_b, r_b) = probs
        r_b = lam * r_b
        o = _dot((e_a * r_a - e_b * r_b).astype(BF16), v)
        if has_cache:
            o = o + _dot((e2_a * r_a - e2_b * r_b).astype(BF16), cv)
        o = o * lax.rsqrt(jnp.mean(o * o, axis=-1, keepdims=True) + EPS) * gain
        o_ref[r0:r0 + chain_rows, :] = (o * (1.0 - lam_init)).astype(BF16)


def _diff_call(l, z, o, lam_params, gain, *, nbatch, seq, row0, tq, lam_init, cache=None):
    nq = seq // tq
    depth = gain.shape[0]
    half = HEAD_DIM // 2
    qrow = lambda b, qi: row0 // tq + b * nq + qi
    krow = lambda b: row0 // seq + b
    lspec = pl.BlockSpec((None, 1, half), lambda b, h, qi: (l, 0, 0))
    in_specs = [lspec] * 4 + [
        pl.BlockSpec((None, DIFF_HEADS, HEAD_DIM), lambda b, h, qi: (l, 0, 0)),
        pl.BlockSpec((tq, HEAD_DIM), lambda b, h, qi: (qrow(b, qi), QC // HEAD_DIM + h)),
        pl.BlockSpec((seq, HEAD_DIM), lambda b, h, qi: (krow(b), KC // HEAD_DIM + h)),
        pl.BlockSpec((seq, HEAD_DIM), lambda b, h, qi: (krow(b), VC // HEAD_DIM + h)),
    ]
    args = [p.reshape(depth, 1, half) for p in lam_params] + [gain, z, z, z]
    if cache is not None:
        past = cache[0].shape[2]
        cspec = pl.BlockSpec((None, None, past, HEAD_DIM), lambda b, h, qi: (b, l, 0, h))
        in_specs += [cspec, cspec]
        args += list(cache)
    in_specs.append(pl.BlockSpec(memory_space=pl.ANY))
    args.append(o)
    ocol0 = (WIN_HEADS + AX_HEADS)
    return pl.pallas_call(
        functools.partial(_diff_kernel, has_cache=cache is not None, lam_init=lam_init,
                          chain_rows=min(tq, 256)),
        out_shape=jax.ShapeDtypeStruct(o.shape, o.dtype), grid=(nbatch, DIFF_HEADS, nq),
        in_specs=in_specs,
        out_specs=pl.BlockSpec((tq, HEAD_DIM), lambda b, h, qi: (qrow(b, qi), ocol0 + h)),
        input_output_aliases={len(args) - 1: 0},
        compiler_params=_cparams(("parallel", "parallel", "arbitrary")),
        name="attn_diff",
    )(*args)


def _out_kernel(o_ref, x_ref, w_ref, g1_ref, sh2_ref, sc2_ref, n2g_ref, x1_ref, h2_ref):
    a = _dot(o_ref[...], w_ref[...])
    x1 = x_ref[...] + g1_ref[...] * a
    x1_ref[...] = x1
    y = x1 * lax.rsqrt(jnp.mean(x1 * x1, axis=-1, keepdims=True) + EPS) * n2g_ref[...]
    h2_ref[...] = (y * (1.0 + sc2_ref[...]) + sh2_ref[...]).astype(BF16)


def _group_of_block(i, nb_ctx, bpl):
    return jnp.where(i < nb_ctx, 0, 1 + (i - nb_ctx) // bpl)


def _out_call(l, o, x, w_out, mods, norm2_g, *, n_ctx, lat_len, tm):
    n, d = x.shape
    depth = w_out.shape[0]
    nb_ctx, bpl = n_ctx // tm, lat_len // tm
    grp = lambda i: _group_of_block(i, nb_ctx, bpl)
    mspec = lambda which: pl.BlockSpec((None, None, None, 1, d), lambda i: (l, grp(i), which, 0, 0))
    return pl.pallas_call(
        _out_kernel,
        out_shape=(jax.ShapeDtypeStruct((n, d), F32), jax.ShapeDtypeStruct((n, d), BF16)),
        grid=(n // tm,),
        in_specs=[
            pl.BlockSpec((tm, d), lambda i: (i, 0)),
            pl.BlockSpec((tm, d), lambda i: (i, 0)),
            pl.BlockSpec((None, d, d), lambda i: (l, 0, 0)),
            mspec(2), mspec(3), mspec(4),
            pl.BlockSpec((None, 1, d), lambda i: (l, 0, 0)),
        ],
        out_specs=(pl.BlockSpec((tm, d), lambda i: (i, 0)), pl.BlockSpec((tm, d), lambda i: (i, 0))),
        compiler_params=_cparams(("parallel",)),
        name="out_proj",
    )(o, x, w_out, mods, mods, mods, norm2_g.reshape(depth, 1, d))


def _score_kernel(kw_ref, h2_ref, o_ref):
    o_ref[...] = _dot_nt(kw_ref[...], h2_ref[...])


def _score_call(l, kw, h2, *, tm):
    n, d = h2.shape
    rows = kw.shape[1]
    return pl.pallas_call(
        _score_kernel,
        out_shape=jax.ShapeDtypeStruct((rows, n), F32),
        grid=(n // tm,),
        in_specs=[pl.BlockSpec((None, rows, d), lambda i: (l, 0, 0)),
                  pl.BlockSpec((tm, d), lambda i: (i, 0))],
        out_specs=pl.BlockSpec((rows, tm), lambda i: (0, i)),
        compiler_params=_cparams(("parallel",)),
        name="peer_scores",
    )(kw, h2)


def _sort_pairs(n):
    pairs = []

    def merge(lo, hi, r):
        step = r * 2
        if step < hi - lo:
            merge(lo, hi, step)
            merge(lo + r, hi, step)
            for i in range(lo + r, hi - r, step):
                pairs.append((i, i + r))
        else:
            pairs.append((lo, lo + r))

    def sort(lo, hi):
        if hi - lo >= 1:
            mid = lo + (hi - lo) // 2
            sort(lo, mid)
            sort(mid + 1, hi)
            merge(lo, hi, 1)

    sort(0, n - 1)
    return pairs


_SORT16 = _sort_pairs(PEER_TOPK)


def _vmax(a, b):
    if a is None:
        return b
    if b is None:
        return a
    return jnp.maximum(a, b)


def _vmin(a, b):
    if a is None or b is None:
        return None
    return jnp.minimum(a, b)


def _bitonic_merge_desc(x):
    x = list(x)
    d = PEER_TOPK // 2
    while d >= 1:
        for i in range(PEER_TOPK):
            if (i & d) == 0:
                a, b = x[i], x[i + d]
                x[i], x[i + d] = _vmax(a, b), _vmin(a, b)
        d //= 2
    return x


def _merge_top16(a, b):
    a = list(a) + [None] * (PEER_TOPK - len(a))
    b = list(b) + [None] * (PEER_TOPK - len(b))
    return _bitonic_merge_desc([_vmax(a[k], b[PEER_TOPK - 1 - k]) for k in range(PEER_TOPK)])


def _router_kernel(s_ref, r1_ref, e1_ref, cnt_ref, e0_ref, top_ref):
    nk = PEER_KEYS
    nv = nk // 8
    tk = PEER_TOPK
    sub = lax.broadcasted_iota(jnp.int32, (8, s_ref.shape[1]), 0)

    for g in range(2 * PEER_HEADS):
        x = [s_ref[g * nk + 8 * r:g * nk + 8 * r + 8, :] for r in range(nv)]
        for (i, j) in _SORT16:
            x[i], x[j] = jnp.maximum(x[i], x[j]), jnp.minimum(x[i], x[j])
        for shift in (4, 2, 1):
            y = [pltpu.roll(v, shift, axis=0) for v in x]
            x = _bitonic_merge_desc([jnp.maximum(x[k], y[tk - 1 - k]) for k in range(tk)])
        for a in range(tk):
            top_ref[g * tk + a] = x[a]

    def packed(c, a):
        out = top_ref[c * tk + a]
        for h in range(1, PEER_HEADS):
            out = jnp.where(sub == h, top_ref[(2 * h + c) * tk + a], out)
        return out

    p0 = [packed(0, a) for a in range(tk)]
    p1 = [packed(1, b) for b in range(tk)]
    rows = [[p0[a] + p1[b] for b in range(tk // (a + 1))] for a in range(8)]
    col0 = [p0[a] + p1[0] for a in range(8, tk)]
    t1 = _merge_top16(rows[0], _merge_top16(rows[1], col0))
    t2 = _merge_top16(_merge_top16(rows[2], rows[3]), _merge_top16(rows[4], rows[5]))
    t3 = _merge_top16(rows[6], rows[7])
    best = _merge_top16(t1, _merge_top16(t2, t3))
    tau_p = best[tk - 1]
    zsum = jnp.ones_like(tau_p)
    for k in range(1, tk):
        zsum = zsum + jnp.exp(best[k] - best[0])
    zinv_p = 1.0 / zsum

    cnt_p = []
    for a in range(tk):
        acc = jnp.zeros_like(tau_p)
        for b in range(tk):
            acc = acc + jnp.where(p0[a] + p1[b] >= tau_p, 1.0, 0.0)
        cnt_p.append(acc)

    def pair_words(v):
        bits = pltpu.bitcast(v, jnp.uint32)
        return bits | lax.shift_right_logical(bits, jnp.uint32(16))

    for h in range(PEER_HEADS):
        shape = (8, s_ref.shape[1])
        zinv = jnp.broadcast_to(zinv_p[h:h + 1, :], shape)
        cnt_a = [jnp.broadcast_to(cnt_p[a][h:h + 1, :], shape) for a in range(tk)]
        t0 = [top_ref[(2 * h) * tk + a] for a in range(tk)]
        t1h = [top_ref[(2 * h + 1) * tk + b] for b in range(tk)]
        for rr in range(nv // 2):
            ranks, gates = [], []
            for r in (2 * rr, 2 * rr + 1):
                s0 = s_ref[(2 * h) * nk + 8 * r:(2 * h) * nk + 8 * r + 8, :]
                s1 = s_ref[(2 * h + 1) * nk + 8 * r:(2 * h + 1) * nk + 8 * r + 8, :]
                rank = jnp.full(shape, float(tk), F32)
                cnt = jnp.zeros(shape, F32)
                for b in range(tk - 1, -1, -1):
                    rank = jnp.where(s1 >= t1h[b], float(b), rank)
                    cnt = jnp.where(s0 >= t0[b], cnt_a[b], cnt)
                ranks.append(rank)
                gates.append(jnp.exp(s1 - t1h[0]))
                e0 = (jnp.exp(s0 - t0[0]) * zinv).astype(BF16).astype(F32)
                cnt_ref[h, 8 * r:8 * r + 8, :] = pair_words(cnt)
                e0_ref[h, 8 * r:8 * r + 8, :] = pair_words(e0)
            r1_ref[h, 16 * rr:16 * rr + 16, :] = jnp.concatenate(ranks, axis=0).astype(BF16)
            e1_ref[h, 16 * rr:16 * rr + 16, :] = jnp.concatenate(gates, axis=0).astype(BF16)


def _router_call(s_t):
    rows, n = s_t.shape
    tl = 128
    half = jax.ShapeDtypeStruct((PEER_HEADS, PEER_KEYS, n), BF16)
    words = jax.ShapeDtypeStruct((PEER_HEADS, PEER_KEYS, n), jnp.uint32)
    spec = pl.BlockSpec((PEER_HEADS, PEER_KEYS, tl), lambda i: (0, 0, i))
    return pl.pallas_call(
        _router_kernel,
        out_shape=(half, half, words, words),
        grid=(n // tl,),
        in_specs=[pl.BlockSpec((rows, tl), lambda i: (0, i))],
        out_specs=(spec, spec, spec, spec),
        scratch_shapes=[pltpu.VMEM((2 * PEER_HEADS * PEER_TOPK, 8, tl), F32)],
        compiler_params=_cparams(("parallel",)),
        name="peer_router",
    )(s_t)


def _gelu(a):
    c0 = math.sqrt(2.0 / math.pi)
    half = 0.5 * a
    return half + half * jnp.tanh(a * (c0 + (c0 * 0.044715) * (a * a)))


PEER_TIE_STRIDE = 6


def _peer_kernel(*refs, final, ib):
    if final:
        (h2_ref, u_ref, vt_ref, r1_ref, e1_ref, cnt_ref, e0_ref, x1_ref, g2_ref, fg_ref,
         o_ref, acc_ref, w_ref, cs_ref, es_ref) = refs
    else:
        (h2_ref, u_ref, vt_ref, r1_ref, e1_ref, cnt_ref, e0_ref, x1_ref, g2_ref,
         o_ref, acc_ref, w_ref, cs_ref, es_ref) = refs
    e = pl.program_id(1)
    t = h2_ref.shape[0]
    nchunk = PEER_KEYS // 16

    @pl.when(e == 0)
    def _():
        acc_ref[...] = jnp.zeros_like(acc_ref)

    lhs = []
    for ii in range(ib):
        for h in range(PEER_HEADS):
            slot = (ii % 2) * PEER_HEADS + h
            cs_ref[slot] = pltpu.bitcast(jnp.broadcast_to(cnt_ref[h, ii:ii + 1, :], (8, t)), BF16)
            es_ref[slot] = pltpu.bitcast(jnp.broadcast_to(e0_ref[h, ii:ii + 1, :], (8, t)), BF16)
        w = [None] * nchunk
        for h in range(PEER_HEADS):
            cnt = cs_ref[(ii % 2) * PEER_HEADS + h]
            e0 = es_ref[(ii % 2) * PEER_HEADS + h]
            for jc in range(nchunk):
                r1 = r1_ref[h, jc * 16:(jc + 1) * 16, :]
                e1 = e1_ref[h, jc * 16:(jc + 1) * 16, :]
                gate = jnp.where(r1 < cnt, e0, jnp.zeros_like(e0)) * e1
                w[jc] = gate if w[jc] is None else w[jc] + gate
        for jc in range(nchunk):
            r0 = ii * PEER_KEYS + jc * 16
            w_ref[r0:r0 + 16, :] = w[jc]
            bits = pltpu.bitcast(w[jc][:, 0:128], jnp.uint32)
            for lt in range(1, t // 128):
                bits = bits | pltpu.bitcast(w[jc][:, lt * 128:(lt + 1) * 128], jnp.uint32)
            zero = lax.shift_right_logical(lax.shift_right_logical(bits, jnp.uint32(16)), jnp.uint32(16))
            zero = pltpu.bitcast(zero, BF16)
            slot = (ii * nchunk + jc) * PEER_TIE_STRIDE
            rows = (slot % 64) * 16
            c0 = (slot // 64) * 256
            piece = u_ref[rows:rows + 16, c0:c0 + 256]
            lhs.append((rows, c0, piece + jnp.concatenate([zero, zero], axis=1)))
    tied = {(rows, c0): v for rows, c0, v in lhs}
    row_groups = []
    for rg in range(ib * PEER_KEYS // 16):
        rows = rg * 16
        cols = []
        for c in range(u_ref.shape[1] // 256):
            key = (rows, c * 256)
            cols.append(tied[key] if key in tied else u_ref[rows:rows + 16, c * 256:(c + 1) * 256])
        row_groups.append(jnp.concatenate(cols, axis=1))
    a_t = _dot_nt(jnp.concatenate(row_groups, axis=0), h2_ref[...])
    p = [w_ref[ch * 16:(ch + 1) * 16, :] * _gelu(a_t[ch * 16:(ch + 1) * 16, :]).astype(BF16)
         for ch in range(ib * nchunk)]
    acc_ref[...] += _dot(vt_ref[...], jnp.concatenate(p, axis=0))

    @pl.when(e == pl.num_programs(1) - 1)
    def _():
        x2 = x1_ref[...] + g2_ref[...] * acc_ref[...].T
        if final:
            x2 = x2 * lax.rsqrt(jnp.mean(x2 * x2, axis=-1, keepdims=True) + EPS) * fg_ref[...]
        o_ref[...] = x2


def _peer_call(l, h2, u_bf, vt_bf, r1, e1, cnt, e0, x1, mods, final_g, *, n_ctx, lat_len, tm, final):
    n, d = x1.shape
    ib = 8
    eb = ib * PEER_KEYS
    n_exp = u_bf.shape[1]
    nb_ctx, bpl = n_ctx // tm, lat_len // tm
    grp = lambda i: _group_of_block(i, nb_ctx, bpl)
    once = dict(pipeline_mode=pl.Buffered(1))
    in_specs = [
        pl.BlockSpec((tm, d), lambda i, e: (i, 0), **once),
        pl.BlockSpec((None, eb, d), lambda i, e: (l, e, 0)),
        pl.BlockSpec((None, d, eb), lambda i, e: (l, 0, e)),
        pl.BlockSpec((PEER_HEADS, PEER_KEYS, tm), lambda i, e: (0, 0, i), **once),
        pl.BlockSpec((PEER_HEADS, PEER_KEYS, tm), lambda i, e: (0, 0, i), **once),
        pl.BlockSpec((PEER_HEADS, ib, tm), lambda i, e: (0, e, i)),
        pl.BlockSpec((PEER_HEADS, ib, tm), lambda i, e: (0, e, i)),
        pl.BlockSpec((tm, d), lambda i, e: (i, 0), **once),
        pl.BlockSpec((None, None, None, 1, d), lambda i, e: (l, grp(i), 5, 0, 0)),
    ]
    args = [h2, u_bf, vt_bf, r1, e1, cnt, e0, x1, mods]
    if final:
        in_specs.append(pl.BlockSpec((1, d), lambda i, e: (0, 0)))
        args.append(final_g.reshape(1, d))
    nstage = 2 * PEER_HEADS
    return pl.pallas_call(
        functools.partial(_peer_kernel, final=final, ib=ib),
        out_shape=jax.ShapeDtypeStruct((n, d), F32),
        grid=(n // tm, n_exp // eb),
        in_specs=in_specs,
        out_specs=pl.BlockSpec((tm, d), lambda i, e: (i, 0)),
        scratch_shapes=[pltpu.VMEM((d, tm), F32), pltpu.VMEM((eb, tm), BF16),
                        pltpu.VMEM((nstage, 16, tm), BF16), pltpu.VMEM((nstage, 16, tm), BF16)],
        compiler_params=_cparams(("parallel", "arbitrary")),
        name="peer_experts",
    )(*args)


def kernel(x_prompt, x_sample, cache_k_win, cache_v_win, cache_k_axial, cache_v_axial, cache_k_diff,
           cache_v_diff, c, c_ctx, norm1_g, w_ada, b_ada, w_in, win_sink, ax_q_g, ax_k_g, lam_q1, lam_k1,
           lam_q2, lam_k2, diff_subln_g, w_out, norm2_g, peer_wq, peer_keys, peer_u, peer_v, final_g):
    batch, seq, d = x_prompt.shape
    dec_batch, lat_len, _ = x_sample.shape
    depth = w_in.shape[0]
    past = cache_k_win.shape[2]
    n_ctx = batch * seq
    n_lat = dec_batch * lat_len
    n = n_ctx + n_lat
    tm = 512
    tq_ctx = seq
    tq_lat = 256

    x = jnp.concatenate([x_prompt.reshape(n_ctx, d), x_sample.reshape(n_lat, d)], axis=0)
    cond = jnp.zeros((8, d), F32).at[0].set(c_ctx).at[1:1 + dec_batch].set(c)
    mods = _ada_call(cond, w_ada, b_ada).reshape(depth, 8, 6, 1, d)
    kw = _kw_call(peer_keys, peer_wq)
    w_in_bf = w_in.astype(BF16)
    w_out_bf = w_out.astype(BF16)
    u_bf = peer_u.astype(BF16)
    vt_bf = _vt_call(peer_v)
    tables = _rope_tables(lat_len)

    cache_a = (cache_k_win.reshape(dec_batch, depth, past, -1), cache_v_win.reshape(dec_batch, depth, past, -1))
    cache_b = (cache_k_axial.reshape(dec_batch, depth, past, -1), cache_v_axial.reshape(dec_batch, depth, past, -1))
    cache_c = (cache_k_diff.reshape(dec_batch, depth, past, -1), cache_v_diff.reshape(dec_batch, depth, past, -1))
    lam_params = (lam_q1, lam_k1, lam_q2, lam_k2)

    o = jnp.zeros((n, d), BF16)
    z = jnp.zeros((n, IN_COLS), BF16)
    kv_out = [[] for _ in range(6)]
    for l in range(depth):
        lam_init = 0.8 - 0.6 * math.exp(-0.3 * l)
        geo = dict(n_ctx=n_ctx, lat_len=lat_len, tm=tm)
        res = _in_call(l, x, mods, norm1_g, w_in_bf, ax_q_g, ax_k_g, None, z, latent=False, **geo)
        z = res[0]
        for k in range(6):
            kv_out[k].append(res[1 + k])
        z = _in_call(l, x, mods, norm1_g, w_in_bf, ax_q_g, ax_k_g, tables, z, latent=True, **geo)

        ctx = dict(nbatch=batch, seq=seq, row0=0, tq=tq_ctx)
        lat = dict(nbatch=dec_batch, seq=lat_len, row0=n_ctx, tq=tq_lat)
        win = dict(qcol=QA, kcol=KA, vcol=VA, ocol=0, heads=WIN_HEADS, kvh=WIN_KV, sink=win_sink)
        axl = dict(qcol=QB, kcol=KB, vcol=VB, ocol=WIN_HEADS * HEAD_DIM, heads=AX_HEADS, kvh=AX_KV)
        o = _gqa_call(l, z, o, **win, **ctx)
        o = _gqa_call(l, z, o, **axl, **ctx)
        o = _diff_call(l, z, o, lam_params, diff_subln_g, lam_init=lam_init, **ctx)
        o = _gqa_call(l, z, o, **win, **lat, window=True, cache=cache_a)
        o = _gqa_call(l, z, o, **axl, **lat, cache=cache_b)
        o = _diff_call(l, z, o, lam_params, diff_subln_g, lam_init=lam_init, cache=cache_c,
                       **dict(lat, tq=2 * tq_lat if lat_len % (2 * tq_lat) == 0 else tq_lat))

        x1, h2 = _out_call(l, o, x, w_out_bf, mods, norm2_g, **geo)
        s_t = _score_call(l, kw, h2, tm=tm)
        r1, e1, cnt, e0 = _router_call(s_t)
        x = _peer_call(l, h2, u_bf, vt_bf, r1, e1, cnt, e0, x1, mods, final_g,
                       final=(l == depth - 1), **geo)

    y_prompt = x[:n_ctx].reshape(batch, seq, d)
    y_sample = x[n_ctx:].reshape(dec_batch, lat_len, d)
    outs = []
    for k, nh in enumerate((WIN_KV, WIN_KV, AX_KV, AX_KV, DIFF_HEADS, DIFF_HEADS)):
        outs.append(jnp.stack([a.reshape(batch, seq, nh, HEAD_DIM) for a in kv_out[k]], axis=1))
    return (y_prompt, y_sample, *outs)
```

```python
import functools
import math

import jax
import jax.numpy as jnp
from jax import lax
from jax.experimental import pallas as pl
from jax.experimental.pallas import tpu as pltpu

F32 = jnp.float32
BF16 = jnp.bfloat16

HEAD_DIM = 128
GRID_W = 64
WINDOW = 128
ROPE_THETA = 10000.0
PEER_HEADS = 8
PEER_KEYS = 128
PEER_TOPK = 16
EPS = 1e-6
NEG_INF = -1e30
LOG2E = 1.4426950408889634

QA, KA, VA, QB, KB, VB, QC, KC, VC = 0, 512, 768, 1024, 2048, 2304, 2560, 3072, 3584
IN_COLS = 4096
WIN_HEADS, WIN_KV, AX_HEADS, AX_KV, DIFF_HEADS = 4, 2, 8, 2, 4

VMEM_LIMIT = 56 * 1024 * 1024


def _cparams(sem, **kw):
    return pltpu.CompilerParams(dimension_semantics=sem, vmem_limit_bytes=VMEM_LIMIT, **kw)


def _dot_nt(a, b):
    return lax.dot_general(a, b, (((1,), (1,)), ((), ())), preferred_element_type=F32)


def _dot(a, b):
    return jnp.dot(a, b, preferred_element_type=F32)


def _ada_kernel(cond_ref, w_ref, b_ref, o_ref):
    c = cond_ref[...]
    s = (c * jax.nn.sigmoid(c)).astype(BF16)
    o_ref[...] = _dot(s, w_ref[...].astype(BF16)) + b_ref[...]


def _ada_call(cond, w_ada, b_ada):
    depth, d, n6 = w_ada.shape
    tn = 1024
    return pl.pallas_call(
        _ada_kernel,
        out_shape=jax.ShapeDtypeStruct((depth, 8, n6), F32),
        grid=(depth, n6 // tn),
        in_specs=[
            pl.BlockSpec((8, d), lambda l, j: (0, 0)),
            pl.BlockSpec((None, d, tn), lambda l, j: (l, 0, j)),
            pl.BlockSpec((None, 1, tn), lambda l, j: (l, 0, j)),
        ],
        out_specs=pl.BlockSpec((None, 8, tn), lambda l, j: (l, 0, j)),
        compiler_params=_cparams(("parallel", "parallel")),
        name="ada",
    )(cond, w_ada, b_ada.reshape(depth, 1, n6))


def _kw_kernel(keys_ref, wq_ref, o_ref):
    kw = lax.dot_general(keys_ref[...], wq_ref[...], (((1,), (1,)), ((), ())),
                         preferred_element_type=F32, precision=lax.Precision.HIGHEST)
    o_ref[...] = kw.astype(BF16)


def _kw_call(peer_keys, peer_wq):
    depth, d, nq = peer_wq.shape
    ngrp = PEER_HEADS * 2
    sub = PEER_KEYS
    return pl.pallas_call(
        _kw_kernel,
        out_shape=jax.ShapeDtypeStruct((depth, ngrp * sub, d), BF16),
        grid=(depth, ngrp),
        in_specs=[
            pl.BlockSpec((None, None, None, sub, sub), lambda l, g: (l, g // 2, g % 2, 0, 0)),
            pl.BlockSpec((None, d, sub), lambda l, g: (l, 0, g)),
        ],
        out_specs=pl.BlockSpec((None, sub, d), lambda l, g: (l, g, 0)),
        compiler_params=_cparams(("parallel", "parallel")),
        name="peer_kw",
    )(peer_keys, peer_wq)


def _vt_kernel(v_ref, o_ref):
    o_ref[...] = v_ref[...].T.astype(BF16)


def _vt_call(peer_v):
    depth, n_exp, d = peer_v.shape
    te = 512
    return pl.pallas_call(
        _vt_kernel,
        out_shape=jax.ShapeDtypeStruct((depth, d, n_exp), BF16),
        grid=(depth, n_exp // te),
        in_specs=[pl.BlockSpec((None, te, d), lambda l, j: (l, j, 0))],
        out_specs=pl.BlockSpec((None, d, te), lambda l, j: (l, 0, j)),
        compiler_params=_cparams(("parallel", "parallel")),
        name="peer_vt",
    )(peer_v)


def _rope_tables(n_tok):
    n_rows = n_tok // GRID_W
    row_pos = jnp.repeat(jnp.arange(n_rows, dtype=jnp.int32), GRID_W).astype(F32)
    col_pos = jnp.tile(jnp.arange(GRID_W, dtype=jnp.int32), n_rows).astype(F32)
    lane = jnp.arange(HEAD_DIM)

    def table(pair, use_row):
        freqs = ROPE_THETA ** (-jnp.arange(pair, dtype=F32) / pair)
        f = freqs[lane % pair]
        pos = jnp.where(use_row[None, :], row_pos[:, None], col_pos[:, None])
        ang = pos * f[None, :]
        first = (lane % (2 * pair)) < pair
        return jnp.cos(ang), jnp.where(first[None, :], -jnp.sin(ang), jnp.sin(ang))

    c_ax, s_ax = table(32, lane < 64)
    c_df, s_df = table(16, (lane % 64) < 32)
    return c_ax, s_ax, c_df, s_df


def _rope(y, c, s, pair):
    lane = lax.broadcasted_iota(jnp.int32, y.shape, 1)
    first = (lane % (2 * pair)) < pair
    partner = jnp.where(first, pltpu.roll(y, HEAD_DIM - pair, axis=1), pltpu.roll(y, pair, axis=1))
    return y * c + partner * s


def _head_norm(y, g):
    return y * lax.rsqrt(jnp.mean(y * y, axis=-1, keepdims=True) + EPS) * g


def _in_kernel(*refs, latent):
    if latent:
        (x_ref, sh_ref, sc_ref, g_ref, w_ref, axq_ref, axk_ref, ca_ref, sa_ref, cd_ref, sd_ref,
         _z_in, z_ref, h_ref) = refs
        ka_ref = va_ref = kb_ref = vb_ref = kc_ref = vc_ref = None
    else:
        (x_ref, sh_ref, sc_ref, g_ref, w_ref, axq_ref, axk_ref, _z_in,
         z_ref, ka_ref, va_ref, kb_ref, vb_ref, kc_ref, vc_ref, h_ref) = refs
    j = pl.program_id(1)

    @pl.when(j == 0)
    def _():
        x = x_ref[...]
        y = x * lax.rsqrt(jnp.mean(x * x, axis=-1, keepdims=True) + EPS) * g_ref[...]
        h_ref[...] = (y * (1.0 + sc_ref[...]) + sh_ref[...]).astype(BF16)

    def rope_ax(y):
        return _rope(y, ca_ref[...], sa_ref[...], 32) if latent else y

    def rope_df(y):
        return _rope(y, cd_ref[...], sd_ref[...], 16) if latent else y

    def head(z, k):
        return z[:, k * HEAD_DIM:(k + 1) * HEAD_DIM]

    def put(k, y):
        z_ref[:, k * HEAD_DIM:(k + 1) * HEAD_DIM] = y.astype(BF16)

    def keep(ref, k, y):
        if not latent:
            ref[:, k * HEAD_DIM:(k + 1) * HEAD_DIM] = y

    @pl.when(j == 0)
    def _():
        z = _dot(h_ref[...], w_ref[...])
        for k in range(4):
            put(k, rope_ax(head(z, k)))
        for k in range(2):
            y = head(z, 4 + k)
            keep(ka_ref, k, y)
            put(4 + k, rope_ax(y))
        for k in range(2):
            y = head(z, 6 + k)
            keep(va_ref, k, y)
            put(6 + k, y)

    @pl.when(j == 1)
    def _():
        z = _dot(h_ref[...], w_ref[...])
        for k in range(8):
            put(k, rope_ax(_head_norm(head(z, k), axq_ref[...])))

    @pl.when(j == 2)
    def _():
        z = _dot(h_ref[...], w_ref[...])
        for k in range(2):
            y = _head_norm(head(z, k), axk_ref[...])
            keep(kb_ref, k, y)
            put(k, rope_ax(y))
        for k in range(2):
            y = head(z, 2 + k)
            keep(vb_ref, k, y)
            put(2 + k, y)
        for k in range(4):
            put(4 + k, rope_df(head(z, 4 + k)))

    @pl.when(j == 3)
    def _():
        z = _dot(h_ref[...], w_ref[...])
        for k in range(4):
            y = head(z, k)
            keep(kc_ref, k, y)
            put(k, rope_df(y))
        for k in range(4):
            y = head(z, 4 + k)
            keep(vc_ref, k, y)
            put(4 + k, y)


def _in_call(l, x, mods, norm1_g, w_in, ax_q_g, ax_k_g, tables, z, *, latent, n_ctx, lat_len, tm):
    n, d = x.shape
    depth = w_in.shape[0]
    nb_ctx = n_ctx // tm
    bpl = lat_len // tm
    if latent:
        nblk = (n - n_ctx) // tm
        row = lambda i: i + nb_ctx
        grp = lambda i: 1 + i // bpl
    else:
        nblk = nb_ctx
        row = lambda i: i
        grp = lambda i: 0
    cb = 1024
    in_specs = [
        pl.BlockSpec((tm, d), lambda i, j: (row(i), 0)),
        pl.BlockSpec((None, None, None, 1, d), lambda i, j: (l, grp(i), 0, 0, 0)),
        pl.BlockSpec((None, None, None, 1, d), lambda i, j: (l, grp(i), 1, 0, 0)),
        pl.BlockSpec((None, 1, d), lambda i, j: (l, 0, 0)),
        pl.BlockSpec((None, d, cb), lambda i, j: (l, 0, j)),
        pl.BlockSpec((None, 1, HEAD_DIM), lambda i, j: (l, 0, 0)),
        pl.BlockSpec((None, 1, HEAD_DIM), lambda i, j: (l, 0, 0)),
    ]
    args = [x, mods, mods, norm1_g.reshape(depth, 1, d), w_in,
            ax_q_g.reshape(depth, 1, HEAD_DIM), ax_k_g.reshape(depth, 1, HEAD_DIM)]
    z_spec = pl.BlockSpec((tm, cb), lambda i, j: (row(i), j))
    z_shape = jax.ShapeDtypeStruct((n, IN_COLS), BF16)
    scratch = [pltpu.VMEM((tm, d), BF16)]
    if latent:
        tspec = pl.BlockSpec((tm, HEAD_DIM), lambda i, j: (i % bpl, 0))
        in_specs += [tspec] * 4 + [pl.BlockSpec(memory_space=pl.ANY)]
        args += list(tables) + [z]
        return pl.pallas_call(
            functools.partial(_in_kernel, latent=True),
            out_shape=z_shape, grid=(nblk, 4), in_specs=in_specs, out_specs=z_spec,
            scratch_shapes=scratch, input_output_aliases={len(args) - 1: 0},
            compiler_params=_cparams(("parallel", "arbitrary")), name="in_proj_latent",
        )(*args)
    kv2 = jax.ShapeDtypeStruct((n_ctx, 2 * HEAD_DIM), F32)
    kv4 = jax.ShapeDtypeStruct((n_ctx, 4 * HEAD_DIM), F32)
    s2 = pl.BlockSpec((tm, 2 * HEAD_DIM), lambda i, j: (i, 0))
    s4 = pl.BlockSpec((tm, 4 * HEAD_DIM), lambda i, j: (i, 0))
    in_specs.append(pl.BlockSpec(memory_space=pl.ANY))
    args.append(z)
    return pl.pallas_call(
        functools.partial(_in_kernel, latent=False),
        out_shape=(z_shape, kv2, kv2, kv2, kv2, kv4, kv4), grid=(nblk, 4), in_specs=in_specs,
        out_specs=(z_spec, s2, s2, s2, s2, s4, s4), scratch_shapes=scratch,
        input_output_aliases={len(args) - 1: 0},
        compiler_params=_cparams(("parallel", "arbitrary")), name="in_proj_context",
    )(*args)


def _gqa_kernel(*refs, groups, tq, window, has_cache, has_sink, l, lat_len):
    refs = list(refs)
    sink_ref = refs.pop(0) if has_sink else None
    q_ref, k_ref, v_ref = refs[:3]
    refs = refs[3:]
    if has_cache:
        ck_ref, cv_ref = refs[:2]
        refs = refs[2:]
    o_ref = refs[-1]
    kv = pl.program_id(1)
    qi = pl.program_id(2)
    scale = HEAD_DIM ** -0.5

    if window:
        span = tq + 2 * WINDOW
        start = jnp.clip(qi * tq - WINDOW, 0, lat_len - span)
        start = pl.multiple_of(start, WINDOW)
        k = k_ref[pl.ds(start, span), :]
        v = v_ref[pl.ds(start, span), :]
        qpos = qi * tq + lax.broadcasted_iota(jnp.int32, (tq, span), 0)
        kpos = start + lax.broadcasted_iota(jnp.int32, (tq, span), 1)
        ok = jnp.abs(qpos - kpos) <= WINDOW
    else:
        k = k_ref[...]
        v = v_ref[...]
    if has_cache:
        ck = ck_ref[...].astype(BF16)
        cv = cv_ref[...].astype(BF16)

    c2 = scale * LOG2E
    for g in range(groups):
        q = q_ref[:, g * HEAD_DIM:(g + 1) * HEAD_DIM]
        s = _dot_nt(q, k)
        if window:
            s = jnp.where(ok, s, NEG_INF)
        m = jnp.max(s, axis=-1, keepdims=True)
        if has_cache:
            s2 = _dot_nt(q, ck)
            m = jnp.maximum(m, jnp.max(s2, axis=-1, keepdims=True))
        if has_sink:
            sink = sink_ref[l, kv * groups + g] * (1.0 / scale)
            m = jnp.maximum(m, sink)
        e = jnp.exp2((s - m) * c2)
        den = jnp.sum(e, axis=-1, keepdims=True)
        o = _dot(e.astype(BF16), v)
        if has_cache:
            e2 = jnp.exp2((s2 - m) * c2)
            den = den + jnp.sum(e2, axis=-1, keepdims=True)
            o = o + _dot(e2.astype(BF16), cv)
        if has_sink:
            den = den + jnp.exp2((sink - m) * c2)
        o_ref[:, g * HEAD_DIM:(g + 1) * HEAD_DIM] = (o / den).astype(BF16)


def _gqa_call(l, z, o, *, qcol, kcol, vcol, ocol, heads, kvh, nbatch, seq, row0, tq,
              window=False, cache=None, sink=None):
    groups = heads // kvh
    gw = groups * HEAD_DIM
    nq = seq // tq
    qrow = lambda b, qi: row0 // tq + b * nq + qi
    krow = lambda b: row0 // seq + b
    in_specs, args = [], []
    if sink is not None:
        in_specs.append(pl.BlockSpec(memory_space=pltpu.SMEM))
        args.append(sink)
    in_specs += [
        pl.BlockSpec((tq, gw), lambda b, kv, qi: (qrow(b, qi), qcol // gw + kv)),
        pl.BlockSpec((seq, HEAD_DIM), lambda b, kv, qi: (krow(b), kcol // HEAD_DIM + kv)),
        pl.BlockSpec((seq, HEAD_DIM), lambda b, kv, qi: (krow(b), vcol // HEAD_DIM + kv)),
    ]
    args += [z, z, z]
    if cache is not None:
        past = cache[0].shape[2]
        cspec = pl.BlockSpec((None, None, past, HEAD_DIM), lambda b, kv, qi: (b, l, 0, kv))
        in_specs += [cspec, cspec]
        args += list(cache)
    in_specs.append(pl.BlockSpec(memory_space=pl.ANY))
    args.append(o)
    kern = functools.partial(_gqa_kernel, groups=groups, tq=tq, window=window,
                             has_cache=cache is not None, has_sink=sink is not None, l=l, lat_len=seq)
    return pl.pallas_call(
        kern, out_shape=jax.ShapeDtypeStruct(o.shape, o.dtype), grid=(nbatch, kvh, nq),
        in_specs=in_specs,
        out_specs=pl.BlockSpec((tq, gw), lambda b, kv, qi: (qrow(b, qi), ocol // gw + kv)),
        input_output_aliases={len(args) - 1: 0},
        compiler_params=_cparams(("parallel", "parallel", "arbitrary")),
        name="attn_window" if sink is not None else "attn_axial",
    )(*args)


def _diff_kernel(*refs, has_cache, lam_init, chain_rows):
    refs = list(refs)
    lq1, lk1, lq2, lk2, gain_ref, q_ref, k_ref, v_ref = refs[:8]
    refs = refs[8:]
    if has_cache:
        ck_ref, cv_ref = refs[:2]
    o_ref = refs[-1]
    h = pl.program_id(1)
    half = HEAD_DIM // 2
    scale = half ** -0.5

    lam = (jnp.exp(jnp.sum(lq1[...] * lk1[...], axis=-1, keepdims=True))
           - jnp.exp(jnp.sum(lq2[...] * lk2[...], axis=-1, keepdims=True)) + lam_init)

    k = k_ref[...]
    v = v_ref[...]
    lane = lax.broadcasted_iota(jnp.int32, (1, HEAD_DIM), 1)
    lo = jnp.where(lane < half, 1.0, 0.0).astype(BF16)
    if has_cache:
        ck = ck_ref[...].astype(BF16)
        cv = cv_ref[...].astype(BF16)
    gain = gain_ref[pl.ds(h, 1), :]
    c2 = scale * LOG2E

    for r0 in range(0, q_ref.shape[0], chain_rows):
        q = q_ref[r0:r0 + chain_rows, :]
        qs = (q * lo, q * (1.0 - lo))
        probs = []
        for c in range(2):
            s = _dot_nt(qs[c], k)
            m = jnp.max(s, axis=-1, keepdims=True)
            if has_cache:
                s2 = _dot_nt(qs[c], ck)
                m = jnp.maximum(m, jnp.max(s2, axis=-1, keepdims=True))
            e = jnp.exp2((s - m) * c2)
            den = jnp.sum(e, axis=-1, keepdims=True)
            if has_cache:
                e2 = jnp.exp2((s2 - m) * c2)
                den = den + jnp.sum(e2, axis=-1, keepdims=True)
            else:
                e2 = None
            probs.append((e, e2, 1.0 / den))
        (e_a, e2_a, r_a), (e_b, e2_b, r_b) = probs
        r_b = lam * r_b
        o = _dot((e_a * r_a - e_b * r_b).astype(BF16), v)
        if has_cache:
            o = o + _dot((e2_a * r_a - e2_b * r_b).astype(BF16), cv)
        o = o * lax.rsqrt(jnp.mean(o * o, axis=-1, keepdims=True) + EPS) * gain
        o_ref[r0:r0 + chain_rows, :] = (o * (1.0 - lam_init)).astype(BF16)


def _diff_call(l, z, o, lam_params, gain, *, nbatch, seq, row0, tq, lam_init, cache=None):
    nq = seq // tq
    depth = gain.shape[0]
    half = HEAD_DIM // 2
    qrow = lambda b, qi: row0 // tq + b * nq + qi
    krow = lambda b: row0 // seq + b
    lspec = pl.BlockSpec((None, 1, half), lambda b, h, qi: (l, 0, 0))
    in_specs = [lspec] * 4 + [
        pl.BlockSpec((None, DIFF_HEADS, HEAD_DIM), lambda b, h, qi: (l, 0, 0)),
        pl.BlockSpec((tq, HEAD_DIM), lambda b, h, qi: (qrow(b, qi), QC // HEAD_DIM + h)),
        pl.BlockSpec((seq, HEAD_DIM), lambda b, h, qi: (krow(b), KC // HEAD_DIM + h)),
        pl.BlockSpec((seq, HEAD_DIM), lambda b, h, qi: (krow(b), VC // HEAD_DIM + h)),
    ]
    args = [p.reshape(depth, 1, half) for p in lam_params] + [gain, z, z, z]
    if cache is not None:
        past = cache[0].shape[2]
        cspec = pl.BlockSpec((None, None, past, HEAD_DIM), lambda b, h, qi: (b, l, 0, h))
        in_specs += [cspec, cspec]
        args += list(cache)
    in_specs.append(pl.BlockSpec(memory_space=pl.ANY))
    args.append(o)
    ocol0 = (WIN_HEADS + AX_HEADS)
    return pl.pallas_call(
        functools.partial(_diff_kernel, has_cache=cache is not None, lam_init=lam_init,
                          chain_rows=min(tq, 256)),
        out_shape=jax.ShapeDtypeStruct(o.shape, o.dtype), grid=(nbatch, DIFF_HEADS, nq),
        in_specs=in_specs,
        out_specs=pl.BlockSpec((tq, HEAD_DIM), lambda b, h, qi: (qrow(b, qi), ocol0 + h)),
        input_output_aliases={len(args) - 1: 0},
        compiler_params=_cparams(("parallel", "parallel", "arbitrary")),
        name="attn_diff",
    )(*args)


def _out_kernel(o_ref, x_ref, w_ref, g1_ref, sh2_ref, sc2_ref, n2g_ref, x1_ref, h2_ref):
    a = _dot(o_ref[...], w_ref[...])
    x1 = x_ref[...] + g1_ref[...] * a
    x1_ref[...] = x1
    y = x1 * lax.rsqrt(jnp.mean(x1 * x1, axis=-1, keepdims=True) + EPS) * n2g_ref[...]
    h2_ref[...] = (y * (1.0 + sc2_ref[...]) + sh2_ref[...]).astype(BF16)


def _group_of_block(i, nb_ctx, bpl):
    return jnp.where(i < nb_ctx, 0, 1 + (i - nb_ctx) // bpl)


def _out_call(l, o, x, w_out, mods, norm2_g, *, n_ctx, lat_len, tm):
    n, d = x.shape
    depth = w_out.shape[0]
    nb_ctx, bpl = n_ctx // tm, lat_len // tm
    grp = lambda i: _group_of_block(i, nb_ctx, bpl)
    mspec = lambda which: pl.BlockSpec((None, None, None, 1, d), lambda i: (l, grp(i), which, 0, 0))
    return pl.pallas_call(
        _out_kernel,
        out_shape=(jax.ShapeDtypeStruct((n, d), F32), jax.ShapeDtypeStruct((n, d), BF16)),
        grid=(n // tm,),
        in_specs=[
            pl.BlockSpec((tm, d), lambda i: (i, 0)),
            pl.BlockSpec((tm, d), lambda i: (i, 0)),
            pl.BlockSpec((None, d, d), lambda i: (l, 0, 0)),
            mspec(2), mspec(3), mspec(4),
            pl.BlockSpec((None, 1, d), lambda i: (l, 0, 0)),
        ],
        out_specs=(pl.BlockSpec((tm, d), lambda i: (i, 0)), pl.BlockSpec((tm, d), lambda i: (i, 0))),
        compiler_params=_cparams(("parallel",)),
        name="out_proj",
    )(o, x, w_out, mods, mods, mods, norm2_g.reshape(depth, 1, d))


def _score_kernel(kw_ref, h2_ref, o_ref):
    o_ref[...] = _dot_nt(kw_ref[...], h2_ref[...])


def _score_call(l, kw, h2, *, tm):
    n, d = h2.shape
    rows = kw.shape[1]
    return pl.pallas_call(
        _score_kernel,
        out_shape=jax.ShapeDtypeStruct((rows, n), F32),
        grid=(n // tm,),
        in_specs=[pl.BlockSpec((None, rows, d), lambda i: (l, 0, 0)),
                  pl.BlockSpec((tm, d), lambda i: (i, 0))],
        out_specs=pl.BlockSpec((rows, tm), lambda i: (0, i)),
        compiler_params=_cparams(("parallel",)),
        name="peer_scores",
    )(kw, h2)


def _sort_pairs(n):
    pairs = []

    def merge(lo, hi, r):
        step = r * 2
        if step < hi - lo:
            merge(lo, hi, step)
            merge(lo + r, hi, step)
            for i in range(lo + r, hi - r, step):
                pairs.append((i, i + r))
        else:
            pairs.append((lo, lo + r))

    def sort(lo, hi):
        if hi - lo >= 1:
            mid = lo + (hi - lo) // 2
            sort(lo, mid)
            sort(mid + 1, hi)
            merge(lo, hi, 1)

    sort(0, n - 1)
    return pairs


_SORT16 = _sort_pairs(PEER_TOPK)


def _vmax(a, b):
    if a is None:
        return b
    if b is None:
        return a
    return jnp.maximum(a, b)


def _vmin(a, b):
    if a is None or b is None:
        return None
    return jnp.minimum(a, b)


def _bitonic_merge_desc(x):
    x = list(x)
    d = PEER_TOPK // 2
    while d >= 1:
        for i in range(PEER_TOPK):
            if (i & d) == 0:
                a, b = x[i], x[i + d]
                x[i], x[i + d] = _vmax(a, b), _vmin(a, b)
        d //= 2
    return x


def _merge_top16(a, b):
    a = list(a) + [None] * (PEER_TOPK - len(a))
    b = list(b) + [None] * (PEER_TOPK - len(b))
    return _bitonic_merge_desc([_vmax(a[k], b[PEER_TOPK - 1 - k]) for k in range(PEER_TOPK)])


def _router_kernel(s_ref, r1_ref, e1_ref, cnt_ref, e0_ref, top_ref):
    nk = PEER_KEYS
    nv = nk // 8
    tk = PEER_TOPK
    sub = lax.broadcasted_iota(jnp.int32, (8, s_ref.shape[1]), 0)

    for g in range(2 * PEER_HEADS):
        x = [s_ref[g * nk + 8 * r:g * nk + 8 * r + 8, :] for r in range(nv)]
        for (i, j) in _SORT16:
            x[i], x[j] = jnp.maximum(x[i], x[j]), jnp.minimum(x[i], x[j])
        for shift in (4, 2, 1):
            y = [pltpu.roll(v, shift, axis=0) for v in x]
            x = _bitonic_merge_desc([jnp.maximum(x[k], y[tk - 1 - k]) for k in range(tk)])
        for a in range(tk):
            top_ref[g * tk + a] = x[a]

    def packed(c, a):
        out = top_ref[c * tk + a]
        for h in range(1, PEER_HEADS):
            out = jnp.where(sub == h, top_ref[(2 * h + c) * tk + a], out)
        return out

    p0 = [packed(0, a) for a in range(tk)]
    p1 = [packed(1, b) for b in range(tk)]
    rows = [[p0[a] + p1[b] for b in range(tk // (a + 1))] for a in range(8)]
    col0 = [p0[a] + p1[0] for a in range(8, tk)]
    t1 = _merge_top16(rows[0], _merge_top16(rows[1], col0))
    t2 = _merge_top16(_merge_top16(rows[2], rows[3]), _merge_top16(rows[4], rows[5]))
    t3 = _merge_top16(rows[6], rows[7])
    best = _merge_top16(t1, _merge_top16(t2, t3))
    tau_p = best[tk - 1]
    zsum = jnp.ones_like(tau_p)
    for k in range(1, tk):
        zsum = zsum + jnp.exp(best[k] - best[0])
    zinv_p = 1.0 / zsum

    cnt_p = []
    for a in range(tk):
        acc = jnp.zeros_like(tau_p)
        for b in range(tk):
            acc = acc + jnp.where(p0[a] + p1[b] >= tau_p, 1.0, 0.0)
        cnt_p.append(acc)

    def pair_words(v):
        bits = pltpu.bitcast(v, jnp.uint32)
        return bits | lax.shift_right_logical(bits, jnp.uint32(16))

    for h in range(PEER_HEADS):
        shape = (8, s_ref.shape[1])
        zinv = jnp.broadcast_to(zinv_p[h:h + 1, :], shape)
        cnt_a = [jnp.broadcast_to(cnt_p[a][h:h + 1, :], shape) for a in range(tk)]
        t0 = [top_ref[(2 * h) * tk + a] for a in range(tk)]
        t1h = [top_ref[(2 * h + 1) * tk + b] for b in range(tk)]
        for rr in range(nv // 2):
            ranks, gates = [], []
            for r in (2 * rr, 2 * rr + 1):
                s0 = s_ref[(2 * h) * nk + 8 * r:(2 * h) * nk + 8 * r + 8, :]
                s1 = s_ref[(2 * h + 1) * nk + 8 * r:(2 * h + 1) * nk + 8 * r + 8, :]
                rank = jnp.full(shape, float(tk), F32)
                cnt = jnp.zeros(shape, F32)
                for b in range(tk - 1, -1, -1):
                    rank = jnp.where(s1 >= t1h[b], float(b), rank)
                    cnt = jnp.where(s0 >= t0[b], cnt_a[b], cnt)
                ranks.append(rank)
                gates.append(jnp.exp(s1 - t1h[0]))
                e0 = (jnp.exp(s0 - t0[0]) * zinv).astype(BF16).astype(F32)
                cnt_ref[h, 8 * r:8 * r + 8, :] = pair_words(cnt)
                e0_ref[h, 8 * r:8 * r + 8, :] = pair_words(e0)
            r1_ref[h, 16 * rr:16 * rr + 16, :] = jnp.concatenate(ranks, axis=0).astype(BF16)
            e1_ref[h, 16 * rr:16 * rr + 16, :] = jnp.concatenate(gates, axis=0).astype(BF16)


def _router_call(s_t):
    rows, n = s_t.shape
    tl = 128
    half = jax.ShapeDtypeStruct((PEER_HEADS, PEER_KEYS, n), BF16)
    words = jax.ShapeDtypeStruct((PEER_HEADS, PEER_KEYS, n), jnp.uint32)
    spec = pl.BlockSpec((PEER_HEADS, PEER_KEYS, tl), lambda i: (0, 0, i))
    return pl.pallas_call(
        _router_kernel,
        out_shape=(half, half, words, words),
        grid=(n // tl,),
        in_specs=[pl.BlockSpec((rows, tl), lambda i: (0, i))],
        out_specs=(spec, spec, spec, spec),
        scratch_shapes=[pltpu.VMEM((2 * PEER_HEADS * PEER_TOPK, 8, tl), F32)],
        compiler_params=_cparams(("parallel",)),
        name="peer_router",
    )(s_t)


def _gelu(a):
    c0 = math.sqrt(2.0 / math.pi)
    half = 0.5 * a
    return half + half * jnp.tanh(a * (c0 + (c0 * 0.044715) * (a * a)))


def _zero_tile(chunk):
    bits = pltpu.bitcast(chunk[:, 0:128], jnp.uint32)
    for lt in range(1, chunk.shape[1] // 128):
        bits = bits | pltpu.bitcast(chunk[:, lt * 128:(lt + 1) * 128], jnp.uint32)
    zero = lax.shift_right_logical(lax.shift_right_logical(bits, jnp.uint32(16)), jnp.uint32(16))
    return pltpu.bitcast(zero, BF16)


def _tied_operand(ref, ties):
    row_groups = []
    for rg in range(ref.shape[0] // 16):
        cols = []
        for c in range(ref.shape[1] // 256):
            piece = ref[rg * 16:(rg + 1) * 16, c * 256:(c + 1) * 256]
            z = ties.get((rg, c))
            cols.append(piece if z is None else piece + jnp.concatenate([z, z], axis=1))
        row_groups.append(jnp.concatenate(cols, axis=1))
    return jnp.concatenate(row_groups, axis=0)


def _peer_kernel(*refs, final, ib, nblk):
    if final:
        (h2_ref, u_ref, vt_ref, r1_ref, e1_ref, cnt_ref, e0_ref, x1_ref, g2_ref, fg_ref,
         o_ref, acc_ref, a0_ref, a1_ref, w0_ref, w1_ref, cs_ref, es_ref) = refs
    else:
        (h2_ref, u_ref, vt_ref, r1_ref, e1_ref, cnt_ref, e0_ref, x1_ref, g2_ref,
         o_ref, acc_ref, a0_ref, a1_ref, w0_ref, w1_ref, cs_ref, es_ref) = refs
    s = pl.program_id(1)
    t = h2_ref.shape[0]
    nchunk = PEER_KEYS // 16
    nch = ib * nchunk
    u_groups = u_ref.shape[0] // 16
    v_groups = vt_ref.shape[0] // 16

    def gates(w_ref):
        chunks = []
        for ii in range(ib):
            for h in range(PEER_HEADS):
                slot = (ii % 2) * PEER_HEADS + h
                cs_ref[slot] = pltpu.bitcast(jnp.broadcast_to(cnt_ref[h, ii:ii + 1, :], (8, t)), BF16)
                es_ref[slot] = pltpu.bitcast(jnp.broadcast_to(e0_ref[h, ii:ii + 1, :], (8, t)), BF16)
            w = [None] * nchunk
            for h in range(PEER_HEADS):
                cnt = cs_ref[(ii % 2) * PEER_HEADS + h]
                e0 = es_ref[(ii % 2) * PEER_HEADS + h]
                for jc in range(nchunk):
                    r1 = r1_ref[h, jc * 16:(jc + 1) * 16, :]
                    e1 = e1_ref[h, jc * 16:(jc + 1) * 16, :]
                    gate = jnp.where(r1 < cnt, e0, jnp.zeros_like(e0)) * e1
                    w[jc] = gate if w[jc] is None else w[jc] + gate
            for jc in range(nchunk):
                r0 = ii * PEER_KEYS + jc * 16
                w_ref[r0:r0 + 16, :] = w[jc]
                chunks.append(w[jc])
        return chunks

    def gated(a_ref, w_ref):
        return [w_ref[ch * 16:(ch + 1) * 16, :] * _gelu(a_ref[ch * 16:(ch + 1) * 16, :]).astype(BF16)
                for ch in range(nch)]

    def slot_u(slot):
        return (slot % u_groups, slot // u_groups)

    def slot_v(slot):
        return (slot % v_groups, slot // v_groups)

    @pl.when(s == 0)
    def _():
        acc_ref[...] = jnp.zeros_like(acc_ref)
        chunks = gates(w0_ref)
        ties = {slot_u(7 * k): _zero_tile(chunks[k]) for k in range(nch)}
        a0_ref[...] = _dot_nt(_tied_operand(u_ref, ties), h2_ref[...])

    bufs = ((a0_ref, w0_ref), (a1_ref, w1_ref))
    for parity in range(2):
        @pl.when((s > 0) & (s < nblk) & (s % 2 == parity))
        def _():
            a_wr, w_wr = bufs[parity]
            a_rd, w_rd = bufs[1 - parity]
            p = gated(a_rd, w_rd)
            chunks = gates(w_wr)
            ties = {slot_u(8 * c + 4): _zero_tile(p[c]) for c in range(nch)}
            ties.update({slot_u(16 * k): _zero_tile(chunks[k]) for k in range(nch // 2)})
            a_wr[...] = _dot_nt(_tied_operand(u_ref, ties), h2_ref[...])
            ties = {slot_v(16 * (k - nch // 2)): _zero_tile(chunks[k]) for k in range(nch // 2, nch)}
            acc_ref[...] += _dot(_tied_operand(vt_ref, ties), jnp.concatenate(p, axis=0))

    @pl.when(s == nblk)
    def _():
        a_rd, w_rd = bufs[(nblk - 1) % 2]
        acc = acc_ref[...] + _dot(vt_ref[...], jnp.concatenate(gated(a_rd, w_rd), axis=0))
        x2 = x1_ref[...] + g2_ref[...] * acc.T
        if final:
            x2 = x2 * lax.rsqrt(jnp.mean(x2 * x2, axis=-1, keepdims=True) + EPS) * fg_ref[...]
        o_ref[...] = x2


def _peer_call(l, h2, u_bf, vt_bf, r1, e1, cnt, e0, x1, mods, final_g, *, n_ctx, lat_len, tm, final):
    n, d = x1.shape
    ib = 8
    eb = ib * PEER_KEYS
    nblk = u_bf.shape[1] // eb
    nb_ctx, bpl = n_ctx // tm, lat_len // tm
    grp = lambda i: _group_of_block(i, nb_ctx, bpl)
    once = dict(pipeline_mode=pl.Buffered(1))
    cur = lambda s: jnp.minimum(s, nblk - 1)
    prev = lambda s: jnp.maximum(s - 1, 0)
    in_specs = [
        pl.BlockSpec((tm, d), lambda i, s: (i, 0), **once),
        pl.BlockSpec((None, eb, d), lambda i, s: (l, cur(s), 0)),
        pl.BlockSpec((None, d, eb), lambda i, s: (l, 0, prev(s))),
        pl.BlockSpec((PEER_HEADS, PEER_KEYS, tm), lambda i, s: (0, 0, i), **once),
        pl.BlockSpec((PEER_HEADS, PEER_KEYS, tm), lambda i, s: (0, 0, i), **once),
        pl.BlockSpec((PEER_HEADS, ib, tm), lambda i, s: (0, cur(s), i)),
        pl.BlockSpec((PEER_HEADS, ib, tm), lambda i, s: (0, cur(s), i)),
        pl.BlockSpec((tm, d), lambda i, s: (i, 0), **once),
        pl.BlockSpec((None, None, None, 1, d), lambda i, s: (l, grp(i), 5, 0, 0)),
    ]
    args = [h2, u_bf, vt_bf, r1, e1, cnt, e0, x1, mods]
    if final:
        in_specs.append(pl.BlockSpec((1, d), lambda i, s: (0, 0)))
        args.append(final_g.reshape(1, d))
    nstage = 2 * PEER_HEADS
    return pl.pallas_call(
        functools.partial(_peer_kernel, final=final, ib=ib, nblk=nblk),
        out_shape=jax.ShapeDtypeStruct((n, d), F32),
        grid=(n // tm, nblk + 1),
        in_specs=in_specs,
        out_specs=pl.BlockSpec((tm, d), lambda i, s: (i, 0)),
        scratch_shapes=[pltpu.VMEM((d, tm), F32),
                        pltpu.VMEM((eb, tm), F32), pltpu.VMEM((eb, tm), F32),
                        pltpu.VMEM((eb, tm), BF16), pltpu.VMEM((eb, tm), BF16),
                        pltpu.VMEM((nstage, 16, tm), BF16), pltpu.VMEM((nstage, 16, tm), BF16)],
        compiler_params=_cparams(("parallel", "arbitrary")),
        name="peer_experts",
    )(*args)


def kernel(x_prompt, x_sample, cache_k_win, cache_v_win, cache_k_axial, cache_v_axial, cache_k_diff,
           cache_v_diff, c, c_ctx, norm1_g, w_ada, b_ada, w_in, win_sink, ax_q_g, ax_k_g, lam_q1, lam_k1,
           lam_q2, lam_k2, diff_subln_g, w_out, norm2_g, peer_wq, peer_keys, peer_u, peer_v, final_g):
    batch, seq, d = x_prompt.shape
    dec_batch, lat_len, _ = x_sample.shape
    depth = w_in.shape[0]
    past = cache_k_win.shape[2]
    n_ctx = batch * seq
    n_lat = dec_batch * lat_len
    n = n_ctx + n_lat
    tm = 512
    tq_ctx = seq
    tq_lat = 256

    x = jnp.concatenate([x_prompt.reshape(n_ctx, d), x_sample.reshape(n_lat, d)], axis=0)
    cond = jnp.zeros((8, d), F32).at[0].set(c_ctx).at[1:1 + dec_batch].set(c)
    mods = _ada_call(cond, w_ada, b_ada).reshape(depth, 8, 6, 1, d)
    kw = _kw_call(peer_keys, peer_wq)
    w_in_bf = w_in.astype(BF16)
    w_out_bf = w_out.astype(BF16)
    u_bf = peer_u.astype(BF16)
    vt_bf = _vt_call(peer_v)
    tables = _rope_tables(lat_len)

    cache_a = (cache_k_win.reshape(dec_batch, depth, past, -1), cache_v_win.reshape(dec_batch, depth, past, -1))
    cache_b = (cache_k_axial.reshape(dec_batch, depth, past, -1), cache_v_axial.reshape(dec_batch, depth, past, -1))
    cache_c = (cache_k_diff.reshape(dec_batch, depth, past, -1), cache_v_diff.reshape(dec_batch, depth, past, -1))
    lam_params = (lam_q1, lam_k1, lam_q2, lam_k2)

    o = jnp.zeros((n, d), BF16)
    z = jnp.zeros((n, IN_COLS), BF16)
    kv_out = [[] for _ in range(6)]
    for l in range(depth):
        lam_init = 0.8 - 0.6 * math.exp(-0.3 * l)
        geo = dict(n_ctx=n_ctx, lat_len=lat_len, tm=tm)
        res = _in_call(l, x, mods, norm1_g, w_in_bf, ax_q_g, ax_k_g, None, z, latent=False, **geo)
        z = res[0]
        for k in range(6):
            kv_out[k].append(res[1 + k])
        z = _in_call(l, x, mods, norm1_g, w_in_bf, ax_q_g, ax_k_g, tables, z, latent=True, **geo)

        ctx = dict(nbatch=batch, seq=seq, row0=0, tq=tq_ctx)
        lat = dict(nbatch=dec_batch, seq=lat_len, row0=n_ctx, tq=tq_lat)
        win = dict(qcol=QA, kcol=KA, vcol=VA, ocol=0, heads=WIN_HEADS, kvh=WIN_KV, sink=win_sink)
        axl = dict(qcol=QB, kcol=KB, vcol=VB, ocol=WIN_HEADS * HEAD_DIM, heads=AX_HEADS, kvh=AX_KV)
        o = _gqa_call(l, z, o, **win, **ctx)
        o = _gqa_call(l, z, o, **axl, **ctx)
        o = _diff_call(l, z, o, lam_params, diff_subln_g, lam_init=lam_init, **ctx)
        o = _gqa_call(l, z, o, **win, **lat, window=True, cache=cache_a)
        o = _gqa_call(l, z, o, **axl, **lat, cache=cache_b)
        o = _diff_call(l, z, o, lam_params, diff_subln_g, lam_init=lam_init, cache=cache_c,
                       **dict(lat, tq=2 * tq_lat if lat_len % (2 * tq_lat) == 0 else tq_lat))

        x1, h2 = _out_call(l, o, x, w_out_bf, mods, norm2_g, **geo)
        s_t = _score_call(l, kw, h2, tm=tm)
        r1, e1, cnt, e0 = _router_call(s_t)
        x = _peer_call(l, h2, u_bf, vt_bf, r1, e1, cnt, e0, x1, mods, final_g,
                       final=(l == depth - 1), **geo)

    y_prompt = x[:n_ctx].reshape(batch, seq, d)
    y_sample = x[n_ctx:].reshape(dec_batch, lat_len, d)
    outs = []
    for k, nh in enumerate((WIN_KV, WIN_KV, AX_KV, AX_KV, DIFF_HEADS, DIFF_HEADS)):
        outs.append(jnp.stack([a.reshape(batch, seq, nh, HEAD_DIM) for a in kv_out[k]], axis=1))
    return (y_prompt, y_sample, *outs)
```

```python
import functools
import math

import jax
import jax.numpy as jnp
from jax import lax
from jax.experimental import pallas as pl
from jax.experimental.pallas import tpu as pltpu

F32 = jnp.float32
BF16 = jnp.bfloat16

HEAD_DIM = 128
GRID_W = 64
WINDOW = 128
ROPE_THETA = 10000.0
PEER_HEADS = 8
PEER_KEYS = 128
PEER_TOPK = 16
EPS = 1e-6
NEG_INF = -1e30
LOG2E = 1.4426950408889634

QA, KA, VA, QB, KB, VB, QC, KC, VC = 0, 512, 768, 1024, 2048, 2304, 2560, 3072, 3584
IN_COLS = 4096
WIN_HEADS, WIN_KV, AX_HEADS, AX_KV, DIFF_HEADS = 4, 2, 8, 2, 4

VMEM_LIMIT = 56 * 1024 * 1024


def _cparams(sem, **kw):
    return pltpu.CompilerParams(dimension_semantics=sem, vmem_limit_bytes=VMEM_LIMIT, **kw)


def _dot_nt(a, b):
    return lax.dot_general(a, b, (((1,), (1,)), ((), ())), preferred_element_type=F32)


def _dot(a, b):
    return jnp.dot(a, b, preferred_element_type=F32)


def _ada_kernel(cond_ref, w_ref, b_ref, o_ref):
    c = cond_ref[...]
    s = (c * jax.nn.sigmoid(c)).astype(BF16)
    o_ref[...] = _dot(s, w_ref[...].astype(BF16)) + b_ref[...]


def _ada_call(cond, w_ada, b_ada):
    depth, d, n6 = w_ada.shape
    tn = 1024
    return pl.pallas_call(
        _ada_kernel,
        out_shape=jax.ShapeDtypeStruct((depth, 8, n6), F32),
        grid=(depth, n6 // tn),
        in_specs=[
            pl.BlockSpec((8, d), lambda l, j: (0, 0)),
            pl.BlockSpec((None, d, tn), lambda l, j: (l, 0, j)),
            pl.BlockSpec((None, 1, tn), lambda l, j: (l, 0, j)),
        ],
        out_specs=pl.BlockSpec((None, 8, tn), lambda l, j: (l, 0, j)),
        compiler_params=_cparams(("parallel", "parallel")),
        name="ada",
    )(cond, w_ada, b_ada.reshape(depth, 1, n6))


def _kw_kernel(keys_ref, wq_ref, o_ref):
    kw = lax.dot_general(keys_ref[...], wq_ref[...], (((1,), (1,)), ((), ())),
                         preferred_element_type=F32, precision=lax.Precision.HIGHEST)
    o_ref[...] = kw.astype(BF16)


def _kw_call(peer_keys, peer_wq):
    depth, d, nq = peer_wq.shape
    ngrp = PEER_HEADS * 2
    sub = PEER_KEYS
    return pl.pallas_call(
        _kw_kernel,
        out_shape=jax.ShapeDtypeStruct((depth, ngrp * sub, d), BF16),
        grid=(depth, ngrp),
        in_specs=[
            pl.BlockSpec((None, None, None, sub, sub), lambda l, g: (l, g // 2, g % 2, 0, 0)),
            pl.BlockSpec((None, d, sub), lambda l, g: (l, 0, g)),
        ],
        out_specs=pl.BlockSpec((None, sub, d), lambda l, g: (l, g, 0)),
        compiler_params=_cparams(("parallel", "parallel")),
        name="peer_kw",
    )(peer_keys, peer_wq)


def _vt_kernel(v_ref, o_ref):
    o_ref[...] = v_ref[...].T.astype(BF16)


def _vt_call(peer_v):
    depth, n_exp, d = peer_v.shape
    te = 512
    return pl.pallas_call(
        _vt_kernel,
        out_shape=jax.ShapeDtypeStruct((depth, d, n_exp), BF16),
        grid=(depth, n_exp // te),
        in_specs=[pl.BlockSpec((None, te, d), lambda l, j: (l, j, 0))],
        out_specs=pl.BlockSpec((None, d, te), lambda l, j: (l, 0, j)),
        compiler_params=_cparams(("parallel", "parallel")),
        name="peer_vt",
    )(peer_v)


def _rope_tables(n_tok):
    n_rows = n_tok // GRID_W
    row_pos = jnp.repeat(jnp.arange(n_rows, dtype=jnp.int32), GRID_W).astype(F32)
    col_pos = jnp.tile(jnp.arange(GRID_W, dtype=jnp.int32), n_rows).astype(F32)
    lane = jnp.arange(HEAD_DIM)

    def table(pair, use_row):
        freqs = ROPE_THETA ** (-jnp.arange(pair, dtype=F32) / pair)
        f = freqs[lane % pair]
        pos = jnp.where(use_row[None, :], row_pos[:, None], col_pos[:, None])
        ang = pos * f[None, :]
        first = (lane % (2 * pair)) < pair
        return jnp.cos(ang), jnp.where(first[None, :], -jnp.sin(ang), jnp.sin(ang))

    c_ax, s_ax = table(32, lane < 64)
    c_df, s_df = table(16, (lane % 64) < 32)
    return c_ax, s_ax, c_df, s_df


def _rope(y, c, s, pair):
    lane = lax.broadcasted_iota(jnp.int32, y.shape, 1)
    first = (lane % (2 * pair)) < pair
    partner = jnp.where(first, pltpu.roll(y, HEAD_DIM - pair, axis=1), pltpu.roll(y, pair, axis=1))
    return y * c + partner * s


def _head_norm(y, g):
    return y * lax.rsqrt(jnp.mean(y * y, axis=-1, keepdims=True) + EPS) * g


def _in_kernel(*refs, latent):
    if latent:
        (x_ref, sh_ref, sc_ref, g_ref, w_ref, axq_ref, axk_ref, ca_ref, sa_ref, cd_ref, sd_ref,
         _z_in, z_ref, h_ref) = refs
        ka_ref = va_ref = kb_ref = vb_ref = kc_ref = vc_ref = None
    else:
        (x_ref, sh_ref, sc_ref, g_ref, w_ref, axq_ref, axk_ref, _z_in,
         z_ref, ka_ref, va_ref, kb_ref, vb_ref, kc_ref, vc_ref, h_ref) = refs
    j = pl.program_id(1)

    @pl.when(j == 0)
    def _():
        x = x_ref[...]
        y = x * lax.rsqrt(jnp.mean(x * x, axis=-1, keepdims=True) + EPS) * g_ref[...]
        h_ref[...] = (y * (1.0 + sc_ref[...]) + sh_ref[...]).astype(BF16)

    def rope_ax(y):
        return _rope(y, ca_ref[...], sa_ref[...], 32) if latent else y

    def rope_df(y):
        return _rope(y, cd_ref[...], sd_ref[...], 16) if latent else y

    def head(z, k):
        return z[:, k * HEAD_DIM:(k + 1) * HEAD_DIM]

    def put(k, y):
        z_ref[:, k * HEAD_DIM:(k + 1) * HEAD_DIM] = y.astype(BF16)

    def keep(ref, k, y):
        if not latent:
            ref[:, k * HEAD_DIM:(k + 1) * HEAD_DIM] = y

    @pl.when(j == 0)
    def _():
        z = _dot(h_ref[...], w_ref[...])
        for k in range(4):
            put(k, rope_ax(head(z, k)))
        for k in range(2):
            y = head(z, 4 + k)
            keep(ka_ref, k, y)
            put(4 + k, rope_ax(y))
        for k in range(2):
            y = head(z, 6 + k)
            keep(va_ref, k, y)
            put(6 + k, y)

    @pl.when(j == 1)
    def _():
        z = _dot(h_ref[...], w_ref[...])
        for k in range(8):
            put(k, rope_ax(_head_norm(head(z, k), axq_ref[...])))

    @pl.when(j == 2)
    def _():
        z = _dot(h_ref[...], w_ref[...])
        for k in range(2):
            y = _head_norm(head(z, k), axk_ref[...])
            keep(kb_ref, k, y)
            put(k, rope_ax(y))
        for k in range(2):
            y = head(z, 2 + k)
            keep(vb_ref, k, y)
            put(2 + k, y)
        for k in range(4):
            put(4 + k, rope_df(head(z, 4 + k)))

    @pl.when(j == 3)
    def _():
        z = _dot(h_ref[...], w_ref[...])
        for k in range(4):
            y = head(z, k)
            keep(kc_ref, k, y)
            put(k, rope_df(y))
        for k in range(4):
            y = head(z, 4 + k)
            keep(vc_ref, k, y)
            put(4 + k, y)


def _in_call(l, x, mods, norm1_g, w_in, ax_q_g, ax_k_g, tables, z, *, latent, n_ctx, lat_len, tm):
    n, d = x.shape
    depth = w_in.shape[0]
    nb_ctx = n_ctx // tm
    bpl = lat_len // tm
    if latent:
        nblk = (n - n_ctx) // tm
        row = lambda i: i + nb_ctx
        grp = lambda i: 1 + i // bpl
    else:
        nblk = nb_ctx
        row = lambda i: i
        grp = lambda i: 0
    cb = 1024
    in_specs = [
        pl.BlockSpec((tm, d), lambda i, j: (row(i), 0)),
        pl.BlockSpec((None, None, None, 1, d), lambda i, j: (l, grp(i), 0, 0, 0)),
        pl.BlockSpec((None, None, None, 1, d), lambda i, j: (l, grp(i), 1, 0, 0)),
        pl.BlockSpec((None, 1, d), lambda i, j: (l, 0, 0)),
        pl.BlockSpec((None, d, cb), lambda i, j: (l, 0, j)),
        pl.BlockSpec((None, 1, HEAD_DIM), lambda i, j: (l, 0, 0)),
        pl.BlockSpec((None, 1, HEAD_DIM), lambda i, j: (l, 0, 0)),
    ]
    args = [x, mods, mods, norm1_g.reshape(depth, 1, d), w_in,
            ax_q_g.reshape(depth, 1, HEAD_DIM), ax_k_g.reshape(depth, 1, HEAD_DIM)]
    z_spec = pl.BlockSpec((tm, cb), lambda i, j: (row(i), j))
    z_shape = jax.ShapeDtypeStruct((n, IN_COLS), BF16)
    scratch = [pltpu.VMEM((tm, d), BF16)]
    if latent:
        tspec = pl.BlockSpec((tm, HEAD_DIM), lambda i, j: (i % bpl, 0))
        in_specs += [tspec] * 4 + [pl.BlockSpec(memory_space=pl.ANY)]
        args += list(tables) + [z]
        return pl.pallas_call(
            functools.partial(_in_kernel, latent=True),
            out_shape=z_shape, grid=(nblk, 4), in_specs=in_specs, out_specs=z_spec,
            scratch_shapes=scratch, input_output_aliases={len(args) - 1: 0},
            compiler_params=_cparams(("parallel", "arbitrary")), name="in_proj_latent",
        )(*args)
    kv2 = jax.ShapeDtypeStruct((n_ctx, 2 * HEAD_DIM), F32)
    kv4 = jax.ShapeDtypeStruct((n_ctx, 4 * HEAD_DIM), F32)
    s2 = pl.BlockSpec((tm, 2 * HEAD_DIM), lambda i, j: (i, 0))
    s4 = pl.BlockSpec((tm, 4 * HEAD_DIM), lambda i, j: (i, 0))
    in_specs.append(pl.BlockSpec(memory_space=pl.ANY))
    args.append(z)
    return pl.pallas_call(
        functools.partial(_in_kernel, latent=False),
        out_shape=(z_shape, kv2, kv2, kv2, kv2, kv4, kv4), grid=(nblk, 4), in_specs=in_specs,
        out_specs=(z_spec, s2, s2, s2, s2, s4, s4), scratch_shapes=scratch,
        input_output_aliases={len(args) - 1: 0},
        compiler_params=_cparams(("parallel", "arbitrary")), name="in_proj_context",
    )(*args)


def _gqa_kernel(*refs, groups, tq, window, has_cache, has_sink, l, lat_len):
    refs = list(refs)
    sink_ref = refs.pop(0) if has_sink else None
    q_ref, k_ref, v_ref = refs[:3]
    refs = refs[3:]
    if has_cache:
        ck_ref, cv_ref = refs[:2]
        refs = refs[2:]
    o_ref = refs[-1]
    kv = pl.program_id(1)
    qi = pl.program_id(2)
    scale = HEAD_DIM ** -0.5

    if window:
        span = tq + 2 * WINDOW
        start = jnp.clip(qi * tq - WINDOW, 0, lat_len - span)
        start = pl.multiple_of(start, WINDOW)
        k = k_ref[pl.ds(start, span), :]
        v = v_ref[pl.ds(start, span), :]
        qpos = qi * tq + lax.broadcasted_iota(jnp.int32, (tq, span), 0)
        kpos = start + lax.broadcasted_iota(jnp.int32, (tq, span), 1)
        ok = jnp.abs(qpos - kpos) <= WINDOW
    else:
        k = k_ref[...]
        v = v_ref[...]
    if has_cache:
        ck = ck_ref[...].astype(BF16)
        cv = cv_ref[...].astype(BF16)

    c2 = scale * LOG2E
    for g in range(groups):
        q = q_ref[:, g * HEAD_DIM:(g + 1) * HEAD_DIM]
        s = _dot_nt(q, k)
        if window:
            s = jnp.where(ok, s, NEG_INF)
        m = jnp.max(s, axis=-1, keepdims=True)
        if has_cache:
            s2 = _dot_nt(q, ck)
            m = jnp.maximum(m, jnp.max(s2, axis=-1, keepdims=True))
        if has_sink:
            sink = sink_ref[l, kv * groups + g] * (1.0 / scale)
            m = jnp.maximum(m, sink)
        e = jnp.exp2((s - m) * c2)
        den = jnp.sum(e, axis=-1, keepdims=True)
        o = _dot(e.astype(BF16), v)
        if has_cache:
            e2 = jnp.exp2((s2 - m) * c2)
            den = den + jnp.sum(e2, axis=-1, keepdims=True)
            o = o + _dot(e2.astype(BF16), cv)
        if has_sink:
            den = den + jnp.exp2((sink - m) * c2)
        o_ref[:, g * HEAD_DIM:(g + 1) * HEAD_DIM] = (o / den).astype(BF16)


def _gqa_call(l, z, o, *, qcol, kcol, vcol, ocol, heads, kvh, nbatch, seq, row0, tq,
              window=False, cache=None, sink=None):
    groups = heads // kvh
    gw = groups * HEAD_DIM
    nq = seq // tq
    qrow = lambda b, qi: row0 // tq + b * nq + qi
    krow = lambda b: row0 // seq + b
    in_specs, args = [], []
    if sink is not None:
        in_specs.append(pl.BlockSpec(memory_space=pltpu.SMEM))
        args.append(sink)
    in_specs += [
        pl.BlockSpec((tq, gw), lambda b, kv, qi: (qrow(b, qi), qcol // gw + kv)),
        pl.BlockSpec((seq, HEAD_DIM), lambda b, kv, qi: (krow(b), kcol // HEAD_DIM + kv)),
        pl.BlockSpec((seq, HEAD_DIM), lambda b, kv, qi: (krow(b), vcol // HEAD_DIM + kv)),
    ]
    args += [z, z, z]
    if cache is not None:
        past = cache[0].shape[2]
        cspec = pl.BlockSpec((None, None, past, HEAD_DIM), lambda b, kv, qi: (b, l, 0, kv))
        in_specs += [cspec, cspec]
        args += list(cache)
    in_specs.append(pl.BlockSpec(memory_space=pl.ANY))
    args.append(o)
    kern = functools.partial(_gqa_kernel, groups=groups, tq=tq, window=window,
                             has_cache=cache is not None, has_sink=sink is not None, l=l, lat_len=seq)
    return pl.pallas_call(
        kern, out_shape=jax.ShapeDtypeStruct(o.shape, o.dtype), grid=(nbatch, kvh, nq),
        in_specs=in_specs,
        out_specs=pl.BlockSpec((tq, gw), lambda b, kv, qi: (qrow(b, qi), ocol // gw + kv)),
        input_output_aliases={len(args) - 1: 0},
        compiler_params=_cparams(("parallel", "parallel", "arbitrary")),
        name="attn_window" if sink is not None else "attn_axial",
    )(*args)


def _diff_kernel(*refs, has_cache, lam_init, chain_rows):
    refs = list(refs)
    lq1, lk1, lq2, lk2, gain_ref, q_ref, k_ref, v_ref = refs[:8]
    refs = refs[8:]
    if has_cache:
        ck_ref, cv_ref = refs[:2]
    o_ref = refs[-1]
    h = pl.program_id(1)
    half = HEAD_DIM // 2
    scale = half ** -0.5

    lam = (jnp.exp(jnp.sum(lq1[...] * lk1[...], axis=-1, keepdims=True))
           - jnp.exp(jnp.sum(lq2[...] * lk2[...], axis=-1, keepdims=True)) + lam_init)

    k = k_ref[...]
    v = v_ref[...]
    lane = lax.broadcasted_iota(jnp.int32, (1, HEAD_DIM), 1)
    lo = jnp.where(lane < half, 1.0, 0.0).astype(BF16)
    if has_cache:
        ck = ck_ref[...].astype(BF16)
        cv = cv_ref[...].astype(BF16)
    gain = gain_ref[pl.ds(h, 1), :]
    c2 = scale * LOG2E

    for r0 in range(0, q_ref.shape[0], chain_rows):
        q = q_ref[r0:r0 + chain_rows, :]
        qs = (q * lo, q * (1.0 - lo))
        probs = []
        for c in range(2):
            s = _dot_nt(qs[c], k)
            m = jnp.max(s, axis=-1, keepdims=True)
            if has_cache:
                s2 = _dot_nt(qs[c], ck)
                m = jnp.maximum(m, jnp.max(s2, axis=-1, keepdims=True))
            e = jnp.exp2((s - m) * c2)
            den = jnp.sum(e, axis=-1, keepdims=True)
            if has_cache:
                e2 = jnp.exp2((s2 - m) * c2)
                den = den + jnp.sum(e2, axis=-1, keepdims=True)
            else:
                e2 = None
            probs.append((e, e2, 1.0 / den))
        (e_a, e2_a, r_a), (e_b, e2_b, r_b) = probs
        r_b = lam * r_b
        o = _dot((e_a * r_a - e_b * r_b).astype(BF16), v)
        if has_cache:
            o = o + _dot((e2_a * r_a - e2_b * r_b).astype(BF16), cv)
        o = o * lax.rsqrt(jnp.mean(o * o, axis=-1, keepdims=True) + EPS) * gain
        o_ref[r0:r0 + chain_rows, :] = (o * (1.0 - lam_init)).astype(BF16)


def _diff_call(l, z, o, lam_params, gain, *, nbatch, seq, row0, tq, lam_init, cache=None):
    nq = seq // tq
    depth = gain.shape[0]
    half = HEAD_DIM // 2
    qrow = lambda b, qi: row0 // tq + b * nq + qi
    krow = lambda b: row0 // seq + b
    lspec = pl.BlockSpec((None, 1, half), lambda b, h, qi: (l, 0, 0))
    in_specs = [lspec] * 4 + [
        pl.BlockSpec((None, DIFF_HEADS, HEAD_DIM), lambda b, h, qi: (l, 0, 0)),
        pl.BlockSpec((tq, HEAD_DIM), lambda b, h, qi: (qrow(b, qi), QC // HEAD_DIM + h)),
        pl.BlockSpec((seq, HEAD_DIM), lambda b, h, qi: (krow(b), KC // HEAD_DIM + h)),
        pl.BlockSpec((seq, HEAD_DIM), lambda b, h, qi: (krow(b), VC // HEAD_DIM + h)),
    ]
    args = [p.reshape(depth, 1, half) for p in lam_params] + [gain, z, z, z]
    if cache is not None:
        past = cache[0].shape[2]
        cspec = pl.BlockSpec((None, None, past, HEAD_DIM), lambda b, h, qi: (b, l, 0, h))
        in_specs += [cspec, cspec]
        args += list(cache)
    in_specs.append(pl.BlockSpec(memory_space=pl.ANY))
    args.append(o)
    ocol0 = (WIN_HEADS + AX_HEADS)
    return pl.pallas_call(
        functools.partial(_diff_kernel, has_cache=cache is not None, lam_init=lam_init,
                          chain_rows=min(tq, 256)),
        out_shape=jax.ShapeDtypeStruct(o.shape, o.dtype), grid=(nbatch, DIFF_HEADS, nq),
        in_specs=in_specs,
        out_specs=pl.BlockSpec((tq, HEAD_DIM), lambda b, h, qi: (qrow(b, qi), ocol0 + h)),
        input_output_aliases={len(args) - 1: 0},
        compiler_params=_cparams(("parallel", "parallel", "arbitrary")),
        name="attn_diff",
    )(*args)


def _out_kernel(o_ref, x_ref, w_ref, g1_ref, sh2_ref, sc2_ref, n2g_ref, x1_ref, h2_ref):
    a = _dot(o_ref[...], w_ref[...])
    x1 = x_ref[...] + g1_ref[...] * a
    x1_ref[...] = x1
    y = x1 * lax.rsqrt(jnp.mean(x1 * x1, axis=-1, keepdims=True) + EPS) * n2g_ref[...]
    h2_ref[...] = (y * (1.0 + sc2_ref[...]) + sh2_ref[...]).astype(BF16)


def _group_of_block(i, nb_ctx, bpl):
    return jnp.where(i < nb_ctx, 0, 1 + (i - nb_ctx) // bpl)


def _out_call(l, o, x, w_out, mods, norm2_g, *, n_ctx, lat_len, tm):
    n, d = x.shape
    depth = w_out.shape[0]
    nb_ctx, bpl = n_ctx // tm, lat_len // tm
    grp = lambda i: _group_of_block(i, nb_ctx, bpl)
    mspec = lambda which: pl.BlockSpec((None, None, None, 1, d), lambda i: (l, grp(i), which, 0, 0))
    return pl.pallas_call(
        _out_kernel,
        out_shape=(jax.ShapeDtypeStruct((n, d), F32), jax.ShapeDtypeStruct((n, d), BF16)),
        grid=(n // tm,),
        in_specs=[
            pl.BlockSpec((tm, d), lambda i: (i, 0)),
            pl.BlockSpec((tm, d), lambda i: (i, 0)),
            pl.BlockSpec((None, d, d), lambda i: (l, 0, 0)),
            mspec(2), mspec(3), mspec(4),
            pl.BlockSpec((None, 1, d), lambda i: (l, 0, 0)),
        ],
        out_specs=(pl.BlockSpec((tm, d), lambda i: (i, 0)), pl.BlockSpec((tm, d), lambda i: (i, 0))),
        compiler_params=_cparams(("parallel",)),
        name="out_proj",
    )(o, x, w_out, mods, mods, mods, norm2_g.reshape(depth, 1, d))


def _sort_pairs(n):
    pairs = []

    def merge(lo, hi, r):
        step = r * 2
        if step < hi - lo:
            merge(lo, hi, step)
            merge(lo + r, hi, step)
            for i in range(lo + r, hi - r, step):
                pairs.append((i, i + r))
        else:
            pairs.append((lo, lo + r))

    def sort(lo, hi):
        if hi - lo >= 1:
            mid = lo + (hi - lo) // 2
            sort(lo, mid)
            sort(mid + 1, hi)
            merge(lo, hi, 1)

    sort(0, n - 1)
    return pairs


_SORT16 = _sort_pairs(PEER_TOPK)


def _vmax(a, b):
    if a is None:
        return b
    if b is None:
        return a
    return jnp.maximum(a, b)


def _vmin(a, b):
    if a is None or b is None:
        return None
    return jnp.minimum(a, b)


def _bitonic_merge_desc(x):
    x = list(x)
    d = PEER_TOPK // 2
    while d >= 1:
        for i in range(PEER_TOPK):
            if (i & d) == 0:
                a, b = x[i], x[i + d]
                x[i], x[i + d] = _vmax(a, b), _vmin(a, b)
        d //= 2
    return x


def _merge_top16(a, b):
    a = list(a) + [None] * (PEER_TOPK - len(a))
    b = list(b) + [None] * (PEER_TOPK - len(b))
    return _bitonic_merge_desc([_vmax(a[k], b[PEER_TOPK - 1 - k]) for k in range(PEER_TOPK)])


def _route_tile(s_ref, ln, r1_ref, e1_ref, cnt_ref, e0_ref, top_ref):
    nk = PEER_KEYS
    nv = nk // 8
    tk = PEER_TOPK
    sub = lax.broadcasted_iota(jnp.int32, (8, 128), 0)
    marks = []

    for g in range(2 * PEER_HEADS):
        x = [s_ref[g * nk + 8 * r:g * nk + 8 * r + 8, ln] for r in range(nv)]
        for (i, j) in _SORT16:
            x[i], x[j] = jnp.maximum(x[i], x[j]), jnp.minimum(x[i], x[j])
        for shift in (4, 2, 1):
            y = [pltpu.roll(v, shift, axis=0) for v in x]
            x = _bitonic_merge_desc([jnp.maximum(x[k], y[tk - 1 - k]) for k in range(tk)])
        for a in range(tk):
            top_ref[g * tk + a] = x[a]
        marks.append(x[tk - 1])

    def packed(c, a):
        out = top_ref[c * tk + a]
        for h in range(1, PEER_HEADS):
            out = jnp.where(sub == h, top_ref[(2 * h + c) * tk + a], out)
        return out

    p0 = [packed(0, a) for a in range(tk)]
    p1 = [packed(1, b) for b in range(tk)]
    rows = [[p0[a] + p1[b] for b in range(tk // (a + 1))] for a in range(8)]
    col0 = [p0[a] + p1[0] for a in range(8, tk)]
    t1 = _merge_top16(rows[0], _merge_top16(rows[1], col0))
    t2 = _merge_top16(_merge_top16(rows[2], rows[3]), _merge_top16(rows[4], rows[5]))
    t3 = _merge_top16(rows[6], rows[7])
    best = _merge_top16(t1, _merge_top16(t2, t3))
    tau_p = best[tk - 1]
    zsum = jnp.ones_like(tau_p)
    for k in range(1, tk):
        zsum = zsum + jnp.exp(best[k] - best[0])
    zinv_p = 1.0 / zsum

    cnt_p = []
    for a in range(tk):
        acc = jnp.zeros_like(tau_p)
        for b in range(tk):
            acc = acc + jnp.where(p0[a] + p1[b] >= tau_p, 1.0, 0.0)
        cnt_p.append(acc)

    def pair_words(v):
        bits = pltpu.bitcast(v, jnp.uint32)
        return bits | lax.shift_right_logical(bits, jnp.uint32(16))

    for h in range(PEER_HEADS):
        shape = (8, 128)
        zinv = jnp.broadcast_to(zinv_p[h:h + 1, :], shape)
        cnt_a = [jnp.broadcast_to(cnt_p[a][h:h + 1, :], shape) for a in range(tk)]
        t0 = [top_ref[(2 * h) * tk + a] for a in range(tk)]
        t1h = [top_ref[(2 * h + 1) * tk + b] for b in range(tk)]
        for rr in range(nv // 2):
            ranks, gates = [], []
            for r in (2 * rr, 2 * rr + 1):
                s0 = s_ref[(2 * h) * nk + 8 * r:(2 * h) * nk + 8 * r + 8, ln]
                s1 = s_ref[(2 * h + 1) * nk + 8 * r:(2 * h + 1) * nk + 8 * r + 8, ln]
                rank = jnp.full(shape, float(tk), F32)
                cnt = jnp.zeros(shape, F32)
                for b in range(tk - 1, -1, -1):
                    rank = jnp.where(s1 >= t1h[b], float(b), rank)
                    cnt = jnp.where(s0 >= t0[b], cnt_a[b], cnt)
                ranks.append(rank)
                gates.append(jnp.exp(s1 - t1h[0]))
                e0 = (jnp.exp(s0 - t0[0]) * zinv).astype(BF16).astype(F32)
                cnt_ref[h, 8 * r:8 * r + 8, ln] = pair_words(cnt)
                e0_ref[h, 8 * r:8 * r + 8, ln] = pair_words(e0)
            ranks = jnp.concatenate(ranks, axis=0).astype(BF16)
            r1_ref[h, 16 * rr:16 * rr + 16, ln] = ranks
            e1_ref[h, 16 * rr:16 * rr + 16, ln] = jnp.concatenate(gates, axis=0).astype(BF16)
            if rr % 2 == 1:
                marks.append(ranks)
    return marks


def _zero_tile32(v):
    bits = pltpu.bitcast(v, jnp.uint32)
    zero = lax.shift_right_logical(lax.shift_right_logical(bits, jnp.uint32(16)), jnp.uint32(16))
    return pltpu.bitcast(zero, BF16)


ROUTE_TIE_STRIDE = 5


def _score_route_kernel(kw_ref, h2_ref, r1_ref, e1_ref, cnt_ref, e0_ref, sa_ref, sb_ref, top_ref, *, nb):
    s = pl.program_id(0)
    tiles = h2_ref.shape[0] // 128
    groups = kw_ref.shape[0] // 16

    def route(sc_ref):
        ties, idx = {}, 0
        for lt in range(tiles):
            ln = slice(lt * 128, (lt + 1) * 128)
            for m in _route_tile(sc_ref, ln, r1_ref, e1_ref, cnt_ref, e0_ref, top_ref.at[lt]):
                slot = idx * ROUTE_TIE_STRIDE
                ties[(slot % groups, slot // groups)] = (
                    _zero_tile32(m) if m.dtype == F32 else _zero_tile(m))
                idx += 1
        return ties

    def scores(sc_ref, ties):
        sc_ref[...] = _dot_nt(_tied_operand(kw_ref, ties), h2_ref[...])

    @pl.when(s == 0)
    def _():
        scores(sa_ref, {})

    bufs = (sa_ref, sb_ref)
    for parity in range(2):
        @pl.when((s > 0) & (s < nb) & (s % 2 == parity))
        def _():
            scores(bufs[parity], route(bufs[1 - parity]))

    @pl.when(s == nb)
    def _():
        route(bufs[(nb - 1) % 2])


def _score_route_call(l, kw, h2, *, tm):
    n, d = h2.shape
    rows = kw.shape[1]
    nb = n // tm
    half = jax.ShapeDtypeStruct((PEER_HEADS, PEER_KEYS, n), BF16)
    words = jax.ShapeDtypeStruct((PEER_HEADS, PEER_KEYS, n), jnp.uint32)
    spec = pl.BlockSpec((PEER_HEADS, PEER_KEYS, tm), lambda s: (0, 0, jnp.maximum(s - 1, 0)))
    return pl.pallas_call(
        functools.partial(_score_route_kernel, nb=nb),
        out_shape=(half, half, words, words),
        grid=(nb + 1,),
        in_specs=[pl.BlockSpec((None, rows, d), lambda s: (l, 0, 0), pipeline_mode=pl.Buffered(1)),
                  pl.BlockSpec((tm, d), lambda s: (jnp.minimum(s, nb - 1), 0))],
        out_specs=(spec, spec, spec, spec),
        scratch_shapes=[pltpu.VMEM((rows, tm), F32), pltpu.VMEM((rows, tm), F32),
                        pltpu.VMEM((tm // 128, 2 * PEER_HEADS * PEER_TOPK, 8, 128), F32)],
        compiler_params=_cparams(("arbitrary",)),
        name="peer_route",
    )(kw, h2)


def _gelu(a):
    c0 = math.sqrt(2.0 / math.pi)
    half = 0.5 * a
    return half + half * jnp.tanh(a * (c0 + (c0 * 0.044715) * (a * a)))


def _zero_tile(chunk):
    bits = pltpu.bitcast(chunk[:, 0:128], jnp.uint32)
    for lt in range(1, chunk.shape[1] // 128):
        bits = bits | pltpu.bitcast(chunk[:, lt * 128:(lt + 1) * 128], jnp.uint32)
    zero = lax.shift_right_logical(lax.shift_right_logical(bits, jnp.uint32(16)), jnp.uint32(16))
    return pltpu.bitcast(zero, BF16)


def _tied_operand(ref, ties):
    row_groups = []
    for rg in range(ref.shape[0] // 16):
        cols = []
        for c in range(ref.shape[1] // 256):
            piece = ref[rg * 16:(rg + 1) * 16, c * 256:(c + 1) * 256]
            z = ties.get((rg, c))
            cols.append(piece if z is None else piece + jnp.concatenate([z, z], axis=1))
        row_groups.append(jnp.concatenate(cols, axis=1))
    return jnp.concatenate(row_groups, axis=0)


def _peer_kernel(*refs, final, ib, nblk):
    if final:
        (h2_ref, u_ref, vt_ref, r1_ref, e1_ref, cnt_ref, e0_ref, x1_ref, g2_ref, fg_ref,
         o_ref, acc_ref, a0_ref, a1_ref, w0_ref, w1_ref, cs_ref, es_ref) = refs
    else:
        (h2_ref, u_ref, vt_ref, r1_ref, e1_ref, cnt_ref, e0_ref, x1_ref, g2_ref,
         o_ref, acc_ref, a0_ref, a1_ref, w0_ref, w1_ref, cs_ref, es_ref) = refs
    s = pl.program_id(1)
    t = h2_ref.shape[0]
    nchunk = PEER_KEYS // 16
    nch = ib * nchunk
    u_groups = u_ref.shape[0] // 16
    v_groups = vt_ref.shape[0] // 16

    def gates(w_ref):
        chunks = []
        for ii in range(ib):
            for h in range(PEER_HEADS):
                slot = (ii % 2) * PEER_HEADS + h
                cs_ref[slot] = pltpu.bitcast(jnp.broadcast_to(cnt_ref[h, ii:ii + 1, :], (8, t)), BF16)
                es_ref[slot] = pltpu.bitcast(jnp.broadcast_to(e0_ref[h, ii:ii + 1, :], (8, t)), BF16)
            w = [None] * nchunk
            for h in range(PEER_HEADS):
                cnt = cs_ref[(ii % 2) * PEER_HEADS + h]
                e0 = es_ref[(ii % 2) * PEER_HEADS + h]
                for jc in range(nchunk):
                    r1 = r1_ref[h, jc * 16:(jc + 1) * 16, :]
                    e1 = e1_ref[h, jc * 16:(jc + 1) * 16, :]
                    gate = jnp.where(r1 < cnt, e0, jnp.zeros_like(e0)) * e1
                    w[jc] = gate if w[jc] is None else w[jc] + gate
            for jc in range(nchunk):
                r0 = ii * PEER_KEYS + jc * 16
                w_ref[r0:r0 + 16, :] = w[jc]
                chunks.append(w[jc])
        return chunks

    def gated(a_ref, w_ref):
        return [w_ref[ch * 16:(ch + 1) * 16, :] * _gelu(a_ref[ch * 16:(ch + 1) * 16, :]).astype(BF16)
                for ch in range(nch)]

    def slot_u(slot):
        return (slot % u_groups, slot // u_groups)

    def slot_v(slot):
        return (slot % v_groups, slot // v_groups)

    @pl.when(s == 0)
    def _():
        acc_ref[...] = jnp.zeros_like(acc_ref)
        chunks = gates(w0_ref)
        ties = {slot_u(7 * k): _zero_tile(chunks[k]) for k in range(nch)}
        a0_ref[...] = _dot_nt(_tied_operand(u_ref, ties), h2_ref[...])

    bufs = ((a0_ref, w0_ref), (a1_ref, w1_ref))
    for parity in range(2):
        @pl.when((s > 0) & (s < nblk) & (s % 2 == parity))
        def _():
            a_wr, w_wr = bufs[parity]
            a_rd, w_rd = bufs[1 - parity]
            p = gated(a_rd, w_rd)
            chunks = gates(w_wr)
            ties = {slot_u(8 * c + 4): _zero_tile(p[c]) for c in range(nch)}
            ties.update({slot_u(16 * k): _zero_tile(chunks[k]) for k in range(nch // 2)})
            a_wr[...] = _dot_nt(_tied_operand(u_ref, ties), h2_ref[...])
            ties = {slot_v(16 * (k - nch // 2)): _zero_tile(chunks[k]) for k in range(nch // 2, nch)}
            acc_ref[...] += _dot(_tied_operand(vt_ref, ties), jnp.concatenate(p, axis=0))

    @pl.when(s == nblk)
    def _():
        a_rd, w_rd = bufs[(nblk - 1) % 2]
        acc = acc_ref[...] + _dot(vt_ref[...], jnp.concatenate(gated(a_rd, w_rd), axis=0))
        x2 = x1_ref[...] + g2_ref[...] * acc.T
        if final:
            x2 = x2 * lax.rsqrt(jnp.mean(x2 * x2, axis=-1, keepdims=True) + EPS) * fg_ref[...]
        o_ref[...] = x2


def _peer_call(l, h2, u_bf, vt_bf, r1, e1, cnt, e0, x1, mods, final_g, *, n_ctx, lat_len, tm, final):
    n, d = x1.shape
    ib = 8
    eb = ib * PEER_KEYS
    nblk = u_bf.shape[1] // eb
    nb_ctx, bpl = n_ctx // tm, lat_len // tm
    grp = lambda i: _group_of_block(i, nb_ctx, bpl)
    once = dict(pipeline_mode=pl.Buffered(1))
    cur = lambda s: jnp.minimum(s, nblk - 1)
    prev = lambda s: jnp.maximum(s - 1, 0)
    in_specs = [
        pl.BlockSpec((tm, d), lambda i, s: (i, 0), **once),
        pl.BlockSpec((None, eb, d), lambda i, s: (l, cur(s), 0)),
        pl.BlockSpec((None, d, eb), lambda i, s: (l, 0, prev(s))),
        pl.BlockSpec((PEER_HEADS, PEER_KEYS, tm), lambda i, s: (0, 0, i), **once),
        pl.BlockSpec((PEER_HEADS, PEER_KEYS, tm), lambda i, s: (0, 0, i), **once),
        pl.BlockSpec((PEER_HEADS, ib, tm), lambda i, s: (0, cur(s), i)),
        pl.BlockSpec((PEER_HEADS, ib, tm), lambda i, s: (0, cur(s), i)),
        pl.BlockSpec((tm, d), lambda i, s: (i, 0), **once),
        pl.BlockSpec((None, None, None, 1, d), lambda i, s: (l, grp(i), 5, 0, 0)),
    ]
    args = [h2, u_bf, vt_bf, r1, e1, cnt, e0, x1, mods]
    if final:
        in_specs.append(pl.BlockSpec((1, d), lambda i, s: (0, 0)))
        args.append(final_g.reshape(1, d))
    nstage = 2 * PEER_HEADS
    return pl.pallas_call(
        functools.partial(_peer_kernel, final=final, ib=ib, nblk=nblk),
        out_shape=jax.ShapeDtypeStruct((n, d), F32),
        grid=(n // tm, nblk + 1),
        in_specs=in_specs,
        out_specs=pl.BlockSpec((tm, d), lambda i, s: (i, 0)),
        scratch_shapes=[pltpu.VMEM((d, tm), F32),
                        pltpu.VMEM((eb, tm), F32), pltpu.VMEM((eb, tm), F32),
                        pltpu.VMEM((eb, tm), BF16), pltpu.VMEM((eb, tm), BF16),
                        pltpu.VMEM((nstage, 16, tm), BF16), pltpu.VMEM((nstage, 16, tm), BF16)],
        compiler_params=_cparams(("parallel", "arbitrary")),
        name="peer_experts",
    )(*args)


def kernel(x_prompt, x_sample, cache_k_win, cache_v_win, cache_k_axial, cache_v_axial, cache_k_diff,
           cache_v_diff, c, c_ctx, norm1_g, w_ada, b_ada, w_in, win_sink, ax_q_g, ax_k_g, lam_q1, lam_k1,
           lam_q2, lam_k2, diff_subln_g, w_out, norm2_g, peer_wq, peer_keys, peer_u, peer_v, final_g):
    batch, seq, d = x_prompt.shape
    dec_batch, lat_len, _ = x_sample.shape
    depth = w_in.shape[0]
    past = cache_k_win.shape[2]
    n_ctx = batch * seq
    n_lat = dec_batch * lat_len
    n = n_ctx + n_lat
    tm = 512
    tq_ctx = seq
    tq_lat = 256

    x = jnp.concatenate([x_prompt.reshape(n_ctx, d), x_sample.reshape(n_lat, d)], axis=0)
    cond = jnp.zeros((8, d), F32).at[0].set(c_ctx).at[1:1 + dec_batch].set(c)
    mods = _ada_call(cond, w_ada, b_ada).reshape(depth, 8, 6, 1, d)
    kw = _kw_call(peer_keys, peer_wq)
    w_in_bf = w_in.astype(BF16)
    w_out_bf = w_out.astype(BF16)
    u_bf = peer_u.astype(BF16)
    vt_bf = _vt_call(peer_v)
    tables = _rope_tables(lat_len)

    cache_a = (cache_k_win.reshape(dec_batch, depth, past, -1), cache_v_win.reshape(dec_batch, depth, past, -1))
    cache_b = (cache_k_axial.reshape(dec_batch, depth, past, -1), cache_v_axial.reshape(dec_batch, depth, past, -1))
    cache_c = (cache_k_diff.reshape(dec_batch, depth, past, -1), cache_v_diff.reshape(dec_batch, depth, past, -1))
    lam_params = (lam_q1, lam_k1, lam_q2, lam_k2)

    o = jnp.zeros((n, d), BF16)
    z = jnp.zeros((n, IN_COLS), BF16)
    kv_out = [[] for _ in range(6)]
    for l in range(depth):
        lam_init = 0.8 - 0.6 * math.exp(-0.3 * l)
        geo = dict(n_ctx=n_ctx, lat_len=lat_len, tm=tm)
        res = _in_call(l, x, mods, norm1_g, w_in_bf, ax_q_g, ax_k_g, None, z, latent=False, **geo)
        z = res[0]
        for k in range(6):
            kv_out[k].append(res[1 + k])
        z = _in_call(l, x, mods, norm1_g, w_in_bf, ax_q_g, ax_k_g, tables, z, latent=True, **geo)

        ctx = dict(nbatch=batch, seq=seq, row0=0, tq=tq_ctx)
        lat = dict(nbatch=dec_batch, seq=lat_len, row0=n_ctx, tq=tq_lat)
        win = dict(qcol=QA, kcol=KA, vcol=VA, ocol=0, heads=WIN_HEADS, kvh=WIN_KV, sink=win_sink)
        axl = dict(qcol=QB, kcol=KB, vcol=VB, ocol=WIN_HEADS * HEAD_DIM, heads=AX_HEADS, kvh=AX_KV)
        o = _gqa_call(l, z, o, **win, **ctx)
        o = _gqa_call(l, z, o, **axl, **ctx)
        o = _diff_call(l, z, o, lam_params, diff_subln_g, lam_init=lam_init, **ctx)
        o = _gqa_call(l, z, o, **win, **lat, window=True, cache=cache_a)
        o = _gqa_call(l, z, o, **axl, **lat, cache=cache_b)
        o = _diff_call(l, z, o, lam_params, diff_subln_g, lam_init=lam_init, cache=cache_c,
                       **dict(lat, tq=2 * tq_lat if lat_len % (2 * tq_lat) == 0 else tq_lat))

        x1, h2 = _out_call(l, o, x, w_out_bf, mods, norm2_g, **geo)
        r1, e1, cnt, e0 = _score_route_call(l, kw, h2, tm=tm)
        x = _peer_call(l, h2, u_bf, vt_bf, r1, e1, cnt, e0, x1, mods, final_g,
                       final=(l == depth - 1), **geo)

    y_prompt = x[:n_ctx].reshape(batch, seq, d)
    y_sample = x[n_ctx:].reshape(dec_batch, lat_len, d)
    outs = []
    for k, nh in enumerate((WIN_KV, WIN_KV, AX_KV, AX_KV, DIFF_HEADS, DIFF_HEADS)):
        outs.append(jnp.stack([a.reshape(batch, seq, nh, HEAD_DIM) for a in kv_out[k]], axis=1))
    return (y_prompt, y_sample, *outs)
```

```python
import functools
import math

import jax
import jax.numpy as jnp
from jax import lax
from jax.experimental import pallas as pl
from jax.experimental.pallas import tpu as pltpu

F32 = jnp.float32
BF16 = jnp.bfloat16

HEAD_DIM = 128
GRID_W = 64
WINDOW = 128
ROPE_THETA = 10000.0
PEER_HEADS = 8
PEER_KEYS = 128
PEER_TOPK = 16
EPS = 1e-6
NEG_INF = -1e30
LOG2E = 1.4426950408889634

QA, KA, VA, QB, KB, VB, QC, KC, VC = 0, 512, 768, 1024, 2048, 2304, 2560, 3072, 3584
IN_COLS = 4096
WIN_HEADS, WIN_KV, AX_HEADS, AX_KV, DIFF_HEADS = 4, 2, 8, 2, 4

VMEM_LIMIT = 56 * 1024 * 1024


def _cparams(sem, **kw):
    return pltpu.CompilerParams(dimension_semantics=sem, vmem_limit_bytes=VMEM_LIMIT, **kw)


def _dot_nt(a, b):
    return lax.dot_general(a, b, (((1,), (1,)), ((), ())), preferred_element_type=F32)


def _dot(a, b):
    return jnp.dot(a, b, preferred_element_type=F32)


def _ada_kernel(cond_ref, w_ref, b_ref, o_ref):
    c = cond_ref[...]
    s = (c * jax.nn.sigmoid(c)).astype(BF16)
    o_ref[...] = _dot(s, w_ref[...].astype(BF16)) + b_ref[...]


def _ada_call(cond, w_ada, b_ada):
    depth, d, n6 = w_ada.shape
    tn = 1024
    return pl.pallas_call(
        _ada_kernel,
        out_shape=jax.ShapeDtypeStruct((depth, 8, n6), F32),
        grid=(depth, n6 // tn),
        in_specs=[
            pl.BlockSpec((8, d), lambda l, j: (0, 0)),
            pl.BlockSpec((None, d, tn), lambda l, j: (l, 0, j)),
            pl.BlockSpec((None, 1, tn), lambda l, j: (l, 0, j)),
        ],
        out_specs=pl.BlockSpec((None, 8, tn), lambda l, j: (l, 0, j)),
        compiler_params=_cparams(("parallel", "parallel")),
        name="ada",
    )(cond, w_ada, b_ada.reshape(depth, 1, n6))


def _kw_kernel(keys_ref, wq_ref, o_ref):
    kw = lax.dot_general(keys_ref[...], wq_ref[...], (((1,), (1,)), ((), ())),
                         preferred_element_type=F32, precision=lax.Precision.HIGHEST)
    o_ref[...] = kw.astype(BF16)


def _kw_call(peer_keys, peer_wq):
    depth, d, nq = peer_wq.shape
    ngrp = PEER_HEADS * 2
    sub = PEER_KEYS
    return pl.pallas_call(
        _kw_kernel,
        out_shape=jax.ShapeDtypeStruct((depth, ngrp * sub, d), BF16),
        grid=(depth, ngrp),
        in_specs=[
            pl.BlockSpec((None, None, None, sub, sub), lambda l, g: (l, g // 2, g % 2, 0, 0)),
            pl.BlockSpec((None, d, sub), lambda l, g: (l, 0, g)),
        ],
        out_specs=pl.BlockSpec((None, sub, d), lambda l, g: (l, g, 0)),
        compiler_params=_cparams(("parallel", "parallel")),
        name="peer_kw",
    )(peer_keys, peer_wq)


def _vt_kernel(v_ref, o_ref):
    o_ref[...] = v_ref[...].T.astype(BF16)


def _vt_call(peer_v):
    depth, n_exp, d = peer_v.shape
    te = 512
    return pl.pallas_call(
        _vt_kernel,
        out_shape=jax.ShapeDtypeStruct((depth, d, n_exp), BF16),
        grid=(depth, n_exp // te),
        in_specs=[pl.BlockSpec((None, te, d), lambda l, j: (l, j, 0))],
        out_specs=pl.BlockSpec((None, d, te), lambda l, j: (l, 0, j)),
        compiler_params=_cparams(("parallel", "parallel")),
        name="peer_vt",
    )(peer_v)


def _rope_tables(n_tok):
    n_rows = n_tok // GRID_W
    row_pos = jnp.repeat(jnp.arange(n_rows, dtype=jnp.int32), GRID_W).astype(F32)
    col_pos = jnp.tile(jnp.arange(GRID_W, dtype=jnp.int32), n_rows).astype(F32)
    lane = jnp.arange(HEAD_DIM)

    def table(pair, use_row):
        freqs = ROPE_THETA ** (-jnp.arange(pair, dtype=F32) / pair)
        f = freqs[lane % pair]
        pos = jnp.where(use_row[None, :], row_pos[:, None], col_pos[:, None])
        ang = pos * f[None, :]
        first = (lane % (2 * pair)) < pair
        return jnp.cos(ang), jnp.where(first[None, :], -jnp.sin(ang), jnp.sin(ang))

    c_ax, s_ax = table(32, lane < 64)
    c_df, s_df = table(16, (lane % 64) < 32)
    return c_ax, s_ax, c_df, s_df


def _rope(y, c, s, pair):
    lane = lax.broadcasted_iota(jnp.int32, y.shape, 1)
    first = (lane % (2 * pair)) < pair
    partner = jnp.where(first, pltpu.roll(y, HEAD_DIM - pair, axis=1), pltpu.roll(y, pair, axis=1))
    return y * c + partner * s


def _head_norm(y, g):
    return y * lax.rsqrt(jnp.mean(y * y, axis=-1, keepdims=True) + EPS) * g


def _in_kernel(*refs, latent):
    if latent:
        (x_ref, sh_ref, sc_ref, g_ref, w_ref, axq_ref, axk_ref, ca_ref, sa_ref, cd_ref, sd_ref,
         _z_in, z_ref, h_ref) = refs
        ka_ref = va_ref = kb_ref = vb_ref = kc_ref = vc_ref = None
    else:
        (x_ref, sh_ref, sc_ref, g_ref, w_ref, axq_ref, axk_ref, _z_in,
         z_ref, ka_ref, va_ref, kb_ref, vb_ref, kc_ref, vc_ref, h_ref) = refs
    j = pl.program_id(1)

    @pl.when(j == 0)
    def _():
        x = x_ref[...]
        y = x * lax.rsqrt(jnp.mean(x * x, axis=-1, keepdims=True) + EPS) * g_ref[...]
        h_ref[...] = (y * (1.0 + sc_ref[...]) + sh_ref[...]).astype(BF16)

    def rope_ax(y):
        return _rope(y, ca_ref[...], sa_ref[...], 32) if latent else y

    def rope_df(y):
        return _rope(y, cd_ref[...], sd_ref[...], 16) if latent else y

    def head(z, k):
        return z[:, k * HEAD_DIM:(k + 1) * HEAD_DIM]

    def put(k, y):
        z_ref[:, k * HEAD_DIM:(k + 1) * HEAD_DIM] = y.astype(BF16)

    def keep(ref, k, y):
        if not latent:
            ref[:, k * HEAD_DIM:(k + 1) * HEAD_DIM] = y

    @pl.when(j == 0)
    def _():
        z = _dot(h_ref[...], w_ref[...])
        for k in range(4):
            put(k, rope_ax(head(z, k)))
        for k in range(2):
            y = head(z, 4 + k)
            keep(ka_ref, k, y)
            put(4 + k, rope_ax(y))
        for k in range(2):
            y = head(z, 6 + k)
            keep(va_ref, k, y)
            put(6 + k, y)

    @pl.when(j == 1)
    def _():
        z = _dot(h_ref[...], w_ref[...])
        for k in range(8):
            put(k, rope_ax(_head_norm(head(z, k), axq_ref[...])))

    @pl.when(j == 2)
    def _():
        z = _dot(h_ref[...], w_ref[...])
        for k in range(2):
            y = _head_norm(head(z, k), axk_ref[...])
            keep(kb_ref, k, y)
            put(k, rope_ax(y))
        for k in range(2):
            y = head(z, 2 + k)
            keep(vb_ref, k, y)
            put(2 + k, y)
        for k in range(4):
            put(4 + k, rope_df(head(z, 4 + k)))

    @pl.when(j == 3)
    def _():
        z = _dot(h_ref[...], w_ref[...])
        for k in range(4):
            y = head(z, k)
            keep(kc_ref, k, y)
            put(k, rope_df(y))
        for k in range(4):
            y = head(z, 4 + k)
            keep(vc_ref, k, y)
            put(4 + k, y)


def _in_call(l, x, mods, norm1_g, w_in, ax_q_g, ax_k_g, tables, z, *, latent, n_ctx, lat_len, tm):
    n, d = x.shape
    depth = w_in.shape[0]
    nb_ctx = n_ctx // tm
    bpl = lat_len // tm
    if latent:
        nblk = (n - n_ctx) // tm
        row = lambda i: i + nb_ctx
        grp = lambda i: 1 + i // bpl
    else:
        nblk = nb_ctx
        row = lambda i: i
        grp = lambda i: 0
    cb = 1024
    in_specs = [
        pl.BlockSpec((tm, d), lambda i, j: (row(i), 0)),
        pl.BlockSpec((None, None, None, 1, d), lambda i, j: (l, grp(i), 0, 0, 0)),
        pl.BlockSpec((None, None, None, 1, d), lambda i, j: (l, grp(i), 1, 0, 0)),
        pl.BlockSpec((None, 1, d), lambda i, j: (l, 0, 0)),
        pl.BlockSpec((None, d, cb), lambda i, j: (l, 0, j)),
        pl.BlockSpec((None, 1, HEAD_DIM), lambda i, j: (l, 0, 0)),
        pl.BlockSpec((None, 1, HEAD_DIM), lambda i, j: (l, 0, 0)),
    ]
    args = [x, mods, mods, norm1_g.reshape(depth, 1, d), w_in,
            ax_q_g.reshape(depth, 1, HEAD_DIM), ax_k_g.reshape(depth, 1, HEAD_DIM)]
    z_spec = pl.BlockSpec((tm, cb), lambda i, j: (row(i), j))
    z_shape = jax.ShapeDtypeStruct((n, IN_COLS), BF16)
    scratch = [pltpu.VMEM((tm, d), BF16)]
    if latent:
        tspec = pl.BlockSpec((tm, HEAD_DIM), lambda i, j: (i % bpl, 0))
        in_specs += [tspec] * 4 + [pl.BlockSpec(memory_space=pl.ANY)]
        args += list(tables) + [z]
        return pl.pallas_call(
            functools.partial(_in_kernel, latent=True),
            out_shape=z_shape, grid=(nblk, 4), in_specs=in_specs, out_specs=z_spec,
            scratch_shapes=scratch, input_output_aliases={len(args) - 1: 0},
            compiler_params=_cparams(("parallel", "arbitrary")), name="in_proj_latent",
        )(*args)
    kv2 = jax.ShapeDtypeStruct((n_ctx, 2 * HEAD_DIM), F32)
    kv4 = jax.ShapeDtypeStruct((n_ctx, 4 * HEAD_DIM), F32)
    s2 = pl.BlockSpec((tm, 2 * HEAD_DIM), lambda i, j: (i, 0))
    s4 = pl.BlockSpec((tm, 4 * HEAD_DIM), lambda i, j: (i, 0))
    in_specs.append(pl.BlockSpec(memory_space=pl.ANY))
    args.append(z)
    return pl.pallas_call(
        functools.partial(_in_kernel, latent=False),
        out_shape=(z_shape, kv2, kv2, kv2, kv2, kv4, kv4), grid=(nblk, 4), in_specs=in_specs,
        out_specs=(z_spec, s2, s2, s2, s2, s4, s4), scratch_shapes=scratch,
        input_output_aliases={len(args) - 1: 0},
        compiler_params=_cparams(("parallel", "arbitrary")), name="in_proj_context",
    )(*args)


def _gqa_kernel(*refs, groups, tq, window, has_cache, has_sink, l, lat_len):
    refs = list(refs)
    sink_ref = refs.pop(0) if has_sink else None
    q_ref, k_ref, v_ref = refs[:3]
    refs = refs[3:]
    if has_cache:
        ck_ref, cv_ref = refs[:2]
        refs = refs[2:]
    o_ref = refs[-1]
    kv = pl.program_id(1)
    qi = pl.program_id(2)
    scale = HEAD_DIM ** -0.5

    if window:
        span = tq + 2 * WINDOW
        start = jnp.clip(qi * tq - WINDOW, 0, lat_len - span)
        start = pl.multiple_of(start, WINDOW)
        k = k_ref[pl.ds(start, span), :]
        v = v_ref[pl.ds(start, span), :]
        qpos = qi * tq + lax.broadcasted_iota(jnp.int32, (tq, span), 0)
        kpos = start + lax.broadcasted_iota(jnp.int32, (tq, span), 1)
        ok = jnp.abs(qpos - kpos) <= WINDOW
    else:
        k = k_ref[...]
        v = v_ref[...]
    if has_cache:
        ck = ck_ref[...].astype(BF16)
        cv = cv_ref[...].astype(BF16)

    c2 = scale * LOG2E
    for g in range(groups):
        q = q_ref[:, g * HEAD_DIM:(g + 1) * HEAD_DIM]
        s = _dot_nt(q, k)
        if window:
            s = jnp.where(ok, s, NEG_INF)
        m = jnp.max(s, axis=-1, keepdims=True)
        if has_cache:
            s2 = _dot_nt(q, ck)
            m = jnp.maximum(m, jnp.max(s2, axis=-1, keepdims=True))
        if has_sink:
            sink = sink_ref[l, kv * groups + g] * (1.0 / scale)
            m = jnp.maximum(m, sink)
        e = jnp.exp2((s - m) * c2)
        den = jnp.sum(e, axis=-1, keepdims=True)
        o = _dot(e.astype(BF16), v)
        if has_cache:
            e2 = jnp.exp2((s2 - m) * c2)
            den = den + jnp.sum(e2, axis=-1, keepdims=True)
            o = o + _dot(e2.astype(BF16), cv)
        if has_sink:
            den = den + jnp.exp2((sink - m) * c2)
        o_ref[:, g * HEAD_DIM:(g + 1) * HEAD_DIM] = (o / den).astype(BF16)


def _gqa_call(l, z, o, *, qcol, kcol, vcol, ocol, heads, kvh, nbatch, seq, row0, tq,
              window=False, cache=None, sink=None):
    groups = heads // kvh
    gw = groups * HEAD_DIM
    nq = seq // tq
    qrow = lambda b, qi: row0 // tq + b * nq + qi
    krow = lambda b: row0 // seq + b
    in_specs, args = [], []
    if sink is not None:
        in_specs.append(pl.BlockSpec(memory_space=pltpu.SMEM))
        args.append(sink)
    in_specs += [
        pl.BlockSpec((tq, gw), lambda b, kv, qi: (qrow(b, qi), qcol // gw + kv)),
        pl.BlockSpec((seq, HEAD_DIM), lambda b, kv, qi: (krow(b), kcol // HEAD_DIM + kv)),
        pl.BlockSpec((seq, HEAD_DIM), lambda b, kv, qi: (krow(b), vcol // HEAD_DIM + kv)),
    ]
    args += [z, z, z]
    if cache is not None:
        past = cache[0].shape[2]
        cspec = pl.BlockSpec((None, None, past, HEAD_DIM), lambda b, kv, qi: (b, l, 0, kv))
        in_specs += [cspec, cspec]
        args += list(cache)
    in_specs.append(pl.BlockSpec(memory_space=pl.ANY))
    args.append(o)
    kern = functools.partial(_gqa_kernel, groups=groups, tq=tq, window=window,
                             has_cache=cache is not None, has_sink=sink is not None, l=l, lat_len=seq)
    return pl.pallas_call(
        kern, out_shape=jax.ShapeDtypeStruct(o.shape, o.dtype), grid=(nbatch, kvh, nq),
        in_specs=in_specs,
        out_specs=pl.BlockSpec((tq, gw), lambda b, kv, qi: (qrow(b, qi), ocol // gw + kv)),
        input_output_aliases={len(args) - 1: 0},
        compiler_params=_cparams(("parallel", "parallel", "arbitrary")),
        name="attn_window" if sink is not None else "attn_axial",
    )(*args)


def _diff_kernel(*refs, has_cache, lam_init, chain_rows):
    refs = list(refs)
    lq1, lk1, lq2, lk2, gain_ref, q_ref, k_ref, v_ref = refs[:8]
    refs = refs[8:]
    if has_cache:
        ck_ref, cv_ref = refs[:2]
    o_ref = refs[-1]
    h = pl.program_id(1)
    half = HEAD_DIM // 2
    scale = half ** -0.5

    lam = (jnp.exp(jnp.sum(lq1[...] * lk1[...], axis=-1, keepdims=True))
           - jnp.exp(jnp.sum(lq2[...] * lk2[...], axis=-1, keepdims=True)) + lam_init)

    k = k_ref[...]
    v = v_ref[...]
    lane = lax.broadcasted_iota(jnp.int32, (1, HEAD_DIM), 1)
    lo = jnp.where(lane < half, 1.0, 0.0).astype(BF16)
    if has_cache:
        ck = ck_ref[...].astype(BF16)
        cv = cv_ref[...].astype(BF16)
    gain = gain_ref[pl.ds(h, 1), :]
    c2 = scale * LOG2E

    for r0 in range(0, q_ref.shape[0], chain_rows):
        q = q_ref[r0:r0 + chain_rows, :]
        qs = (q * lo, q * (1.0 - lo))
        probs = []
        for c in range(2):
            s = _dot_nt(qs[c], k)
            m = jnp.max(s, axis=-1, keepdims=True)
            if has_cache:
                s2 = _dot_nt(qs[c], ck)
                m = jnp.maximum(m, jnp.max(s2, axis=-1, keepdims=True))
            e = jnp.exp2((s - m) * c2)
            den = jnp.sum(e, axis=-1, keepdims=True)
            if has_cache:
                e2 = jnp.exp2((s2 - m) * c2)
                den = den + jnp.sum(e2, axis=-1, keepdims=True)
            else:
                e2 = None
            probs.append((e, e2, 1.0 / den))
        (e_a, e2_a, r_a), (e_b, e2_b, r_b) = probs
        r_b = lam * r_b
        o = _dot((e_a * r_a - e_b * r_b).astype(BF16), v)
        if has_cache:
            o = o + _dot((e2_a * r_a - e2_b * r_b).astype(BF16), cv)
        o = o * lax.rsqrt(jnp.mean(o * o, axis=-1, keepdims=True) + EPS) * gain
        o_ref[r0:r0 + chain_rows, :] = (o * (1.0 - lam_init)).astype(BF16)


def _diff_call(l, z, o, lam_params, gain, *, nbatch, seq, row0, tq, lam_init, cache=None):
    nq = seq // tq
    depth = gain.shape[0]
    half = HEAD_DIM // 2
    qrow = lambda b, qi: row0 // tq + b * nq + qi
    krow = lambda b: row0 // seq + b
    lspec = pl.BlockSpec((None, 1, half), lambda b, h, qi: (l, 0, 0))
    in_specs = [lspec] * 4 + [
        pl.BlockSpec((None, DIFF_HEADS, HEAD_DIM), lambda b, h, qi: (l, 0, 0)),
        pl.BlockSpec((tq, HEAD_DIM), lambda b, h, qi: (qrow(b, qi), QC // HEAD_DIM + h)),
        pl.BlockSpec((seq, HEAD_DIM), lambda b, h, qi: (krow(b), KC // HEAD_DIM + h)),
        pl.BlockSpec((seq, HEAD_DIM), lambda b, h, qi: (krow(b), VC // HEAD_DIM + h)),
    ]
    args = [p.reshape(depth, 1, half) for p in lam_params] + [gain, z, z, z]
    if cache is not None:
        past = cache[0].shape[2]
        cspec = pl.BlockSpec((None, None, past, HEAD_DIM), lambda b, h, qi: (b, l, 0, h))
        in_specs += [cspec, cspec]
        args += list(cache)
    in_specs.append(pl.BlockSpec(memory_space=pl.ANY))
    args.append(o)
    ocol0 = (WIN_HEADS + AX_HEADS)
    return pl.pallas_call(
        functools.partial(_diff_kernel, has_cache=cache is not None, lam_init=lam_init,
                          chain_rows=min(tq, 256)),
        out_shape=jax.ShapeDtypeStruct(o.shape, o.dtype), grid=(nbatch, DIFF_HEADS, nq),
        in_specs=in_specs,
        out_specs=pl.BlockSpec((tq, HEAD_DIM), lambda b, h, qi: (qrow(b, qi), ocol0 + h)),
        input_output_aliases={len(args) - 1: 0},
        compiler_params=_cparams(("parallel", "parallel", "arbitrary")),
        name="attn_diff",
    )(*args)


def _out_kernel(o_ref, x_ref, w_ref, g1_ref, sh2_ref, sc2_ref, n2g_ref, x1_ref, h2_ref):
    a = _dot(o_ref[...], w_ref[...])
    x1 = x_ref[...] + g1_ref[...] * a
    x1_ref[...] = x1
    y = x1 * lax.rsqrt(jnp.mean(x1 * x1, axis=-1, keepdims=True) + EPS) * n2g_ref[...]
    h2_ref[...] = (y * (1.0 + sc2_ref[...]) + sh2_ref[...]).astype(BF16)


def _group_of_block(i, nb_ctx, bpl):
    return jnp.where(i < nb_ctx, 0, 1 + (i - nb_ctx) // bpl)


def _out_call(l, o, x, w_out, mods, norm2_g, *, n_ctx, lat_len, tm):
    n, d = x.shape
    depth = w_out.shape[0]
    nb_ctx, bpl = n_ctx // tm, lat_len // tm
    grp = lambda i: _group_of_block(i, nb_ctx, bpl)
    mspec = lambda which: pl.BlockSpec((None, None, None, 1, d), lambda i: (l, grp(i), which, 0, 0))
    return pl.pallas_call(
        _out_kernel,
        out_shape=(jax.ShapeDtypeStruct((n, d), F32), jax.ShapeDtypeStruct((n, d), BF16)),
        grid=(n // tm,),
        in_specs=[
            pl.BlockSpec((tm, d), lambda i: (i, 0)),
            pl.BlockSpec((tm, d), lambda i: (i, 0)),
            pl.BlockSpec((None, d, d), lambda i: (l, 0, 0)),
            mspec(2), mspec(3), mspec(4),
            pl.BlockSpec((None, 1, d), lambda i: (l, 0, 0)),
        ],
        out_specs=(pl.BlockSpec((tm, d), lambda i: (i, 0)), pl.BlockSpec((tm, d), lambda i: (i, 0))),
        compiler_params=_cparams(("parallel",)),
        name="out_proj",
    )(o, x, w_out, mods, mods, mods, norm2_g.reshape(depth, 1, d))


def _score_kernel(kw_ref, h2_ref, o_ref):
    o_ref[...] = _dot_nt(kw_ref[...], h2_ref[...])


def _score_call(l, kw, h2, *, tm):
    n, d = h2.shape
    rows = kw.shape[1]
    return pl.pallas_call(
        _score_kernel,
        out_shape=jax.ShapeDtypeStruct((rows, n), F32),
        grid=(n // tm,),
        in_specs=[pl.BlockSpec((None, rows, d), lambda i: (l, 0, 0)),
                  pl.BlockSpec((tm, d), lambda i: (i, 0))],
        out_specs=pl.BlockSpec((rows, tm), lambda i: (0, i)),
        compiler_params=_cparams(("parallel",)),
        name="peer_scores",
    )(kw, h2)


def _sort_pairs(n):
    pairs = []

    def merge(lo, hi, r):
        step = r * 2
        if step < hi - lo:
            merge(lo, hi, step)
            merge(lo + r, hi, step)
            for i in range(lo + r, hi - r, step):
                pairs.append((i, i + r))
        else:
            pairs.append((lo, lo + r))

    def sort(lo, hi):
        if hi - lo >= 1:
            mid = lo + (hi - lo) // 2
            sort(lo, mid)
            sort(mid + 1, hi)
            merge(lo, hi, 1)

    sort(0, n - 1)
    return pairs


_SORT16 = _sort_pairs(PEER_TOPK)


def _vmax(a, b):
    if a is None:
        return b
    if b is None:
        return a
    return jnp.maximum(a, b)


def _vmin(a, b):
    if a is None or b is None:
        return None
    return jnp.minimum(a, b)


def _bitonic_merge_desc(x):
    x = list(x)
    d = PEER_TOPK // 2
    while d >= 1:
        for i in range(PEER_TOPK):
            if (i & d) == 0:
                a, b = x[i], x[i + d]
                x[i], x[i + d] = _vmax(a, b), _vmin(a, b)
        d //= 2
    return x


def _merge_top16(a, b):
    a = list(a) + [None] * (PEER_TOPK - len(a))
    b = list(b) + [None] * (PEER_TOPK - len(b))
    return _bitonic_merge_desc([_vmax(a[k], b[PEER_TOPK - 1 - k]) for k in range(PEER_TOPK)])


def _router_kernel(s_ref, r1_ref, e1_ref, cnt_ref, e0_ref, top_ref):
    nk = PEER_KEYS
    nv = nk // 8
    tk = PEER_TOPK
    sub = lax.broadcasted_iota(jnp.int32, (8, s_ref.shape[1]), 0)

    for g in range(2 * PEER_HEADS):
        x = [s_ref[g * nk + 8 * r:g * nk + 8 * r + 8, :] for r in range(nv)]
        for (i, j) in _SORT16:
            x[i], x[j] = jnp.maximum(x[i], x[j]), jnp.minimum(x[i], x[j])
        for shift in (4, 2, 1):
            y = [pltpu.roll(v, shift, axis=0) for v in x]
            x = _bitonic_merge_desc([jnp.maximum(x[k], y[tk - 1 - k]) for k in range(tk)])
        for a in range(tk):
            top_ref[g * tk + a] = x[a]

    def packed(c, a):
        out = top_ref[c * tk + a]
        for h in range(1, PEER_HEADS):
            out = jnp.where(sub == h, top_ref[(2 * h + c) * tk + a], out)
        return out

    p0 = [packed(0, a) for a in range(tk)]
    p1 = [packed(1, b) for b in range(tk)]
    rows = [[p0[a] + p1[b] for b in range(tk // (a + 1))] for a in range(8)]
    col0 = [p0[a] + p1[0] for a in range(8, tk)]
    t1 = _merge_top16(rows[0], _merge_top16(rows[1], col0))
    t2 = _merge_top16(_merge_top16(rows[2], rows[3]), _merge_top16(rows[4], rows[5]))
    t3 = _merge_top16(rows[6], rows[7])
    best = _merge_top16(t1, _merge_top16(t2, t3))
    tau_p = best[tk - 1]
    zsum = jnp.ones_like(tau_p)
    for k in range(1, tk):
        zsum = zsum + jnp.exp(best[k] - best[0])
    zinv_p = 1.0 / zsum

    cnt_p = []
    for a in range(tk):
        acc = jnp.zeros_like(tau_p)
        for b in range(tk):
            acc = acc + jnp.where(p0[a] + p1[b] >= tau_p, 1.0, 0.0)
        cnt_p.append(acc)

    def pair_words(v):
        bits = pltpu.bitcast(v, jnp.uint32)
        return bits | lax.shift_right_logical(bits, jnp.uint32(16))

    for h in range(PEER_HEADS):
        shape = (8, s_ref.shape[1])
        zinv = jnp.broadcast_to(zinv_p[h:h + 1, :], shape)
        cnt_a = [jnp.broadcast_to(cnt_p[a][h:h + 1, :], shape) for a in range(tk)]
        t0 = [top_ref[(2 * h) * tk + a] for a in range(tk)]
        t1h = [top_ref[(2 * h + 1) * tk + b] for b in range(tk)]
        for rr in range(nv // 2):
            ranks, gates = [], []
            for r in (2 * rr, 2 * rr + 1):
                s0 = s_ref[(2 * h) * nk + 8 * r:(2 * h) * nk + 8 * r + 8, :]
                s1 = s_ref[(2 * h + 1) * nk + 8 * r:(2 * h + 1) * nk + 8 * r + 8, :]
                rank = jnp.full(shape, float(tk), F32)
                cnt = jnp.zeros(shape, F32)
                for b in range(tk - 1, -1, -1):
                    rank = jnp.where(s1 >= t1h[b], float(b), rank)
                    cnt = jnp.where(s0 >= t0[b], cnt_a[b], cnt)
                ranks.append(rank)
                gates.append(jnp.exp(s1 - t1h[0]))
                e0 = (jnp.exp(s0 - t0[0]) * zinv).astype(BF16).astype(F32)
                cnt_ref[h, 8 * r:8 * r + 8, :] = pair_words(cnt)
                e0_ref[h, 8 * r:8 * r + 8, :] = pair_words(e0)
            r1_ref[h, 16 * rr:16 * rr + 16, :] = jnp.concatenate(ranks, axis=0).astype(BF16)
            e1_ref[h, 16 * rr:16 * rr + 16, :] = jnp.concatenate(gates, axis=0).astype(BF16)


def _router_call(s_t):
    rows, n = s_t.shape
    tl = 128
    half = jax.ShapeDtypeStruct((PEER_HEADS, PEER_KEYS, n), BF16)
    words = jax.ShapeDtypeStruct((PEER_HEADS, PEER_KEYS, n), jnp.uint32)
    spec = pl.BlockSpec((PEER_HEADS, PEER_KEYS, tl), lambda i: (0, 0, i))
    return pl.pallas_call(
        _router_kernel,
        out_shape=(half, half, words, words),
        grid=(n // tl,),
        in_specs=[pl.BlockSpec((rows, tl), lambda i: (0, i))],
        out_specs=(spec, spec, spec, spec),
        scratch_shapes=[pltpu.VMEM((2 * PEER_HEADS * PEER_TOPK, 8, tl), F32)],
        compiler_params=_cparams(("parallel",)),
        name="peer_router",
    )(s_t)


def _gelu(a):
    c0 = math.sqrt(2.0 / math.pi)
    half = 0.5 * a
    return half + half * jnp.tanh(a * (c0 + (c0 * 0.044715) * (a * a)))


def _zero_tile(chunk):
    bits = pltpu.bitcast(chunk[:, 0:128], jnp.uint32)
    for lt in range(1, chunk.shape[1] // 128):
        bits = bits | pltpu.bitcast(chunk[:, lt * 128:(lt + 1) * 128], jnp.uint32)
    zero = lax.shift_right_logical(lax.shift_right_logical(bits, jnp.uint32(16)), jnp.uint32(16))
    return pltpu.bitcast(zero, BF16)


def _tied_operand(ref, ties):
    row_groups = []
    for rg in range(ref.shape[0] // 16):
        cols = []
        for c in range(ref.shape[1] // 256):
            piece = ref[rg * 16:(rg + 1) * 16, c * 256:(c + 1) * 256]
            z = ties.get((rg, c))
            cols.append(piece if z is None else piece + jnp.concatenate([z, z], axis=1))
        row_groups.append(jnp.concatenate(cols, axis=1))
    return jnp.concatenate(row_groups, axis=0)


def _peer_kernel(*refs, final, ib, nblk):
    if final:
        (h2_ref, u_ref, vt_ref, r1_ref, e1_ref, cnt_ref, e0_ref, x1_ref, g2_ref, fg_ref,
         o_ref, acc_ref, a_ref, w_ref, cs_ref, es_ref) = refs
    else:
        (h2_ref, u_ref, vt_ref, r1_ref, e1_ref, cnt_ref, e0_ref, x1_ref, g2_ref,
         o_ref, acc_ref, a_ref, w_ref, cs_ref, es_ref) = refs
    s = pl.program_id(1)
    t = h2_ref.shape[0]
    nchunk = PEER_KEYS // 16
    nch = ib * nchunk
    u_groups = u_ref.shape[0] // 16
    v_groups = vt_ref.shape[0] // 16

    def gates():
        chunks = []
        for ii in range(ib):
            for h in range(PEER_HEADS):
                slot = (ii % 2) * PEER_HEADS + h
                cs_ref[slot] = pltpu.bitcast(jnp.broadcast_to(cnt_ref[h, ii:ii + 1, :], (8, t)), BF16)
                es_ref[slot] = pltpu.bitcast(jnp.broadcast_to(e0_ref[h, ii:ii + 1, :], (8, t)), BF16)
            w = [None] * nchunk
            for h in range(PEER_HEADS):
                cnt = cs_ref[(ii % 2) * PEER_HEADS + h]
                e0 = es_ref[(ii % 2) * PEER_HEADS + h]
                for jc in range(nchunk):
                    r1 = r1_ref[h, jc * 16:(jc + 1) * 16, :]
                    e1 = e1_ref[h, jc * 16:(jc + 1) * 16, :]
                    gate = jnp.where(r1 < cnt, e0, jnp.zeros_like(e0)) * e1
                    w[jc] = gate if w[jc] is None else w[jc] + gate
            for jc in range(nchunk):
                r0 = ii * PEER_KEYS + jc * 16
                w_ref[r0:r0 + 16, :] = w[jc]
                chunks.append(w[jc])
        return chunks

    def gated():
        return [w_ref[ch * 16:(ch + 1) * 16, :] * _gelu(a_ref[ch * 16:(ch + 1) * 16, :]).astype(BF16)
                for ch in range(nch)]

    def slot_u(slot):
        return (slot % u_groups, slot // u_groups)

    def slot_v(slot):
        return (slot % v_groups, slot // v_groups)

    @pl.when(s == 0)
    def _():
        acc_ref[...] = jnp.zeros_like(acc_ref)
        chunks = gates()
        ties = {slot_u(7 * k): _zero_tile(chunks[k]) for k in range(nch)}
        a_ref[...] = _dot_nt(_tied_operand(u_ref, ties), h2_ref[...])

    @pl.when((s > 0) & (s < nblk))
    def _():
        p = gated()
        chunks = gates()
        ties = {slot_u(8 * c + 4): _zero_tile(p[c]) for c in range(nch)}
        ties.update({slot_u(16 * k): _zero_tile(chunks[k]) for k in range(nch // 2)})
        a_ref[...] = _dot_nt(_tied_operand(u_ref, ties), h2_ref[...])
        ties = {slot_v(16 * (k - nch // 2)): _zero_tile(chunks[k]) for k in range(nch // 2, nch)}
        acc_ref[...] += _dot(_tied_operand(vt_ref, ties), jnp.concatenate(p, axis=0))

    @pl.when(s == nblk)
    def _():
        acc = acc_ref[...] + _dot(vt_ref[...], jnp.concatenate(gated(), axis=0))
        x2 = x1_ref[...] + g2_ref[...] * acc.T
        if final:
            x2 = x2 * lax.rsqrt(jnp.mean(x2 * x2, axis=-1, keepdims=True) + EPS) * fg_ref[...]
        o_ref[...] = x2


def _peer_call(l, h2, u_bf, vt_bf, r1, e1, cnt, e0, x1, mods, final_g, *, n_ctx, lat_len, tm, final):
    n, d = x1.shape
    ib = 8
    eb = ib * PEER_KEYS
    nblk = u_bf.shape[1] // eb
    nb_ctx, bpl = n_ctx // tm, lat_len // tm
    grp = lambda i: _group_of_block(i, nb_ctx, bpl)
    once = dict(pipeline_mode=pl.Buffered(1))
    cur = lambda s: jnp.minimum(s, nblk - 1)
    prev = lambda s: jnp.maximum(s - 1, 0)
    in_specs = [
        pl.BlockSpec((tm, d), lambda i, s: (i, 0), **once),
        pl.BlockSpec((None, eb, d), lambda i, s: (l, cur(s), 0)),
        pl.BlockSpec((None, d, eb), lambda i, s: (l, 0, prev(s))),
        pl.BlockSpec((PEER_HEADS, PEER_KEYS, tm), lambda i, s: (0, 0, i), **once),
        pl.BlockSpec((PEER_HEADS, PEER_KEYS, tm), lambda i, s: (0, 0, i), **once),
        pl.BlockSpec((PEER_HEADS, ib, tm), lambda i, s: (0, cur(s), i)),
        pl.BlockSpec((PEER_HEADS, ib, tm), lambda i, s: (0, cur(s), i)),
        pl.BlockSpec((tm, d), lambda i, s: (i, 0), **once),
        pl.BlockSpec((None, None, None, 1, d), lambda i, s: (l, grp(i), 5, 0, 0)),
    ]
    args = [h2, u_bf, vt_bf, r1, e1, cnt, e0, x1, mods]
    if final:
        in_specs.append(pl.BlockSpec((1, d), lambda i, s: (0, 0)))
        args.append(final_g.reshape(1, d))
    nstage = 2 * PEER_HEADS
    return pl.pallas_call(
        functools.partial(_peer_kernel, final=final, ib=ib, nblk=nblk),
        out_shape=jax.ShapeDtypeStruct((n, d), F32),
        grid=(n // tm, nblk + 1),
        in_specs=in_specs,
        out_specs=pl.BlockSpec((tm, d), lambda i, s: (i, 0)),
        scratch_shapes=[pltpu.VMEM((d, tm), F32), pltpu.VMEM((eb, tm), F32), pltpu.VMEM((eb, tm), BF16),
                        pltpu.VMEM((nstage, 16, tm), BF16), pltpu.VMEM((nstage, 16, tm), BF16)],
        compiler_params=_cparams(("parallel", "arbitrary")),
        name="peer_experts",
    )(*args)


def kernel(x_prompt, x_sample, cache_k_win, cache_v_win, cache_k_axial, cache_v_axial, cache_k_diff,
           cache_v_diff, c, c_ctx, norm1_g, w_ada, b_ada, w_in, win_sink, ax_q_g, ax_k_g, lam_q1, lam_k1,
           lam_q2, lam_k2, diff_subln_g, w_out, norm2_g, peer_wq, peer_keys, peer_u, peer_v, final_g):
    batch, seq, d = x_prompt.shape
    dec_batch, lat_len, _ = x_sample.shape
    depth = w_in.shape[0]
    past = cache_k_win.shape[2]
    n_ctx = batch * seq
    n_lat = dec_batch * lat_len
    n = n_ctx + n_lat
    tm = 512
    tq_ctx = seq
    tq_lat = 256

    x = jnp.concatenate([x_prompt.reshape(n_ctx, d), x_sample.reshape(n_lat, d)], axis=0)
    cond = jnp.zeros((8, d), F32).at[0].set(c_ctx).at[1:1 + dec_batch].set(c)
    mods = _ada_call(cond, w_ada, b_ada).reshape(depth, 8, 6, 1, d)
    kw = _kw_call(peer_keys, peer_wq)
    w_in_bf = w_in.astype(BF16)
    w_out_bf = w_out.astype(BF16)
    u_bf = peer_u.astype(BF16)
    vt_bf = _vt_call(peer_v)
    tables = _rope_tables(lat_len)

    cache_a = (cache_k_win.reshape(dec_batch, depth, past, -1), cache_v_win.reshape(dec_batch, depth, past, -1))
    cache_b = (cache_k_axial.reshape(dec_batch, depth, past, -1), cache_v_axial.reshape(dec_batch, depth, past, -1))
    cache_c = (cache_k_diff.reshape(dec_batch, depth, past, -1), cache_v_diff.reshape(dec_batch, depth, past, -1))
    lam_params = (lam_q1, lam_k1, lam_q2, lam_k2)

    o = jnp.zeros((n, d), BF16)
    z = jnp.zeros((n, IN_COLS), BF16)
    kv_out = [[] for _ in range(6)]
    for l in range(depth):
        lam_init = 0.8 - 0.6 * math.exp(-0.3 * l)
        geo = dict(n_ctx=n_ctx, lat_len=lat_len, tm=tm)
        res = _in_call(l, x, mods, norm1_g, w_in_bf, ax_q_g, ax_k_g, None, z, latent=False, **geo)
        z = res[0]
        for k in range(6):
            kv_out[k].append(res[1 + k])
        z = _in_call(l, x, mods, norm1_g, w_in_bf, ax_q_g, ax_k_g, tables, z, latent=True, **geo)

        ctx = dict(nbatch=batch, seq=seq, row0=0, tq=tq_ctx)
        lat = dict(nbatch=dec_batch, seq=lat_len, row0=n_ctx, tq=tq_lat)
        win = dict(qcol=QA, kcol=KA, vcol=VA, ocol=0, heads=WIN_HEADS, kvh=WIN_KV, sink=win_sink)
        axl = dict(qcol=QB, kcol=KB, vcol=VB, ocol=WIN_HEADS * HEAD_DIM, heads=AX_HEADS, kvh=AX_KV)
        o = _gqa_call(l, z, o, **win, **ctx)
        o = _gqa_call(l, z, o, **axl, **ctx)
        o = _diff_call(l, z, o, lam_params, diff_subln_g, lam_init=lam_init, **ctx)
        o = _gqa_call(l, z, o, **win, **lat, window=True, cache=cache_a)
        o = _gqa_call(l, z, o, **axl, **lat, cache=cache_b)
        o = _diff_call(l, z, o, lam_params, diff_subln_g, lam_init=lam_init, cache=cache_c,
                       **dict(lat, tq=2 * tq_lat if lat_len % (2 * tq_lat) == 0 else tq_lat))

        x1, h2 = _out_call(l, o, x, w_out_bf, mods, norm2_g, **geo)
        s_t = _score_call(l, kw, h2, tm=tm)
        r1, e1, cnt, e0 = _router_call(s_t)
        x = _peer_call(l, h2, u_bf, vt_bf, r1, e1, cnt, e0, x1, mods, final_g,
                       final=(l == depth - 1), **geo)

    y_prompt = x[:n_ctx].reshape(batch, seq, d)
    y_sample = x[n_ctx:].reshape(dec_batch, lat_len, d)
    outs = []
    for k, nh in enumerate((WIN_KV, WIN_KV, AX_KV, AX_KV, DIFF_HEADS, DIFF_HEADS)):
        outs.append(jnp.stack([a.reshape(batch, seq, nh, HEAD_DIM) for a in kv_out[k]], axis=1))
    return (y_prompt, y_sample, *outs)
```

```python
import functools
import math

import jax
import jax.numpy as jnp
from jax import lax
from jax.experimental import pallas as pl
from jax.experimental.pallas import tpu as pltpu

F32 = jnp.float32
BF16 = jnp.bfloat16

HEAD_DIM = 128
GRID_W = 64
WINDOW = 128
ROPE_THETA = 10000.0
PEER_HEADS = 8
PEER_KEYS = 128
PEER_TOPK = 16
EPS = 1e-6
NEG_INF = -1e30
LOG2E = 1.4426950408889634

QA, KA, VA, QB, KB, VB, QC, KC, VC = 0, 512, 768, 1024, 2048, 2304, 2560, 3072, 3584
IN_COLS = 4096
WIN_HEADS, WIN_KV, AX_HEADS, AX_KV, DIFF_HEADS = 4, 2, 8, 2, 4

VMEM_LIMIT = 56 * 1024 * 1024


def _cparams(sem, **kw):
    return pltpu.CompilerParams(dimension_semantics=sem, vmem_limit_bytes=VMEM_LIMIT, **kw)


def _dot_nt(a, b):
    return lax.dot_general(a, b, (((1,), (1,)), ((), ())), preferred_element_type=F32)


def _dot(a, b):
    return jnp.dot(a, b, preferred_element_type=F32)


def _ada_kernel(cond_ref, w_ref, b_ref, o_ref):
    c = cond_ref[...]
    s = (c * jax.nn.sigmoid(c)).astype(BF16)
    o_ref[...] = _dot(s, w_ref[...].astype(BF16)) + b_ref[...]


def _ada_call(cond, w_ada, b_ada):
    depth, d, n6 = w_ada.shape
    tn = 1024
    return pl.pallas_call(
        _ada_kernel,
        out_shape=jax.ShapeDtypeStruct((depth, 8, n6), F32),
        grid=(depth, n6 // tn),
        in_specs=[
            pl.BlockSpec((8, d), lambda l, j: (0, 0)),
            pl.BlockSpec((None, d, tn), lambda l, j: (l, 0, j)),
            pl.BlockSpec((None, 1, tn), lambda l, j: (l, 0, j)),
        ],
        out_specs=pl.BlockSpec((None, 8, tn), lambda l, j: (l, 0, j)),
        compiler_params=_cparams(("parallel", "parallel")),
        name="ada",
    )(cond, w_ada, b_ada.reshape(depth, 1, n6))


def _kw_kernel(keys_ref, wq_ref, o_ref):
    kw = lax.dot_general(keys_ref[...], wq_ref[...], (((1,), (1,)), ((), ())),
                         preferred_element_type=F32, precision=lax.Precision.HIGHEST)
    o_ref[...] = kw.astype(BF16)


def _kw_call(peer_keys, peer_wq):
    depth, d, nq = peer_wq.shape
    ngrp = PEER_HEADS * 2
    sub = PEER_KEYS
    return pl.pallas_call(
        _kw_kernel,
        out_shape=jax.ShapeDtypeStruct((depth, ngrp * sub, d), BF16),
        grid=(depth, ngrp),
        in_specs=[
            pl.BlockSpec((None, None, None, sub, sub), lambda l, g: (l, g // 2, g % 2, 0, 0)),
            pl.BlockSpec((None, d, sub), lambda l, g: (l, 0, g)),
        ],
        out_specs=pl.BlockSpec((None, sub, d), lambda l, g: (l, g, 0)),
        compiler_params=_cparams(("parallel", "parallel")),
        name="peer_kw",
    )(peer_keys, peer_wq)


def _vt_kernel(v_ref, o_ref):
    o_ref[...] = v_ref[...].T.astype(BF16)


def _vt_call(peer_v):
    depth, n_exp, d = peer_v.shape
    te = 512
    return pl.pallas_call(
        _vt_kernel,
        out_shape=jax.ShapeDtypeStruct((depth, d, n_exp), BF16),
        grid=(depth, n_exp // te),
        in_specs=[pl.BlockSpec((None, te, d), lambda l, j: (l, j, 0))],
        out_specs=pl.BlockSpec((None, d, te), lambda l, j: (l, 0, j)),
        compiler_params=_cparams(("parallel", "parallel")),
        name="peer_vt",
    )(peer_v)


def _rope_tables(n_tok):
    n_rows = n_tok // GRID_W
    row_pos = jnp.repeat(jnp.arange(n_rows, dtype=jnp.int32), GRID_W).astype(F32)
    col_pos = jnp.tile(jnp.arange(GRID_W, dtype=jnp.int32), n_rows).astype(F32)
    lane = jnp.arange(HEAD_DIM)

    def table(pair, use_row):
        freqs = ROPE_THETA ** (-jnp.arange(pair, dtype=F32) / pair)
        f = freqs[lane % pair]
        pos = jnp.where(use_row[None, :], row_pos[:, None], col_pos[:, None])
        ang = pos * f[None, :]
        first = (lane % (2 * pair)) < pair
        return jnp.cos(ang), jnp.where(first[None, :], -jnp.sin(ang), jnp.sin(ang))

    c_ax, s_ax = table(32, lane < 64)
    c_df, s_df = table(16, (lane % 64) < 32)
    return c_ax, s_ax, c_df, s_df


def _rope(y, c, s, pair):
    lane = lax.broadcasted_iota(jnp.int32, y.shape, 1)
    first = (lane % (2 * pair)) < pair
    partner = jnp.where(first, pltpu.roll(y, HEAD_DIM - pair, axis=1), pltpu.roll(y, pair, axis=1))
    return y * c + partner * s


def _head_norm(y, g):
    return y * lax.rsqrt(jnp.mean(y * y, axis=-1, keepdims=True) + EPS) * g


def _in_kernel(*refs, latent):
    if latent:
        (x_ref, sh_ref, sc_ref, g_ref, w_ref, axq_ref, axk_ref, ca_ref, sa_ref, cd_ref, sd_ref,
         _z_in, z_ref, h_ref) = refs
        ka_ref = va_ref = kb_ref = vb_ref = kc_ref = vc_ref = None
    else:
        (x_ref, sh_ref, sc_ref, g_ref, w_ref, axq_ref, axk_ref, _z_in,
         z_ref, ka_ref, va_ref, kb_ref, vb_ref, kc_ref, vc_ref, h_ref) = refs
    j = pl.program_id(1)

    @pl.when(j == 0)
    def _():
        x = x_ref[...]
        y = x * lax.rsqrt(jnp.mean(x * x, axis=-1, keepdims=True) + EPS) * g_ref[...]
        h_ref[...] = (y * (1.0 + sc_ref[...]) + sh_ref[...]).astype(BF16)

    def rope_ax(y):
        return _rope(y, ca_ref[...], sa_ref[...], 32) if latent else y

    def rope_df(y):
        return _rope(y, cd_ref[...], sd_ref[...], 16) if latent else y

    def head(z, k):
        return z[:, k * HEAD_DIM:(k + 1) * HEAD_DIM]

    def put(k, y):
        z_ref[:, k * HEAD_DIM:(k + 1) * HEAD_DIM] = y.astype(BF16)

    def keep(ref, k, y):
        if not latent:
            ref[:, k * HEAD_DIM:(k + 1) * HEAD_DIM] = y

    @pl.when(j == 0)
    def _():
        z = _dot(h_ref[...], w_ref[...])
        for k in range(4):
            put(k, rope_ax(head(z, k)))
        for k in range(2):
            y = head(z, 4 + k)
            keep(ka_ref, k, y)
            put(4 + k, rope_ax(y))
        for k in range(2):
            y = head(z, 6 + k)
            keep(va_ref, k, y)
            put(6 + k, y)

    @pl.when(j == 1)
    def _():
        z = _dot(h_ref[...], w_ref[...])
        for k in range(8):
            put(k, rope_ax(_head_norm(head(z, k), axq_ref[...])))

    @pl.when(j == 2)
    def _():
        z = _dot(h_ref[...], w_ref[...])
        for k in range(2):
            y = _head_norm(head(z, k), axk_ref[...])
            keep(kb_ref, k, y)
            put(k, rope_ax(y))
        for k in range(2):
            y = head(z, 2 + k)
            keep(vb_ref, k, y)
            put(2 + k, y)
        for k in range(4):
            put(4 + k, rope_df(head(z, 4 + k)))

    @pl.when(j == 3)
    def _():
        z = _dot(h_ref[...], w_ref[...])
        for k in range(4):
            y = head(z, k)
            keep(kc_ref, k, y)
            put(k, rope_df(y))
        for k in range(4):
            y = head(z, 4 + k)
            keep(vc_ref, k, y)
            put(4 + k, y)


def _in_call(l, x, mods, norm1_g, w_in, ax_q_g, ax_k_g, tables, z, *, latent, n_ctx, lat_len, tm):
    n, d = x.shape
    depth = w_in.shape[0]
    nb_ctx = n_ctx // tm
    bpl = lat_len // tm
    if latent:
        nblk = (n - n_ctx) // tm
        row = lambda i: i + nb_ctx
        grp = lambda i: 1 + i // bpl
    else:
        nblk = nb_ctx
        row = lambda i: i
        grp = lambda i: 0
    cb = 1024
    in_specs = [
        pl.BlockSpec((tm, d), lambda i, j: (row(i), 0)),
        pl.BlockSpec((None, None, None, 1, d), lambda i, j: (l, grp(i), 0, 0, 0)),
        pl.BlockSpec((None, None, None, 1, d), lambda i, j: (l, grp(i), 1, 0, 0)),
        pl.BlockSpec((None, 1, d), lambda i, j: (l, 0, 0)),
        pl.BlockSpec((None, d, cb), lambda i, j: (l, 0, j)),
        pl.BlockSpec((None, 1, HEAD_DIM), lambda i, j: (l, 0, 0)),
        pl.BlockSpec((None, 1, HEAD_DIM), lambda i, j: (l, 0, 0)),
    ]
    args = [x, mods, mods, norm1_g.reshape(depth, 1, d), w_in,
            ax_q_g.reshape(depth, 1, HEAD_DIM), ax_k_g.reshape(depth, 1, HEAD_DIM)]
    z_spec = pl.BlockSpec((tm, cb), lambda i, j: (row(i), j))
    z_shape = jax.ShapeDtypeStruct((n, IN_COLS), BF16)
    scratch = [pltpu.VMEM((tm, d), BF16)]
    if latent:
        tspec = pl.BlockSpec((tm, HEAD_DIM), lambda i, j: (i % bpl, 0))
        in_specs += [tspec] * 4 + [pl.BlockSpec(memory_space=pl.ANY)]
        args += list(tables) + [z]
        return pl.pallas_call(
            functools.partial(_in_kernel, latent=True),
            out_shape=z_shape, grid=(nblk, 4), in_specs=in_specs, out_specs=z_spec,
            scratch_shapes=scratch, input_output_aliases={len(args) - 1: 0},
            compiler_params=_cparams(("parallel", "arbitrary")), name="in_proj_latent",
        )(*args)
    kv2 = jax.ShapeDtypeStruct((n_ctx, 2 * HEAD_DIM), F32)
    kv4 = jax.ShapeDtypeStruct((n_ctx, 4 * HEAD_DIM), F32)
    s2 = pl.BlockSpec((tm, 2 * HEAD_DIM), lambda i, j: (i, 0))
    s4 = pl.BlockSpec((tm, 4 * HEAD_DIM), lambda i, j: (i, 0))
    in_specs.append(pl.BlockSpec(memory_space=pl.ANY))
    args.append(z)
    return pl.pallas_call(
        functools.partial(_in_kernel, latent=False),
        out_shape=(z_shape, kv2, kv2, kv2, kv2, kv4, kv4), grid=(nblk, 4), in_specs=in_specs,
        out_specs=(z_spec, s2, s2, s2, s2, s4, s4), scratch_shapes=scratch,
        input_output_aliases={len(args) - 1: 0},
        compiler_params=_cparams(("parallel", "arbitrary")), name="in_proj_context",
    )(*args)


def _gqa_kernel(*refs, groups, tq, window, has_cache, has_sink, l, lat_len):
    refs = list(refs)
    sink_ref = refs.pop(0) if has_sink else None
    q_ref, k_ref, v_ref = refs[:3]
    refs = refs[3:]
    if has_cache:
        ck_ref, cv_ref = refs[:2]
        refs = refs[2:]
    o_ref = refs[-1]
    kv = pl.program_id(1)
    qi = pl.program_id(2)
    scale = HEAD_DIM ** -0.5

    if window:
        span = tq + 2 * WINDOW
        start = jnp.clip(qi * tq - WINDOW, 0, lat_len - span)
        start = pl.multiple_of(start, WINDOW)
        k = k_ref[pl.ds(start, span), :]
        v = v_ref[pl.ds(start, span), :]
        qpos = qi * tq + lax.broadcasted_iota(jnp.int32, (tq, span), 0)
        kpos = start + lax.broadcasted_iota(jnp.int32, (tq, span), 1)
        ok = jnp.abs(qpos - kpos) <= WINDOW
    else:
        k = k_ref[...]
        v = v_ref[...]
    if has_cache:
        ck = ck_ref[...].astype(BF16)
        cv = cv_ref[...].astype(BF16)

    c2 = scale * LOG2E
    rows = tq if window else min(tq, 256)
    for r0 in range(0, tq, rows):
        for g in range(groups):
            q = q_ref[r0:r0 + rows, g * HEAD_DIM:(g + 1) * HEAD_DIM]
            s = _dot_nt(q, k)
            if window:
                s = jnp.where(ok, s, NEG_INF)
            m = jnp.max(s, axis=-1, keepdims=True)
            if has_cache:
                s2 = _dot_nt(q, ck)
                m = jnp.maximum(m, jnp.max(s2, axis=-1, keepdims=True))
            if has_sink:
                sink = sink_ref[l, kv * groups + g] * (1.0 / scale)
                m = jnp.maximum(m, sink)
            e = jnp.exp2((s - m) * c2)
            den = jnp.sum(e, axis=-1, keepdims=True)
            o = _dot(e.astype(BF16), v)
            if has_cache:
                e2 = jnp.exp2((s2 - m) * c2)
                den = den + jnp.sum(e2, axis=-1, keepdims=True)
                o = o + _dot(e2.astype(BF16), cv)
            if has_sink:
                den = den + jnp.exp2((sink - m) * c2)
            o_ref[r0:r0 + rows, g * HEAD_DIM:(g + 1) * HEAD_DIM] = (o / den).astype(BF16)


def _gqa_call(l, z, o, *, qcol, kcol, vcol, ocol, heads, kvh, nbatch, seq, row0, tq,
              window=False, cache=None, sink=None):
    groups = heads // kvh
    gw = groups * HEAD_DIM
    nq = seq // tq
    qrow = lambda b, qi: row0 // tq + b * nq + qi
    krow = lambda b: row0 // seq + b
    in_specs, args = [], []
    if sink is not None:
        in_specs.append(pl.BlockSpec(memory_space=pltpu.SMEM))
        args.append(sink)
    in_specs += [
        pl.BlockSpec((tq, gw), lambda b, kv, qi: (qrow(b, qi), qcol // gw + kv)),
        pl.BlockSpec((seq, HEAD_DIM), lambda b, kv, qi: (krow(b), kcol // HEAD_DIM + kv)),
        pl.BlockSpec((seq, HEAD_DIM), lambda b, kv, qi: (krow(b), vcol // HEAD_DIM + kv)),
    ]
    args += [z, z, z]
    if cache is not None:
        past = cache[0].shape[2]
        cspec = pl.BlockSpec((None, None, past, HEAD_DIM), lambda b, kv, qi: (b, l, 0, kv))
        in_specs += [cspec, cspec]
        args += list(cache)
    in_specs.append(pl.BlockSpec(memory_space=pl.ANY))
    args.append(o)
    kern = functools.partial(_gqa_kernel, groups=groups, tq=tq, window=window,
                             has_cache=cache is not None, has_sink=sink is not None, l=l, lat_len=seq)
    return pl.pallas_call(
        kern, out_shape=jax.ShapeDtypeStruct(o.shape, o.dtype), grid=(nbatch, kvh, nq),
        in_specs=in_specs,
        out_specs=pl.BlockSpec((tq, gw), lambda b, kv, qi: (qrow(b, qi), ocol // gw + kv)),
        input_output_aliases={len(args) - 1: 0},
        compiler_params=_cparams(("parallel", "parallel", "arbitrary")),
        name="attn_window" if sink is not None else "attn_axial",
    )(*args)


def _diff_kernel(*refs, has_cache, lam_init, chain_rows):
    refs = list(refs)
    lq1, lk1, lq2, lk2, gain_ref, q_ref, k_ref, v_ref = refs[:8]
    refs = refs[8:]
    if has_cache:
        ck_ref, cv_ref = refs[:2]
    o_ref = refs[-1]
    h = pl.program_id(1)
    half = HEAD_DIM // 2
    scale = half ** -0.5

    lam = (jnp.exp(jnp.sum(lq1[...] * lk1[...], axis=-1, keepdims=True))
           - jnp.exp(jnp.sum(lq2[...] * lk2[...], axis=-1, keepdims=True)) + lam_init)

    k = k_ref[...]
    v = v_ref[...]
    lane = lax.broadcasted_iota(jnp.int32, (1, HEAD_DIM), 1)
    lo = jnp.where(lane < half, 1.0, 0.0).astype(BF16)
    if has_cache:
        ck = ck_ref[...].astype(BF16)
        cv = cv_ref[...].astype(BF16)
    gain = gain_ref[pl.ds(h, 1), :]
    c2 = scale * LOG2E

    for r0 in range(0, q_ref.shape[0], chain_rows):
        q = q_ref[r0:r0 + chain_rows, :]
        qs = (q * lo, q * (1.0 - lo))
        probs = []
        for c in range(2):
            s = _dot_nt(qs[c], k)
            m = jnp.max(s, axis=-1, keepdims=True)
            if has_cache:
                s2 = _dot_nt(qs[c], ck)
                m = jnp.maximum(m, jnp.max(s2, axis=-1, keepdims=True))
            e = jnp.exp2((s - m) * c2)
            den = jnp.sum(e, axis=-1, keepdims=True)
            if has_cache:
                e2 = jnp.exp2((s2 - m) * c2)
                den = den + jnp.sum(e2, axis=-1, keepdims=True)
            else:
                e2 = None
            probs.append((e, e2, 1.0 / den))
        (e_a, e2_a, r_a), (e_b, e2_b, r_b) = probs
        r_b = lam * r_b
        o = _dot((e_a * r_a - e_b * r_b).astype(BF16), v)
        if has_cache:
            o = o + _dot((e2_a * r_a - e2_b * r_b).astype(BF16), cv)
        o = o * lax.rsqrt(jnp.mean(o * o, axis=-1, keepdims=True) + EPS) * gain
        o_ref[r0:r0 + chain_rows, :] = (o * (1.0 - lam_init)).astype(BF16)


def _diff_call(l, z, o, lam_params, gain, *, nbatch, seq, row0, tq, lam_init, cache=None):
    nq = seq // tq
    depth = gain.shape[0]
    half = HEAD_DIM // 2
    qrow = lambda b, qi: row0 // tq + b * nq + qi
    krow = lambda b: row0 // seq + b
    lspec = pl.BlockSpec((None, 1, half), lambda b, h, qi: (l, 0, 0))
    in_specs = [lspec] * 4 + [
        pl.BlockSpec((None, DIFF_HEADS, HEAD_DIM), lambda b, h, qi: (l, 0, 0)),
        pl.BlockSpec((tq, HEAD_DIM), lambda b, h, qi: (qrow(b, qi), QC // HEAD_DIM + h)),
        pl.BlockSpec((seq, HEAD_DIM), lambda b, h, qi: (krow(b), KC // HEAD_DIM + h)),
        pl.BlockSpec((seq, HEAD_DIM), lambda b, h, qi: (krow(b), VC // HEAD_DIM + h)),
    ]
    args = [p.reshape(depth, 1, half) for p in lam_params] + [gain, z, z, z]
    if cache is not None:
        past = cache[0].shape[2]
        cspec = pl.BlockSpec((None, None, past, HEAD_DIM), lambda b, h, qi: (b, l, 0, h))
        in_specs += [cspec, cspec]
        args += list(cache)
    in_specs.append(pl.BlockSpec(memory_space=pl.ANY))
    args.append(o)
    ocol0 = (WIN_HEADS + AX_HEADS)
    return pl.pallas_call(
        functools.partial(_diff_kernel, has_cache=cache is not None, lam_init=lam_init,
                          chain_rows=min(tq, 256)),
        out_shape=jax.ShapeDtypeStruct(o.shape, o.dtype), grid=(nbatch, DIFF_HEADS, nq),
        in_specs=in_specs,
        out_specs=pl.BlockSpec((tq, HEAD_DIM), lambda b, h, qi: (qrow(b, qi), ocol0 + h)),
        input_output_aliases={len(args) - 1: 0},
        compiler_params=_cparams(("parallel", "parallel", "arbitrary")),
        name="attn_diff",
    )(*args)


def _out_kernel(o_ref, x_ref, w_ref, g1_ref, sh2_ref, sc2_ref, n2g_ref, x1_ref, h2_ref):
    a = _dot(o_ref[...], w_ref[...])
    x1 = x_ref[...] + g1_ref[...] * a
    x1_ref[...] = x1
    y = x1 * lax.rsqrt(jnp.mean(x1 * x1, axis=-1, keepdims=True) + EPS) * n2g_ref[...]
    h2_ref[...] = (y * (1.0 + sc2_ref[...]) + sh2_ref[...]).astype(BF16)


def _group_of_block(i, nb_ctx, bpl):
    return jnp.where(i < nb_ctx, 0, 1 + (i - nb_ctx) // bpl)


def _out_call(l, o, x, w_out, mods, norm2_g, *, n_ctx, lat_len, tm):
    n, d = x.shape
    depth = w_out.shape[0]
    nb_ctx, bpl = n_ctx // tm, lat_len // tm
    grp = lambda i: _group_of_block(i, nb_ctx, bpl)
    mspec = lambda which: pl.BlockSpec((None, None, None, 1, d), lambda i: (l, grp(i), which, 0, 0))
    return pl.pallas_call(
        _out_kernel,
        out_shape=(jax.ShapeDtypeStruct((n, d), F32), jax.ShapeDtypeStruct((n, d), BF16)),
        grid=(n // tm,),
        in_specs=[
            pl.BlockSpec((tm, d), lambda i: (i, 0)),
            pl.BlockSpec((tm, d), lambda i: (i, 0)),
            pl.BlockSpec((None, d, d), lambda i: (l, 0, 0)),
            mspec(2), mspec(3), mspec(4),
            pl.BlockSpec((None, 1, d), lambda i: (l, 0, 0)),
        ],
        out_specs=(pl.BlockSpec((tm, d), lambda i: (i, 0)), pl.BlockSpec((tm, d), lambda i: (i, 0))),
        compiler_params=_cparams(("parallel",)),
        name="out_proj",
    )(o, x, w_out, mods, mods, mods, norm2_g.reshape(depth, 1, d))


def _score_kernel(kw_ref, h2_ref, o_ref):
    o_ref[...] = _dot_nt(kw_ref[...], h2_ref[...])


def _score_call(l, kw, h2, *, tm):
    n, d = h2.shape
    rows = kw.shape[1]
    return pl.pallas_call(
        _score_kernel,
        out_shape=jax.ShapeDtypeStruct((rows, n), F32),
        grid=(n // tm,),
        in_specs=[pl.BlockSpec((None, rows, d), lambda i: (l, 0, 0)),
                  pl.BlockSpec((tm, d), lambda i: (i, 0))],
        out_specs=pl.BlockSpec((rows, tm), lambda i: (0, i)),
        compiler_params=_cparams(("parallel",)),
        name="peer_scores",
    )(kw, h2)


def _sort_pairs(n):
    pairs = []

    def merge(lo, hi, r):
        step = r * 2
        if step < hi - lo:
            merge(lo, hi, step)
            merge(lo + r, hi, step)
            for i in range(lo + r, hi - r, step):
                pairs.append((i, i + r))
        else:
            pairs.append((lo, lo + r))

    def sort(lo, hi):
        if hi - lo >= 1:
            mid = lo + (hi - lo) // 2
            sort(lo, mid)
            sort(mid + 1, hi)
            merge(lo, hi, 1)

    sort(0, n - 1)
    return pairs


_SORT16 = _sort_pairs(PEER_TOPK)


def _vmax(a, b):
    if a is None:
        return b
    if b is None:
        return a
    return jnp.maximum(a, b)


def _vmin(a, b):
    if a is None or b is None:
        return None
    return jnp.minimum(a, b)


def _bitonic_merge_desc(x):
    x = list(x)
    d = PEER_TOPK // 2
    while d >= 1:
        for i in range(PEER_TOPK):
            if (i & d) == 0:
                a, b = x[i], x[i + d]
                x[i], x[i + d] = _vmax(a, b), _vmin(a, b)
        d //= 2
    return x


def _merge_top16(a, b):
    a = list(a) + [None] * (PEER_TOPK - len(a))
    b = list(b) + [None] * (PEER_TOPK - len(b))
    return _bitonic_merge_desc([_vmax(a[k], b[PEER_TOPK - 1 - k]) for k in range(PEER_TOPK)])


def _router_kernel(s_ref, r1_ref, e1_ref, cnt_ref, e0_ref, top_ref):
    nk = PEER_KEYS
    nv = nk // 8
    tk = PEER_TOPK
    sub = lax.broadcasted_iota(jnp.int32, (8, s_ref.shape[1]), 0)

    for g in range(2 * PEER_HEADS):
        x = [s_ref[g * nk + 8 * r:g * nk + 8 * r + 8, :] for r in range(nv)]
        for (i, j) in _SORT16:
            x[i], x[j] = jnp.maximum(x[i], x[j]), jnp.minimum(x[i], x[j])
        for shift in (4, 2, 1):
            y = [pltpu.roll(v, shift, axis=0) for v in x]
            x = _bitonic_merge_desc([jnp.maximum(x[k], y[tk - 1 - k]) for k in range(tk)])
        for a in range(tk):
            top_ref[g * tk + a] = x[a]

    def packed(c, a):
        out = top_ref[c * tk + a]
        for h in range(1, PEER_HEADS):
            out = jnp.where(sub == h, top_ref[(2 * h + c) * tk + a], out)
        return out

    p0 = [packed(0, a) for a in range(tk)]
    p1 = [packed(1, b) for b in range(tk)]
    rows = [[p0[a] + p1[b] for b in range(tk // (a + 1))] for a in range(8)]
    col0 = [p0[a] + p1[0] for a in range(8, tk)]
    t1 = _merge_top16(rows[0], _merge_top16(rows[1], col0))
    t2 = _merge_top16(_merge_top16(rows[2], rows[3]), _merge_top16(rows[4], rows[5]))
    t3 = _merge_top16(rows[6], rows[7])
    best = _merge_top16(t1, _merge_top16(t2, t3))
    tau_p = best[tk - 1]
    zsum = jnp.ones_like(tau_p)
    for k in range(1, tk):
        zsum = zsum + jnp.exp(best[k] - best[0])
    zinv_p = 1.0 / zsum

    cnt_p = []
    for a in range(tk):
        acc = jnp.zeros_like(tau_p)
        for b in range(tk):
            acc = acc + jnp.where(p0[a] + p1[b] >= tau_p, 1.0, 0.0)
        cnt_p.append(acc)

    def pair_words(v):
        bits = pltpu.bitcast(v, jnp.uint32)
        return bits | lax.shift_right_logical(bits, jnp.uint32(16))

    for h in range(PEER_HEADS):
        shape = (8, s_ref.shape[1])
        zinv = jnp.broadcast_to(zinv_p[h:h + 1, :], shape)
        cnt_a = [jnp.broadcast_to(cnt_p[a][h:h + 1, :], shape) for a in range(tk)]
        t0 = [top_ref[(2 * h) * tk + a] for a in range(tk)]
        t1h = [top_ref[(2 * h + 1) * tk + b] for b in range(tk)]
        for rr in range(nv // 2):
            ranks, gates = [], []
            for r in (2 * rr, 2 * rr + 1):
                s0 = s_ref[(2 * h) * nk + 8 * r:(2 * h) * nk + 8 * r + 8, :]
                s1 = s_ref[(2 * h + 1) * nk + 8 * r:(2 * h + 1) * nk + 8 * r + 8, :]
                rank = jnp.full(shape, float(tk), F32)
                cnt = jnp.zeros(shape, F32)
                for b in range(tk - 1, -1, -1):
                    rank = jnp.where(s1 >= t1h[b], float(b), rank)
                    cnt = jnp.where(s0 >= t0[b], cnt_a[b], cnt)
                ranks.append(rank)
                gates.append(jnp.exp(s1 - t1h[0]))
                e0 = (jnp.exp(s0 - t0[0]) * zinv).astype(BF16).astype(F32)
                cnt_ref[h, 8 * r:8 * r + 8, :] = pair_words(cnt)
                e0_ref[h, 8 * r:8 * r + 8, :] = pair_words(e0)
            r1_ref[h, 16 * rr:16 * rr + 16, :] = jnp.concatenate(ranks, axis=0).astype(BF16)
            e1_ref[h, 16 * rr:16 * rr + 16, :] = jnp.concatenate(gates, axis=0).astype(BF16)


def _router_call(s_t):
    rows, n = s_t.shape
    tl = 128
    half = jax.ShapeDtypeStruct((PEER_HEADS, PEER_KEYS, n), BF16)
    words = jax.ShapeDtypeStruct((PEER_HEADS, PEER_KEYS, n), jnp.uint32)
    spec = pl.BlockSpec((PEER_HEADS, PEER_KEYS, tl), lambda i: (0, 0, i))
    return pl.pallas_call(
        _router_kernel,
        out_shape=(half, half, words, words),
        grid=(n // tl,),
        in_specs=[pl.BlockSpec((rows, tl), lambda i: (0, i))],
        out_specs=(spec, spec, spec, spec),
        scratch_shapes=[pltpu.VMEM((2 * PEER_HEADS * PEER_TOPK, 8, tl), F32)],
        compiler_params=_cparams(("parallel",)),
        name="peer_router",
    )(s_t)


def _gelu(a):
    c0 = math.sqrt(2.0 / math.pi)
    half = 0.5 * a
    return half + half * jnp.tanh(a * (c0 + (c0 * 0.044715) * (a * a)))


def _zero_tile(chunk):
    bits = pltpu.bitcast(chunk[:, 0:128], jnp.uint32)
    for lt in range(1, chunk.shape[1] // 128):
        bits = bits | pltpu.bitcast(chunk[:, lt * 128:(lt + 1) * 128], jnp.uint32)
    zero = lax.shift_right_logical(lax.shift_right_logical(bits, jnp.uint32(16)), jnp.uint32(16))
    return pltpu.bitcast(zero, BF16)


def _tied_operand(ref, ties):
    row_groups = []
    for rg in range(ref.shape[0] // 16):
        cols = []
        for c in range(ref.shape[1] // 256):
            piece = ref[rg * 16:(rg + 1) * 16, c * 256:(c + 1) * 256]
            z = ties.get((rg, c))
            cols.append(piece if z is None else piece + jnp.concatenate([z, z], axis=1))
        row_groups.append(jnp.concatenate(cols, axis=1))
    return jnp.concatenate(row_groups, axis=0)


def _peer_kernel(*refs, final, ib, nblk):
    if final:
        (h2_ref, u_ref, vt_ref, r1_ref, e1_ref, cnt_ref, e0_ref, x1_ref, g2_ref, fg_ref,
         o_ref, acc_ref, a0_ref, a1_ref, w0_ref, w1_ref, cs_ref, es_ref) = refs
    else:
        (h2_ref, u_ref, vt_ref, r1_ref, e1_ref, cnt_ref, e0_ref, x1_ref, g2_ref,
         o_ref, acc_ref, a0_ref, a1_ref, w0_ref, w1_ref, cs_ref, es_ref) = refs
    s = pl.program_id(1)
    t = h2_ref.shape[0]
    nchunk = PEER_KEYS // 16
    nch = ib * nchunk
    u_groups = u_ref.shape[0] // 16
    v_groups = vt_ref.shape[0] // 16

    def gates(w_ref):
        chunks = []
        for ii in range(ib):
            for h in range(PEER_HEADS):
                slot = (ii % 2) * PEER_HEADS + h
                cs_ref[slot] = pltpu.bitcast(jnp.broadcast_to(cnt_ref[h, ii:ii + 1, :], (8, t)), BF16)
                es_ref[slot] = pltpu.bitcast(jnp.broadcast_to(e0_ref[h, ii:ii + 1, :], (8, t)), BF16)
            w = [None] * nchunk
            for h in range(PEER_HEADS):
                cnt = cs_ref[(ii % 2) * PEER_HEADS + h]
                e0 = es_ref[(ii % 2) * PEER_HEADS + h]
                for jc in range(nchunk):
                    r1 = r1_ref[h, jc * 16:(jc + 1) * 16, :]
                    e1 = e1_ref[h, jc * 16:(jc + 1) * 16, :]
                    gate = jnp.where(r1 < cnt, e0, jnp.zeros_like(e0)) * e1
                    w[jc] = gate if w[jc] is None else w[jc] + gate
            for jc in range(nchunk):
                r0 = ii * PEER_KEYS + jc * 16
                w_ref[r0:r0 + 16, :] = w[jc]
                chunks.append(w[jc])
        return chunks

    def gated(a_ref, w_ref):
        return [w_ref[ch * 16:(ch + 1) * 16, :] * _gelu(a_ref[ch * 16:(ch + 1) * 16, :]).astype(BF16)
                for ch in range(nch)]

    def slot_u(slot):
        return (slot % u_groups, slot // u_groups)

    def slot_v(slot):
        return (slot % v_groups, slot // v_groups)

    @pl.when(s == 0)
    def _():
        acc_ref[...] = jnp.zeros_like(acc_ref)
        chunks = gates(w0_ref)
        ties = {slot_u(7 * k): _zero_tile(chunks[k]) for k in range(nch)}
        a0_ref[...] = _dot_nt(_tied_operand(u_ref, ties), h2_ref[...])

    bufs = ((a0_ref, w0_ref), (a1_ref, w1_ref))
    for parity in range(2):
        @pl.when((s > 0) & (s < nblk) & (s % 2 == parity))
        def _():
            a_wr, w_wr = bufs[parity]
            a_rd, w_rd = bufs[1 - parity]
            p = gated(a_rd, w_rd)
            chunks = gates(w_wr)
            ties = {slot_u(8 * c + 4): _zero_tile(p[c]) for c in range(nch)}
            ties.update({slot_u(16 * k): _zero_tile(chunks[k]) for k in range(nch // 2)})
            a_wr[...] = _dot_nt(_tied_operand(u_ref, ties), h2_ref[...])
            ties = {slot_v(16 * (k - nch // 2)): _zero_tile(chunks[k]) for k in range(nch // 2, nch)}
            acc_ref[...] += _dot(_tied_operand(vt_ref, ties), jnp.concatenate(p, axis=0))

    @pl.when(s == nblk)
    def _():
        a_rd, w_rd = bufs[(nblk - 1) % 2]
        acc = acc_ref[...] + _dot(vt_ref[...], jnp.concatenate(gated(a_rd, w_rd), axis=0))
        x2 = x1_ref[...] + g2_ref[...] * acc.T
        if final:
            x2 = x2 * lax.rsqrt(jnp.mean(x2 * x2, axis=-1, keepdims=True) + EPS) * fg_ref[...]
        o_ref[...] = x2


def _peer_call(l, h2, u_bf, vt_bf, r1, e1, cnt, e0, x1, mods, final_g, *, n_ctx, lat_len, tm, final):
    n, d = x1.shape
    ib = 8
    eb = ib * PEER_KEYS
    nblk = u_bf.shape[1] // eb
    nb_ctx, bpl = n_ctx // tm, lat_len // tm
    grp = lambda i: _group_of_block(i, nb_ctx, bpl)
    once = dict(pipeline_mode=pl.Buffered(1))
    cur = lambda s: jnp.minimum(s, nblk - 1)
    prev = lambda s: jnp.maximum(s - 1, 0)
    in_specs = [
        pl.BlockSpec((tm, d), lambda i, s: (i, 0), **once),
        pl.BlockSpec((None, eb, d), lambda i, s: (l, cur(s), 0)),
        pl.BlockSpec((None, d, eb), lambda i, s: (l, 0, prev(s))),
        pl.BlockSpec((PEER_HEADS, PEER_KEYS, tm), lambda i, s: (0, 0, i), **once),
        pl.BlockSpec((PEER_HEADS, PEER_KEYS, tm), lambda i, s: (0, 0, i), **once),
        pl.BlockSpec((PEER_HEADS, ib, tm), lambda i, s: (0, cur(s), i)),
        pl.BlockSpec((PEER_HEADS, ib, tm), lambda i, s: (0, cur(s), i)),
        pl.BlockSpec((tm, d), lambda i, s: (i, 0), **once),
        pl.BlockSpec((None, None, None, 1, d), lambda i, s: (l, grp(i), 5, 0, 0)),
    ]
    args = [h2, u_bf, vt_bf, r1, e1, cnt, e0, x1, mods]
    if final:
        in_specs.append(pl.BlockSpec((1, d), lambda i, s: (0, 0)))
        args.append(final_g.reshape(1, d))
    nstage = 2 * PEER_HEADS
    return pl.pallas_call(
        functools.partial(_peer_kernel, final=final, ib=ib, nblk=nblk),
        out_shape=jax.ShapeDtypeStruct((n, d), F32),
        grid=(n // tm, nblk + 1),
        in_specs=in_specs,
        out_specs=pl.BlockSpec((tm, d), lambda i, s: (i, 0)),
        scratch_shapes=[pltpu.VMEM((d, tm), F32),
                        pltpu.VMEM((eb, tm), F32), pltpu.VMEM((eb, tm), F32),
                        pltpu.VMEM((eb, tm), BF16), pltpu.VMEM((eb, tm), BF16),
                        pltpu.VMEM((nstage, 16, tm), BF16), pltpu.VMEM((nstage, 16, tm), BF16)],
        compiler_params=_cparams(("parallel", "arbitrary")),
        name="peer_experts",
    )(*args)


def kernel(x_prompt, x_sample, cache_k_win, cache_v_win, cache_k_axial, cache_v_axial, cache_k_diff,
           cache_v_diff, c, c_ctx, norm1_g, w_ada, b_ada, w_in, win_sink, ax_q_g, ax_k_g, lam_q1, lam_k1,
           lam_q2, lam_k2, diff_subln_g, w_out, norm2_g, peer_wq, peer_keys, peer_u, peer_v, final_g):
    batch, seq, d = x_prompt.shape
    dec_batch, lat_len, _ = x_sample.shape
    depth = w_in.shape[0]
    past = cache_k_win.shape[2]
    n_ctx = batch * seq
    n_lat = dec_batch * lat_len
    n = n_ctx + n_lat
    tm = 512
    tq_ctx = seq
    tq_lat = 256

    x = jnp.concatenate([x_prompt.reshape(n_ctx, d), x_sample.reshape(n_lat, d)], axis=0)
    cond = jnp.zeros((8, d), F32).at[0].set(c_ctx).at[1:1 + dec_batch].set(c)
    mods = _ada_call(cond, w_ada, b_ada).reshape(depth, 8, 6, 1, d)
    kw = _kw_call(peer_keys, peer_wq)
    w_in_bf = w_in.astype(BF16)
    w_out_bf = w_out.astype(BF16)
    u_bf = peer_u.astype(BF16)
    vt_bf = _vt_call(peer_v)
    tables = _rope_tables(lat_len)

    cache_a = (cache_k_win.reshape(dec_batch, depth, past, -1), cache_v_win.reshape(dec_batch, depth, past, -1))
    cache_b = (cache_k_axial.reshape(dec_batch, depth, past, -1), cache_v_axial.reshape(dec_batch, depth, past, -1))
    cache_c = (cache_k_diff.reshape(dec_batch, depth, past, -1), cache_v_diff.reshape(dec_batch, depth, past, -1))
    lam_params = (lam_q1, lam_k1, lam_q2, lam_k2)

    o = jnp.zeros((n, d), BF16)
    z = jnp.zeros((n, IN_COLS), BF16)
    kv_out = [[] for _ in range(6)]
    for l in range(depth):
        lam_init = 0.8 - 0.6 * math.exp(-0.3 * l)
        geo = dict(n_ctx=n_ctx, lat_len=lat_len, tm=tm)
        res = _in_call(l, x, mods, norm1_g, w_in_bf, ax_q_g, ax_k_g, None, z, latent=False, **geo)
        z = res[0]
        for k in range(6):
            kv_out[k].append(res[1 + k])
        z = _in_call(l, x, mods, norm1_g, w_in_bf, ax_q_g, ax_k_g, tables, z, latent=True, **geo)

        ctx = dict(nbatch=batch, seq=seq, row0=0, tq=tq_ctx)
        lat = dict(nbatch=dec_batch, seq=lat_len, row0=n_ctx, tq=tq_lat)
        win = dict(qcol=QA, kcol=KA, vcol=VA, ocol=0, heads=WIN_HEADS, kvh=WIN_KV, sink=win_sink)
        axl = dict(qcol=QB, kcol=KB, vcol=VB, ocol=WIN_HEADS * HEAD_DIM, heads=AX_HEADS, kvh=AX_KV)
        o = _gqa_call(l, z, o, **win, **ctx)
        o = _gqa_call(l, z, o, **axl, **ctx)
        o = _diff_call(l, z, o, lam_params, diff_subln_g, lam_init=lam_init, **ctx)
        o = _gqa_call(l, z, o, **win, **lat, window=True, cache=cache_a)
        lat2 = dict(lat, tq=2 * tq_lat if lat_len % (2 * tq_lat) == 0 else tq_lat)
        o = _gqa_call(l, z, o, **axl, **lat2, cache=cache_b)
        o = _diff_call(l, z, o, lam_params, diff_subln_g, lam_init=lam_init, cache=cache_c, **lat2)

        x1, h2 = _out_call(l, o, x, w_out_bf, mods, norm2_g, **geo)
        s_t = _score_call(l, kw, h2, tm=tm)
        r1, e1, cnt, e0 = _router_call(s_t)
        x = _peer_call(l, h2, u_bf, vt_bf, r1, e1, cnt, e0, x1, mods, final_g,
                       final=(l == depth - 1), **geo)

    y_prompt = x[:n_ctx].reshape(batch, seq, d)
    y_sample = x[n_ctx:].reshape(dec_batch, lat_len, d)
    outs = []
    for k, nh in enumerate((WIN_KV, WIN_KV, AX_KV, AX_KV, DIFF_HEADS, DIFF_HEADS)):
        outs.append(jnp.stack([a.reshape(batch, seq, nh, HEAD_DIM) for a in kv_out[k]], axis=1))
    return (y_prompt, y_sample, *outs)
```

```python
import functools
import math

import jax
import jax.numpy as jnp
from jax import lax
from jax.experimental import pallas as pl
from jax.experimental.pallas import tpu as pltpu

F32 = jnp.float32
BF16 = jnp.bfloat16

HEAD_DIM = 128
GRID_W = 64
WINDOW = 128
ROPE_THETA = 10000.0
PEER_HEADS = 8
PEER_KEYS = 128
PEER_TOPK = 16
EPS = 1e-6
NEG_INF = -1e30
LOG2E = 1.4426950408889634

QA, KA, VA, QB, KB, VB, QC, KC, VC = 0, 512, 768, 1024, 2048, 2304, 2560, 3072, 3584
IN_COLS = 4096
WIN_HEADS, WIN_KV, AX_HEADS, AX_KV, DIFF_HEADS = 4, 2, 8, 2, 4

VMEM_LIMIT = 56 * 1024 * 1024


def _cparams(sem, **kw):
    return pltpu.CompilerParams(dimension_semantics=sem, vmem_limit_bytes=VMEM_LIMIT, **kw)


def _dot_nt(a, b):
    return lax.dot_general(a, b, (((1,), (1,)), ((), ())), preferred_element_type=F32)


def _dot(a, b):
    return jnp.dot(a, b, preferred_element_type=F32)


def _ada_kernel(cond_ref, w_ref, b_ref, o_ref):
    c = cond_ref[...]
    s = (c * jax.nn.sigmoid(c)).astype(BF16)
    o_ref[...] = _dot(s, w_ref[...].astype(BF16)) + b_ref[...]


def _ada_call(cond, w_ada, b_ada):
    depth, d, n6 = w_ada.shape
    tn = 1024
    return pl.pallas_call(
        _ada_kernel,
        out_shape=jax.ShapeDtypeStruct((depth, 8, n6), F32),
        grid=(depth, n6 // tn),
        in_specs=[
            pl.BlockSpec((8, d), lambda l, j: (0, 0)),
            pl.BlockSpec((None, d, tn), lambda l, j: (l, 0, j)),
            pl.BlockSpec((None, 1, tn), lambda l, j: (l, 0, j)),
        ],
        out_specs=pl.BlockSpec((None, 8, tn), lambda l, j: (l, 0, j)),
        compiler_params=_cparams(("parallel", "parallel")),
        name="ada",
    )(cond, w_ada, b_ada.reshape(depth, 1, n6))


def _kw_kernel(keys_ref, wq_ref, o_ref):
    kw = lax.dot_general(keys_ref[...], wq_ref[...], (((1,), (1,)), ((), ())),
                         preferred_element_type=F32, precision=lax.Precision.HIGHEST)
    o_ref[...] = kw.astype(BF16)


def _kw_call(peer_keys, peer_wq):
    depth, d, nq = peer_wq.shape
    ngrp = PEER_HEADS * 2
    sub = PEER_KEYS
    return pl.pallas_call(
        _kw_kernel,
        out_shape=jax.ShapeDtypeStruct((depth, ngrp * sub, d), BF16),
        grid=(depth, ngrp),
        in_specs=[
            pl.BlockSpec((None, None, None, sub, sub), lambda l, g: (l, g // 2, g % 2, 0, 0)),
            pl.BlockSpec((None, d, sub), lambda l, g: (l, 0, g)),
        ],
        out_specs=pl.BlockSpec((None, sub, d), lambda l, g: (l, g, 0)),
        compiler_params=_cparams(("parallel", "parallel")),
        name="peer_kw",
    )(peer_keys, peer_wq)


def _vt_kernel(v_ref, o_ref):
    o_ref[...] = v_ref[...].T.astype(BF16)


def _vt_call(peer_v):
    depth, n_exp, d = peer_v.shape
    te = 512
    return pl.pallas_call(
        _vt_kernel,
        out_shape=jax.ShapeDtypeStruct((depth, d, n_exp), BF16),
        grid=(depth, n_exp // te),
        in_specs=[pl.BlockSpec((None, te, d), lambda l, j: (l, j, 0))],
        out_specs=pl.BlockSpec((None, d, te), lambda l, j: (l, 0, j)),
        compiler_params=_cparams(("parallel", "parallel")),
        name="peer_vt",
    )(peer_v)


def _rope_tables(n_tok):
    n_rows = n_tok // GRID_W
    row_pos = jnp.repeat(jnp.arange(n_rows, dtype=jnp.int32), GRID_W).astype(F32)
    col_pos = jnp.tile(jnp.arange(GRID_W, dtype=jnp.int32), n_rows).astype(F32)
    lane = jnp.arange(HEAD_DIM)

    def table(pair, use_row):
        freqs = ROPE_THETA ** (-jnp.arange(pair, dtype=F32) / pair)
        f = freqs[lane % pair]
        pos = jnp.where(use_row[None, :], row_pos[:, None], col_pos[:, None])
        ang = pos * f[None, :]
        first = (lane % (2 * pair)) < pair
        return jnp.cos(ang), jnp.where(first[None, :], -jnp.sin(ang), jnp.sin(ang))

    c_ax, s_ax = table(32, lane < 64)
    c_df, s_df = table(16, (lane % 64) < 32)
    return c_ax, s_ax, c_df, s_df


def _rope(y, c, s, pair):
    lane = lax.broadcasted_iota(jnp.int32, y.shape, 1)
    first = (lane % (2 * pair)) < pair
    partner = jnp.where(first, pltpu.roll(y, HEAD_DIM - pair, axis=1), pltpu.roll(y, pair, axis=1))
    return y * c + partner * s


def _head_norm(y, g):
    return y * lax.rsqrt(jnp.mean(y * y, axis=-1, keepdims=True) + EPS) * g


def _in_kernel(*refs, latent):
    if latent:
        (x_ref, sh_ref, sc_ref, g_ref, w_ref, axq_ref, axk_ref, ca_ref, sa_ref, cd_ref, sd_ref,
         _z_in, z_ref, h_ref) = refs
        ka_ref = va_ref = kb_ref = vb_ref = kc_ref = vc_ref = None
    else:
        (x_ref, sh_ref, sc_ref, g_ref, w_ref, axq_ref, axk_ref, _z_in,
         z_ref, ka_ref, va_ref, kb_ref, vb_ref, kc_ref, vc_ref, h_ref) = refs
    j = pl.program_id(1)

    @pl.when(j == 0)
    def _():
        x = x_ref[...]
        y = x * lax.rsqrt(jnp.mean(x * x, axis=-1, keepdims=True) + EPS) * g_ref[...]
        h_ref[...] = (y * (1.0 + sc_ref[...]) + sh_ref[...]).astype(BF16)

    def rope_ax(y):
        return _rope(y, ca_ref[...], sa_ref[...], 32) if latent else y

    def rope_df(y):
        return _rope(y, cd_ref[...], sd_ref[...], 16) if latent else y

    def head(z, k):
        return z[:, k * HEAD_DIM:(k + 1) * HEAD_DIM]

    def put(k, y):
        z_ref[:, k * HEAD_DIM:(k + 1) * HEAD_DIM] = y.astype(BF16)

    def keep(ref, k, y):
        if not latent:
            ref[:, k * HEAD_DIM:(k + 1) * HEAD_DIM] = y

    @pl.when(j == 0)
    def _():
        z = _dot(h_ref[...], w_ref[...])
        for k in range(4):
            put(k, rope_ax(head(z, k)))
        for k in range(2):
            y = head(z, 4 + k)
            keep(ka_ref, k, y)
            put(4 + k, rope_ax(y))
        for k in range(2):
            y = head(z, 6 + k)
            keep(va_ref, k, y)
            put(6 + k, y)

    @pl.when(j == 1)
    def _():
        z = _dot(h_ref[...], w_ref[...])
        for k in range(8):
            put(k, rope_ax(_head_norm(head(z, k), axq_ref[...])))

    @pl.when(j == 2)
    def _():
        z = _dot(h_ref[...], w_ref[...])
        for k in range(2):
            y = _head_norm(head(z, k), axk_ref[...])
            keep(kb_ref, k, y)
            put(k, rope_ax(y))
        for k in range(2):
            y = head(z, 2 + k)
            keep(vb_ref, k, y)
            put(2 + k, y)
        for k in range(4):
            put(4 + k, rope_df(head(z, 4 + k)))

    @pl.when(j == 3)
    def _():
        z = _dot(h_ref[...], w_ref[...])
        for k in range(4):
            y = head(z, k)
            keep(kc_ref, k, y)
            put(k, rope_df(y))
        for k in range(4):
            y = head(z, 4 + k)
            keep(vc_ref, k, y)
            put(4 + k, y)


def _in_call(l, x, mods, norm1_g, w_in, ax_q_g, ax_k_g, tables, z, *, latent, n_ctx, lat_len, tm):
    n, d = x.shape
    depth = w_in.shape[0]
    nb_ctx = n_ctx // tm
    bpl = lat_len // tm
    if latent:
        nblk = (n - n_ctx) // tm
        row = lambda i: i + nb_ctx
        grp = lambda i: 1 + i // bpl
    else:
        nblk = nb_ctx
        row = lambda i: i
        grp = lambda i: 0
    cb = 1024
    in_specs = [
        pl.BlockSpec((tm, d), lambda i, j: (row(i), 0)),
        pl.BlockSpec((None, None, None, 1, d), lambda i, j: (l, grp(i), 0, 0, 0)),
        pl.BlockSpec((None, None, None, 1, d), lambda i, j: (l, grp(i), 1, 0, 0)),
        pl.BlockSpec((None, 1, d), lambda i, j: (l, 0, 0)),
        pl.BlockSpec((None, d, cb), lambda i, j: (l, 0, j)),
        pl.BlockSpec((None, 1, HEAD_DIM), lambda i, j: (l, 0, 0)),
        pl.BlockSpec((None, 1, HEAD_DIM), lambda i, j: (l, 0, 0)),
    ]
    args = [x, mods, mods, norm1_g.reshape(depth, 1, d), w_in,
            ax_q_g.reshape(depth, 1, HEAD_DIM), ax_k_g.reshape(depth, 1, HEAD_DIM)]
    z_spec = pl.BlockSpec((tm, cb), lambda i, j: (row(i), j))
    z_shape = jax.ShapeDtypeStruct((n, IN_COLS), BF16)
    scratch = [pltpu.VMEM((tm, d), BF16)]
    if latent:
        tspec = pl.BlockSpec((tm, HEAD_DIM), lambda i, j: (i % bpl, 0))
        in_specs += [tspec] * 4 + [pl.BlockSpec(memory_space=pl.ANY)]
        args += list(tables) + [z]
        return pl.pallas_call(
            functools.partial(_in_kernel, latent=True),
            out_shape=z_shape, grid=(nblk, 4), in_specs=in_specs, out_specs=z_spec,
            scratch_shapes=scratch, input_output_aliases={len(args) - 1: 0},
            compiler_params=_cparams(("parallel", "arbitrary")), name="in_proj_latent",
        )(*args)
    kv2 = jax.ShapeDtypeStruct((n_ctx, 2 * HEAD_DIM), F32)
    kv4 = jax.ShapeDtypeStruct((n_ctx, 4 * HEAD_DIM), F32)
    s2 = pl.BlockSpec((tm, 2 * HEAD_DIM), lambda i, j: (i, 0))
    s4 = pl.BlockSpec((tm, 4 * HEAD_DIM), lambda i, j: (i, 0))
    in_specs.append(pl.BlockSpec(memory_space=pl.ANY))
    args.append(z)
    return pl.pallas_call(
        functools.partial(_in_kernel, latent=False),
        out_shape=(z_shape, kv2, kv2, kv2, kv2, kv4, kv4), grid=(nblk, 4), in_specs=in_specs,
        out_specs=(z_spec, s2, s2, s2, s2, s4, s4), scratch_shapes=scratch,
        input_output_aliases={len(args) - 1: 0},
        compiler_params=_cparams(("parallel", "arbitrary")), name="in_proj_context",
    )(*args)


def _gqa_kernel(*refs, groups, tq, window, has_cache, has_sink, l, lat_len):
    refs = list(refs)
    sink_ref = refs.pop(0) if has_sink else None
    q_ref, k_ref, v_ref = refs[:3]
    refs = refs[3:]
    if has_cache:
        ck_ref, cv_ref = refs[:2]
        refs = refs[2:]
    o_ref = refs[-1]
    kv = pl.program_id(1)
    qi = pl.program_id(2)
    scale = HEAD_DIM ** -0.5

    if window:
        span = tq + 2 * WINDOW
        start = jnp.clip(qi * tq - WINDOW, 0, lat_len - span)
        start = pl.multiple_of(start, WINDOW)
        k = k_ref[pl.ds(start, span), :]
        v = v_ref[pl.ds(start, span), :]
        qpos = qi * tq + lax.broadcasted_iota(jnp.int32, (tq, span), 0)
        kpos = start + lax.broadcasted_iota(jnp.int32, (tq, span), 1)
        ok = jnp.abs(qpos - kpos) <= WINDOW
    else:
        k = k_ref[...]
        v = v_ref[...]
    if has_cache:
        ck = ck_ref[...].astype(BF16)
        cv = cv_ref[...].astype(BF16)

    c2 = scale * LOG2E
    rows = tq if window else min(tq, 256)
    for r0 in range(0, tq, rows):
        for g in range(groups):
            q = q_ref[r0:r0 + rows, g * HEAD_DIM:(g + 1) * HEAD_DIM]
            s = _dot_nt(q, k)
            if window:
                s = jnp.where(ok, s, NEG_INF)
            m = jnp.max(s, axis=-1, keepdims=True)
            if has_cache:
                s2 = _dot_nt(q, ck)
                m = jnp.maximum(m, jnp.max(s2, axis=-1, keepdims=True))
            if has_sink:
                sink = sink_ref[l, kv * groups + g] * (1.0 / scale)
                m = jnp.maximum(m, sink)
            e = jnp.exp2((s - m) * c2)
            den = jnp.sum(e, axis=-1, keepdims=True)
            o = _dot(e.astype(BF16), v)
            if has_cache:
                e2 = jnp.exp2((s2 - m) * c2)
                den = den + jnp.sum(e2, axis=-1, keepdims=True)
                o = o + _dot(e2.astype(BF16), cv)
            if has_sink:
                den = den + jnp.exp2((sink - m) * c2)
            o_ref[r0:r0 + rows, g * HEAD_DIM:(g + 1) * HEAD_DIM] = (o / den).astype(BF16)


def _gqa_call(l, z, o, *, qcol, kcol, vcol, ocol, heads, kvh, nbatch, seq, row0, tq,
              window=False, cache=None, sink=None):
    groups = heads // kvh
    gw = groups * HEAD_DIM
    nq = seq // tq
    qrow = lambda b, qi: row0 // tq + b * nq + qi
    krow = lambda b: row0 // seq + b
    in_specs, args = [], []
    if sink is not None:
        in_specs.append(pl.BlockSpec(memory_space=pltpu.SMEM))
        args.append(sink)
    in_specs += [
        pl.BlockSpec((tq, gw), lambda b, kv, qi: (qrow(b, qi), qcol // gw + kv)),
        pl.BlockSpec((seq, HEAD_DIM), lambda b, kv, qi: (krow(b), kcol // HEAD_DIM + kv)),
        pl.BlockSpec((seq, HEAD_DIM), lambda b, kv, qi: (krow(b), vcol // HEAD_DIM + kv)),
    ]
    args += [z, z, z]
    if cache is not None:
        past = cache[0].shape[2]
        cspec = pl.BlockSpec((None, None, past, HEAD_DIM), lambda b, kv, qi: (b, l, 0, kv))
        in_specs += [cspec, cspec]
        args += list(cache)
    in_specs.append(pl.BlockSpec(memory_space=pl.ANY))
    args.append(o)
    kern = functools.partial(_gqa_kernel, groups=groups, tq=tq, window=window,
                             has_cache=cache is not None, has_sink=sink is not None, l=l, lat_len=seq)
    return pl.pallas_call(
        kern, out_shape=jax.ShapeDtypeStruct(o.shape, o.dtype), grid=(nbatch, kvh, nq),
        in_specs=in_specs,
        out_specs=pl.BlockSpec((tq, gw), lambda b, kv, qi: (qrow(b, qi), ocol // gw + kv)),
        input_output_aliases={len(args) - 1: 0},
        compiler_params=_cparams(("parallel", "parallel", "arbitrary")),
        name="attn_window" if sink is not None else "attn_axial",
    )(*args)


def _diff_kernel(*refs, has_cache, lam_init, chain_rows, hps):
    refs = list(refs)
    lq1, lk1, lq2, lk2, gain_ref, q_ref, k_ref, v_ref = refs[:8]
    refs = refs[8:]
    if has_cache:
        ck_ref, cv_ref = refs[:2]
    o_ref = refs[-1]
    h = pl.program_id(1)
    half = HEAD_DIM // 2
    scale = half ** -0.5

    lam = (jnp.exp(jnp.sum(lq1[...] * lk1[...], axis=-1, keepdims=True))
           - jnp.exp(jnp.sum(lq2[...] * lk2[...], axis=-1, keepdims=True)) + lam_init)

    lane = lax.broadcasted_iota(jnp.int32, (1, HEAD_DIM), 1)
    lo = jnp.where(lane < half, 1.0, 0.0).astype(BF16)
    c2 = scale * LOG2E

    for hh in range(hps):
        cols = slice(hh * HEAD_DIM, (hh + 1) * HEAD_DIM)
        k = k_ref[:, cols]
        v = v_ref[:, cols]
        if has_cache:
            ck = ck_ref[:, cols].astype(BF16)
            cv = cv_ref[:, cols].astype(BF16)
        gain = gain_ref[pl.ds(h * hps + hh, 1), :]
        for r0 in range(0, q_ref.shape[0], chain_rows):
            q = q_ref[r0:r0 + chain_rows, cols]
            qs = (q * lo, q * (1.0 - lo))
            probs = []
            for c in range(2):
                s = _dot_nt(qs[c], k)
                m = jnp.max(s, axis=-1, keepdims=True)
                if has_cache:
                    s2 = _dot_nt(qs[c], ck)
                    m = jnp.maximum(m, jnp.max(s2, axis=-1, keepdims=True))
                e = jnp.exp2((s - m) * c2)
                den = jnp.sum(e, axis=-1, keepdims=True)
                if has_cache:
                    e2 = jnp.exp2((s2 - m) * c2)
                    den = den + jnp.sum(e2, axis=-1, keepdims=True)
                else:
                    e2 = None
                probs.append((e, e2, 1.0 / den))
            (e_a, e2_a, r_a), (e_b, e2_b, r_b) = probs
            r_b = lam * r_b
            o = _dot((e_a * r_a - e_b * r_b).astype(BF16), v)
            if has_cache:
                o = o + _dot((e2_a * r_a - e2_b * r_b).astype(BF16), cv)
            o = o * lax.rsqrt(jnp.mean(o * o, axis=-1, keepdims=True) + EPS) * gain
            o_ref[r0:r0 + chain_rows, cols] = (o * (1.0 - lam_init)).astype(BF16)


def _diff_call(l, z, o, lam_params, gain, *, nbatch, seq, row0, tq, lam_init, cache=None):
    nq = seq // tq
    depth = gain.shape[0]
    half = HEAD_DIM // 2
    qrow = lambda b, qi: row0 // tq + b * nq + qi
    krow = lambda b: row0 // seq + b
    lspec = pl.BlockSpec((None, 1, half), lambda b, h, qi: (l, 0, 0))
    hps = 2
    hw = hps * HEAD_DIM
    in_specs = [lspec] * 4 + [
        pl.BlockSpec((None, DIFF_HEADS, HEAD_DIM), lambda b, h, qi: (l, 0, 0)),
        pl.BlockSpec((tq, hw), lambda b, h, qi: (qrow(b, qi), QC // hw + h)),
        pl.BlockSpec((seq, hw), lambda b, h, qi: (krow(b), KC // hw + h)),
        pl.BlockSpec((seq, hw), lambda b, h, qi: (krow(b), VC // hw + h)),
    ]
    args = [p.reshape(depth, 1, half) for p in lam_params] + [gain, z, z, z]
    if cache is not None:
        past = cache[0].shape[2]
        cspec = pl.BlockSpec((None, None, past, hw), lambda b, h, qi: (b, l, 0, h))
        in_specs += [cspec, cspec]
        args += list(cache)
    in_specs.append(pl.BlockSpec(memory_space=pl.ANY))
    args.append(o)
    ocol0 = (WIN_HEADS + AX_HEADS) * HEAD_DIM
    return pl.pallas_call(
        functools.partial(_diff_kernel, has_cache=cache is not None, lam_init=lam_init,
                          chain_rows=min(tq, 256), hps=hps),
        out_shape=jax.ShapeDtypeStruct(o.shape, o.dtype), grid=(nbatch, DIFF_HEADS // hps, nq),
        in_specs=in_specs,
        out_specs=pl.BlockSpec((tq, hw), lambda b, h, qi: (qrow(b, qi), ocol0 // hw + h)),
        input_output_aliases={len(args) - 1: 0},
        compiler_params=_cparams(("parallel", "parallel", "arbitrary")),
        name="attn_diff",
    )(*args)


def _out_kernel(o_ref, x_ref, w_ref, g1_ref, sh2_ref, sc2_ref, n2g_ref, x1_ref, h2_ref):
    a = _dot(o_ref[...], w_ref[...])
    x1 = x_ref[...] + g1_ref[...] * a
    x1_ref[...] = x1
    y = x1 * lax.rsqrt(jnp.mean(x1 * x1, axis=-1, keepdims=True) + EPS) * n2g_ref[...]
    h2_ref[...] = (y * (1.0 + sc2_ref[...]) + sh2_ref[...]).astype(BF16)


def _group_of_block(i, nb_ctx, bpl):
    return jnp.where(i < nb_ctx, 0, 1 + (i - nb_ctx) // bpl)


def _out_call(l, o, x, w_out, mods, norm2_g, *, n_ctx, lat_len, tm):
    n, d = x.shape
    depth = w_out.shape[0]
    nb_ctx, bpl = n_ctx // tm, lat_len // tm
    grp = lambda i: _group_of_block(i, nb_ctx, bpl)
    mspec = lambda which: pl.BlockSpec((None, None, None, 1, d), lambda i: (l, grp(i), which, 0, 0))
    return pl.pallas_call(
        _out_kernel,
        out_shape=(jax.ShapeDtypeStruct((n, d), F32), jax.ShapeDtypeStruct((n, d), BF16)),
        grid=(n // tm,),
        in_specs=[
            pl.BlockSpec((tm, d), lambda i: (i, 0)),
            pl.BlockSpec((tm, d), lambda i: (i, 0)),
            pl.BlockSpec((None, d, d), lambda i: (l, 0, 0)),
            mspec(2), mspec(3), mspec(4),
            pl.BlockSpec((None, 1, d), lambda i: (l, 0, 0)),
        ],
        out_specs=(pl.BlockSpec((tm, d), lambda i: (i, 0)), pl.BlockSpec((tm, d), lambda i: (i, 0))),
        compiler_params=_cparams(("parallel",)),
        name="out_proj",
    )(o, x, w_out, mods, mods, mods, norm2_g.reshape(depth, 1, d))


def _score_kernel(kw_ref, h2_ref, o_ref):
    o_ref[...] = _dot_nt(kw_ref[...], h2_ref[...])


def _score_call(l, kw, h2, *, tm):
    n, d = h2.shape
    rows = kw.shape[1]
    return pl.pallas_call(
        _score_kernel,
        out_shape=jax.ShapeDtypeStruct((rows, n), F32),
        grid=(n // tm,),
        in_specs=[pl.BlockSpec((None, rows, d), lambda i: (l, 0, 0)),
                  pl.BlockSpec((tm, d), lambda i: (i, 0))],
        out_specs=pl.BlockSpec((rows, tm), lambda i: (0, i)),
        compiler_params=_cparams(("parallel",)),
        name="peer_scores",
    )(kw, h2)


def _sort_pairs(n):
    pairs = []

    def merge(lo, hi, r):
        step = r * 2
        if step < hi - lo:
            merge(lo, hi, step)
            merge(lo + r, hi, step)
            for i in range(lo + r, hi - r, step):
                pairs.append((i, i + r))
        else:
            pairs.append((lo, lo + r))

    def sort(lo, hi):
        if hi - lo >= 1:
            mid = lo + (hi - lo) // 2
            sort(lo, mid)
            sort(mid + 1, hi)
            merge(lo, hi, 1)

    sort(0, n - 1)
    return pairs


_SORT16 = _sort_pairs(PEER_TOPK)


def _vmax(a, b):
    if a is None:
        return b
    if b is None:
        return a
    return jnp.maximum(a, b)


def _vmin(a, b):
    if a is None or b is None:
        return None
    return jnp.minimum(a, b)


def _bitonic_merge_desc(x):
    x = list(x)
    d = PEER_TOPK // 2
    while d >= 1:
        for i in range(PEER_TOPK):
            if (i & d) == 0:
                a, b = x[i], x[i + d]
                x[i], x[i + d] = _vmax(a, b), _vmin(a, b)
        d //= 2
    return x


def _merge_top16(a, b):
    a = list(a) + [None] * (PEER_TOPK - len(a))
    b = list(b) + [None] * (PEER_TOPK - len(b))
    return _bitonic_merge_desc([_vmax(a[k], b[PEER_TOPK - 1 - k]) for k in range(PEER_TOPK)])


def _router_kernel(s_ref, r1_ref, e1_ref, cnt_ref, e0_ref, top_ref):
    nk = PEER_KEYS
    nv = nk // 8
    tk = PEER_TOPK
    sub = lax.broadcasted_iota(jnp.int32, (8, s_ref.shape[1]), 0)

    for g in range(2 * PEER_HEADS):
        x = [s_ref[g * nk + 8 * r:g * nk + 8 * r + 8, :] for r in range(nv)]
        for (i, j) in _SORT16:
            x[i], x[j] = jnp.maximum(x[i], x[j]), jnp.minimum(x[i], x[j])
        for shift in (4, 2, 1):
            y = [pltpu.roll(v, shift, axis=0) for v in x]
            x = _bitonic_merge_desc([jnp.maximum(x[k], y[tk - 1 - k]) for k in range(tk)])
        for a in range(tk):
            top_ref[g * tk + a] = x[a]

    def packed(c, a):
        out = top_ref[c * tk + a]
        for h in range(1, PEER_HEADS):
            out = jnp.where(sub == h, top_ref[(2 * h + c) * tk + a], out)
        return out

    p0 = [packed(0, a) for a in range(tk)]
    p1 = [packed(1, b) for b in range(tk)]
    rows = [[p0[a] + p1[b] for b in range(tk // (a + 1))] for a in range(8)]
    col0 = [p0[a] + p1[0] for a in range(8, tk)]
    t1 = _merge_top16(rows[0], _merge_top16(rows[1], col0))
    t2 = _merge_top16(_merge_top16(rows[2], rows[3]), _merge_top16(rows[4], rows[5]))
    t3 = _merge_top16(rows[6], rows[7])
    best = _merge_top16(t1, _merge_top16(t2, t3))
    tau_p = best[tk - 1]
    zsum = jnp.ones_like(tau_p)
    for k in range(1, tk):
        zsum = zsum + jnp.exp(best[k] - best[0])
    zinv_p = 1.0 / zsum

    cnt_p = []
    for a in range(tk):
        acc = jnp.zeros_like(tau_p)
        for b in range(tk):
            acc = acc + jnp.where(p0[a] + p1[b] >= tau_p, 1.0, 0.0)
        cnt_p.append(acc)

    def pair_words(v):
        bits = pltpu.bitcast(v, jnp.uint32)
        return bits | lax.shift_right_logical(bits, jnp.uint32(16))

    for h in range(PEER_HEADS):
        shape = (8, s_ref.shape[1])
        zinv = jnp.broadcast_to(zinv_p[h:h + 1, :], shape)
        cnt_a = [jnp.broadcast_to(cnt_p[a][h:h + 1, :], shape) for a in range(tk)]
        t0 = [top_ref[(2 * h) * tk + a] for a in range(tk)]
        t1h = [top_ref[(2 * h + 1) * tk + b] for b in range(tk)]
        for rr in range(nv // 2):
            ranks, gates = [], []
            for r in (2 * rr, 2 * rr + 1):
                s0 = s_ref[(2 * h) * nk + 8 * r:(2 * h) * nk + 8 * r + 8, :]
                s1 = s_ref[(2 * h + 1) * nk + 8 * r:(2 * h + 1) * nk + 8 * r + 8, :]
                rank = jnp.full(shape, float(tk), F32)
                cnt = jnp.zeros(shape, F32)
                for b in range(tk - 1, -1, -1):
                    rank = jnp.where(s1 >= t1h[b], float(b), rank)
                    cnt = jnp.where(s0 >= t0[b], cnt_a[b], cnt)
                ranks.append(rank)
                gates.append(jnp.exp(s1 - t1h[0]))
                e0 = (jnp.exp(s0 - t0[0]) * zinv).astype(BF16).astype(F32)
                cnt_ref[h, 8 * r:8 * r + 8, :] = pair_words(cnt)
                e0_ref[h, 8 * r:8 * r + 8, :] = pair_words(e0)
            r1_ref[h, 16 * rr:16 * rr + 16, :] = jnp.concatenate(ranks, axis=0).astype(BF16)
            e1_ref[h, 16 * rr:16 * rr + 16, :] = jnp.concatenate(gates, axis=0).astype(BF16)


def _router_call(s_t):
    rows, n = s_t.shape
    tl = 128
    half = jax.ShapeDtypeStruct((PEER_HEADS, PEER_KEYS, n), BF16)
    words = jax.ShapeDtypeStruct((PEER_HEADS, PEER_KEYS, n), jnp.uint32)
    spec = pl.BlockSpec((PEER_HEADS, PEER_KEYS, tl), lambda i: (0, 0, i))
    return pl.pallas_call(
        _router_kernel,
        out_shape=(half, half, words, words),
        grid=(n // tl,),
        in_specs=[pl.BlockSpec((rows, tl), lambda i: (0, i))],
        out_specs=(spec, spec, spec, spec),
        scratch_shapes=[pltpu.VMEM((2 * PEER_HEADS * PEER_TOPK, 8, tl), F32)],
        compiler_params=_cparams(("parallel",)),
        name="peer_router",
    )(s_t)


def _gelu(a):
    c0 = math.sqrt(2.0 / math.pi)
    half = 0.5 * a
    return half + half * jnp.tanh(a * (c0 + (c0 * 0.044715) * (a * a)))


def _zero_tile(chunk):
    bits = pltpu.bitcast(chunk[:, 0:128], jnp.uint32)
    for lt in range(1, chunk.shape[1] // 128):
        bits = bits | pltpu.bitcast(chunk[:, lt * 128:(lt + 1) * 128], jnp.uint32)
    zero = lax.shift_right_logical(lax.shift_right_logical(bits, jnp.uint32(16)), jnp.uint32(16))
    return pltpu.bitcast(zero, BF16)


def _tied_operand(ref, ties):
    row_groups = []
    for rg in range(ref.shape[0] // 16):
        cols = []
        for c in range(ref.shape[1] // 256):
            piece = ref[rg * 16:(rg + 1) * 16, c * 256:(c + 1) * 256]
            z = ties.get((rg, c))
            cols.append(piece if z is None else piece + jnp.concatenate([z, z], axis=1))
        row_groups.append(jnp.concatenate(cols, axis=1))
    return jnp.concatenate(row_groups, axis=0)


def _peer_kernel(*refs, final, ib, nblk):
    if final:
        (h2_ref, u_ref, vt_ref, r1_ref, e1_ref, cnt_ref, e0_ref, x1_ref, g2_ref, fg_ref,
         o_ref, acc_ref, a0_ref, a1_ref, w0_ref, w1_ref, cs_ref, es_ref) = refs
    else:
        (h2_ref, u_ref, vt_ref, r1_ref, e1_ref, cnt_ref, e0_ref, x1_ref, g2_ref,
         o_ref, acc_ref, a0_ref, a1_ref, w0_ref, w1_ref, cs_ref, es_ref) = refs
    s = pl.program_id(1)
    t = h2_ref.shape[0]
    nchunk = PEER_KEYS // 16
    nch = ib * nchunk
    u_groups = u_ref.shape[0] // 16
    v_groups = vt_ref.shape[0] // 16

    def gates(w_ref):
        chunks = []
        for ii in range(ib):
            for h in range(PEER_HEADS):
                slot = (ii % 2) * PEER_HEADS + h
                cs_ref[slot] = pltpu.bitcast(jnp.broadcast_to(cnt_ref[h, ii:ii + 1, :], (8, t)), BF16)
                es_ref[slot] = pltpu.bitcast(jnp.broadcast_to(e0_ref[h, ii:ii + 1, :], (8, t)), BF16)
            w = [None] * nchunk
            for h in range(PEER_HEADS):
                cnt = cs_ref[(ii % 2) * PEER_HEADS + h]
                e0 = es_ref[(ii % 2) * PEER_HEADS + h]
                for jc in range(nchunk):
                    r1 = r1_ref[h, jc * 16:(jc + 1) * 16, :]
                    e1 = e1_ref[h, jc * 16:(jc + 1) * 16, :]
                    gate = jnp.where(r1 < cnt, e0, jnp.zeros_like(e0)) * e1
                    w[jc] = gate if w[jc] is None else w[jc] + gate
            for jc in range(nchunk):
                r0 = ii * PEER_KEYS + jc * 16
                w_ref[r0:r0 + 16, :] = w[jc]
                chunks.append(w[jc])
        return chunks

    def gated(a_ref, w_ref):
        return [w_ref[ch * 16:(ch + 1) * 16, :] * _gelu(a_ref[ch * 16:(ch + 1) * 16, :]).astype(BF16)
                for ch in range(nch)]

    def slot_u(slot):
        return (slot % u_groups, slot // u_groups)

    def slot_v(slot):
        return (slot % v_groups, slot // v_groups)

    @pl.when(s == 0)
    def _():
        acc_ref[...] = jnp.zeros_like(acc_ref)
        chunks = gates(w0_ref)
        ties = {slot_u(7 * k): _zero_tile(chunks[k]) for k in range(nch)}
        a0_ref[...] = _dot_nt(_tied_operand(u_ref, ties), h2_ref[...])

    bufs = ((a0_ref, w0_ref), (a1_ref, w1_ref))
    for parity in range(2):
        @pl.when((s > 0) & (s < nblk) & (s % 2 == parity))
        def _():
            a_wr, w_wr = bufs[parity]
            a_rd, w_rd = bufs[1 - parity]
            p = gated(a_rd, w_rd)
            chunks = gates(w_wr)
            ties = {slot_u(8 * c + 4): _zero_tile(p[c]) for c in range(nch)}
            ties.update({slot_u(16 * k): _zero_tile(chunks[k]) for k in range(nch // 2)})
            a_wr[...] = _dot_nt(_tied_operand(u_ref, ties), h2_ref[...])
            ties = {slot_v(16 * (k - nch // 2)): _zero_tile(chunks[k]) for k in range(nch // 2, nch)}
            acc_ref[...] += _dot(_tied_operand(vt_ref, ties), jnp.concatenate(p, axis=0))

    @pl.when(s == nblk)
    def _():
        a_rd, w_rd = bufs[(nblk - 1) % 2]
        acc = acc_ref[...] + _dot(vt_ref[...], jnp.concatenate(gated(a_rd, w_rd), axis=0))
        x2 = x1_ref[...] + g2_ref[...] * acc.T
        if final:
            x2 = x2 * lax.rsqrt(jnp.mean(x2 * x2, axis=-1, keepdims=True) + EPS) * fg_ref[...]
        o_ref[...] = x2


def _peer_call(l, h2, u_bf, vt_bf, r1, e1, cnt, e0, x1, mods, final_g, *, n_ctx, lat_len, tm, final):
    n, d = x1.shape
    ib = 8
    eb = ib * PEER_KEYS
    nblk = u_bf.shape[1] // eb
    nb_ctx, bpl = n_ctx // tm, lat_len // tm
    grp = lambda i: _group_of_block(i, nb_ctx, bpl)
    once = dict(pipeline_mode=pl.Buffered(1))
    cur = lambda s: jnp.minimum(s, nblk - 1)
    prev = lambda s: jnp.maximum(s - 1, 0)
    in_specs = [
        pl.BlockSpec((tm, d), lambda i, s: (i, 0), **once),
        pl.BlockSpec((None, eb, d), lambda i, s: (l, cur(s), 0)),
        pl.BlockSpec((None, d, eb), lambda i, s: (l, 0, prev(s))),
        pl.BlockSpec((PEER_HEADS, PEER_KEYS, tm), lambda i, s: (0, 0, i), **once),
        pl.BlockSpec((PEER_HEADS, PEER_KEYS, tm), lambda i, s: (0, 0, i), **once),
        pl.BlockSpec((PEER_HEADS, ib, tm), lambda i, s: (0, cur(s), i)),
        pl.BlockSpec((PEER_HEADS, ib, tm), lambda i, s: (0, cur(s), i)),
        pl.BlockSpec((tm, d), lambda i, s: (i, 0), **once),
        pl.BlockSpec((None, None, None, 1, d), lambda i, s: (l, grp(i), 5, 0, 0)),
    ]
    args = [h2, u_bf, vt_bf, r1, e1, cnt, e0, x1, mods]
    if final:
        in_specs.append(pl.BlockSpec((1, d), lambda i, s: (0, 0)))
        args.append(final_g.reshape(1, d))
    nstage = 2 * PEER_HEADS
    return pl.pallas_call(
        functools.partial(_peer_kernel, final=final, ib=ib, nblk=nblk),
        out_shape=jax.ShapeDtypeStruct((n, d), F32),
        grid=(n // tm, nblk + 1),
        in_specs=in_specs,
        out_specs=pl.BlockSpec((tm, d), lambda i, s: (i, 0)),
        scratch_shapes=[pltpu.VMEM((d, tm), F32),
                        pltpu.VMEM((eb, tm), F32), pltpu.VMEM((eb, tm), F32),
                        pltpu.VMEM((eb, tm), BF16), pltpu.VMEM((eb, tm), BF16),
                        pltpu.VMEM((nstage, 16, tm), BF16), pltpu.VMEM((nstage, 16, tm), BF16)],
        compiler_params=_cparams(("parallel", "arbitrary")),
        name="peer_experts",
    )(*args)


def kernel(x_prompt, x_sample, cache_k_win, cache_v_win, cache_k_axial, cache_v_axial, cache_k_diff,
           cache_v_diff, c, c_ctx, norm1_g, w_ada, b_ada, w_in, win_sink, ax_q_g, ax_k_g, lam_q1, lam_k1,
           lam_q2, lam_k2, diff_subln_g, w_out, norm2_g, peer_wq, peer_keys, peer_u, peer_v, final_g):
    batch, seq, d = x_prompt.shape
    dec_batch, lat_len, _ = x_sample.shape
    depth = w_in.shape[0]
    past = cache_k_win.shape[2]
    n_ctx = batch * seq
    n_lat = dec_batch * lat_len
    n = n_ctx + n_lat
    tm = 512
    tq_ctx = seq
    tq_lat = 256

    x = jnp.concatenate([x_prompt.reshape(n_ctx, d), x_sample.reshape(n_lat, d)], axis=0)
    cond = jnp.zeros((8, d), F32).at[0].set(c_ctx).at[1:1 + dec_batch].set(c)
    mods = _ada_call(cond, w_ada, b_ada).reshape(depth, 8, 6, 1, d)
    kw = _kw_call(peer_keys, peer_wq)
    w_in_bf = w_in.astype(BF16)
    w_out_bf = w_out.astype(BF16)
    u_bf = peer_u.astype(BF16)
    vt_bf = _vt_call(peer_v)
    tables = _rope_tables(lat_len)

    cache_a = (cache_k_win.reshape(dec_batch, depth, past, -1), cache_v_win.reshape(dec_batch, depth, past, -1))
    cache_b = (cache_k_axial.reshape(dec_batch, depth, past, -1), cache_v_axial.reshape(dec_batch, depth, past, -1))
    cache_c = (cache_k_diff.reshape(dec_batch, depth, past, -1), cache_v_diff.reshape(dec_batch, depth, past, -1))
    lam_params = (lam_q1, lam_k1, lam_q2, lam_k2)

    o = jnp.zeros((n, d), BF16)
    z = jnp.zeros((n, IN_COLS), BF16)
    kv_out = [[] for _ in range(6)]
    for l in range(depth):
        lam_init = 0.8 - 0.6 * math.exp(-0.3 * l)
        geo = dict(n_ctx=n_ctx, lat_len=lat_len, tm=tm)
        res = _in_call(l, x, mods, norm1_g, w_in_bf, ax_q_g, ax_k_g, None, z, latent=False, **geo)
        z = res[0]
        for k in range(6):
            kv_out[k].append(res[1 + k])
        z = _in_call(l, x, mods, norm1_g, w_in_bf, ax_q_g, ax_k_g, tables, z, latent=True, **geo)

        ctx = dict(nbatch=batch, seq=seq, row0=0, tq=tq_ctx)
        lat = dict(nbatch=dec_batch, seq=lat_len, row0=n_ctx, tq=tq_lat)
        win = dict(qcol=QA, kcol=KA, vcol=VA, ocol=0, heads=WIN_HEADS, kvh=WIN_KV, sink=win_sink)
        axl = dict(qcol=QB, kcol=KB, vcol=VB, ocol=WIN_HEADS * HEAD_DIM, heads=AX_HEADS, kvh=AX_KV)
        o = _gqa_call(l, z, o, **win, **ctx)
        o = _gqa_call(l, z, o, **axl, **ctx)
        o = _diff_call(l, z, o, lam_params, diff_subln_g, lam_init=lam_init, **ctx)
        o = _gqa_call(l, z, o, **win, **lat, window=True, cache=cache_a)
        lat2 = dict(lat, tq=2 * tq_lat if lat_len % (2 * tq_lat) == 0 else tq_lat)
        o = _gqa_call(l, z, o, **axl, **lat2, cache=cache_b)
        o = _diff_call(l, z, o, lam_params, diff_subln_g, lam_init=lam_init, cache=cache_c, **lat2)

        x1, h2 = _out_call(l, o, x, w_out_bf, mods, norm2_g, **geo)
        s_t = _score_call(l, kw, h2, tm=tm)
        r1, e1, cnt, e0 = _router_call(s_t)
        x = _peer_call(l, h2, u_bf, vt_bf, r1, e1, cnt, e0, x1, mods, final_g,
                       final=(l == depth - 1), **geo)

    y_prompt = x[:n_ctx].reshape(batch, seq, d)
    y_sample = x[n_ctx:].reshape(dec_batch, lat_len, d)
    outs = []
    for k, nh in enumerate((WIN_KV, WIN_KV, AX_KV, AX_KV, DIFF_HEADS, DIFF_HEADS)):
        outs.append(jnp.stack([a.reshape(batch, seq, nh, HEAD_DIM) for a in kv_out[k]], axis=1))
    return (y_prompt, y_sample, *outs)
```
